```python
import math
import jax, jax.numpy as jnp
from jax import lax
import numpy as np

D_MODEL = 4096
BATCH = 4
SEQ = 2048
DEPTH = 1

N_MEM = 256
POOL_WINDOWS = (2, 4, 8, 16)
N_POOL_GROUPS = 4
POOL_WIDTH = D_MODEL // 2
POOL_GROUP = POOL_WIDTH // N_POOL_GROUPS
POOL_OUT_GROUP = D_MODEL // N_POOL_GROUPS
FOX_HEAD_DIM = 128
FOX_HEADS = (D_MODEL // 2) // FOX_HEAD_DIM
FOX_WIDTH = FOX_HEADS * FOX_HEAD_DIM
Q_BLOCK = 128
FORGET_W_SCALE = 0.3
FORGET_BIAS_MEAN = 3.0
MEM_HEADS = 4
MEM_HEAD_DIM = 256
MEM_WIDTH = MEM_HEADS * MEM_HEAD_DIM
N_EXPERTS = 32
TOP_K = 4
EXPERT_FF = 1536
SWIGLU_LIMIT = 7.0
SWIGLU_ALPHA = 1.702
ROUTE_BLOCK = 128
EPS = 1e-5
OFF_Q = POOL_WIDTH
OFF_K = OFF_Q + FOX_WIDTH
OFF_V = OFF_K + FOX_WIDTH
OFF_F = OFF_V + FOX_WIDTH
OFF_GATE = OFF_F + FOX_HEADS
IN_WIDTH = OFF_GATE + 2 * D_MODEL

kernel_name = "hybrid_pool_fox_memxattn_moe"


def rmsnorm(x, g):
    xf = x.astype(jnp.float32)
    y = xf * lax.rsqrt(jnp.mean(xf * xf, axis=-1, keepdims=True) + EPS)
    return (y * g.astype(jnp.float32)).astype(x.dtype)


def causal_pool_mixer(u, w_pool, scale):
    B, S, _ = u.shape
    ug = u.reshape(B, S, N_POOL_GROUPS, POOL_GROUP)
    pos = jnp.arange(1, S + 1, dtype=jnp.float32)
    outs = []
    for g, w in enumerate(POOL_WINDOWS):
        v = ug[:, :, g].astype(jnp.float32)
        cs = jnp.cumsum(v, axis=1)
        lagged = jnp.pad(cs, ((0, 0), (w, 0), (0, 0)))[:, :S]
        cnt = jnp.minimum(pos, float(w))[None, :, None]
        outs.append((cs - lagged) / cnt - v)
    pooled = jnp.stack(outs, axis=2).astype(u.dtype)
    y = jnp.einsum("bsgc,gcd->bsgd", pooled, w_pool).reshape(B, S, D_MODEL)
    return y * scale


def forgetting_attention(q, k, v, log_f):
    B, H, S, Dh = q.shape
    c = jnp.cumsum(log_f, axis=-1)
    scale = Dh ** -0.5
    outs = []
    for i in range(S // Q_BLOCK):
        lo, hi = i * Q_BLOCK, (i + 1) * Q_BLOCK
        qb = q[:, :, lo:hi]
        kp, vp = k[:, :, :hi], v[:, :, :hi]
        s = jnp.einsum("bhqd,bhkd->bhqk", qb, kp).astype(jnp.float32) * scale
        s = s + c[:, :, lo:hi, None] - c[:, :, None, :hi]
        qpos = jnp.arange(lo, hi)[:, None]
        kpos = jnp.arange(hi)[None, :]
        s = jnp.where(kpos <= qpos, s, -jnp.inf)
        p = jax.nn.softmax(s, axis=-1)
        outs.append(jnp.einsum("bhqk,bhkd->bhqd", p.astype(v.dtype), vp))
    return jnp.concatenate(outs, axis=2)


def memory_cross_attention(h, m, w_q, w_kv, w_o):
    B, S, _ = h.shape
    M = m.shape[1]
    q = jnp.einsum("bsd,de->bse", h, w_q).reshape(B, S, MEM_HEADS, MEM_HEAD_DIM)
    kv = jnp.einsum("bmd,de->bme", m, w_kv).reshape(B, M, 2, MEM_HEADS, MEM_HEAD_DIM)
    s = jnp.einsum("bqhd,bkhd->bhqk", q, kv[:, :, 0]).astype(jnp.float32) * (MEM_HEAD_DIM ** -0.5)
    p = jax.nn.softmax(s, axis=-1)
    o = jnp.einsum("bhqk,bkhd->bqhd", p.astype(h.dtype), kv[:, :, 1]).reshape(B, S, MEM_WIDTH)
    return jnp.einsum("bse,ed->bsd", o, w_o)


def moe_ffn(h, w_router, b_router, w_gate_up, b_gate_up, w_down, b_down):
    B, S, D = h.shape
    T = B * S
    ht = h.reshape(T, D)
    logits = (jnp.einsum("td,de->te", ht, w_router) + b_router).astype(jnp.float32)
    top_vals, top_idx = lax.top_k(logits, TOP_K)
    gates = jax.nn.softmax(top_vals, axis=-1)
    A = T * TOP_K
    e_flat = top_idx.reshape(A).astype(jnp.int32)
    tok_flat = jnp.arange(A, dtype=jnp.int32) // TOP_K
    w_flat = gates.reshape(A)
    order = jnp.argsort(e_flat)
    e_sorted = e_flat[order]
    counts = jnp.zeros((N_EXPERTS,), jnp.int32).at[e_flat].add(1)
    starts = jnp.cumsum(counts) - counts
    padded = (counts + ROUTE_BLOCK - 1) // ROUTE_BLOCK * ROUTE_BLOCK
    pad_ends = jnp.cumsum(padded)
    pad_starts = pad_ends - padded
    dest = pad_starts[e_sorted] + (jnp.arange(A, dtype=jnp.int32) - starts[e_sorted])
    R = (A + ROUTE_BLOCK - 1) // ROUTE_BLOCK * ROUTE_BLOCK + N_EXPERTS * ROUTE_BLOCK
    NB = R // ROUTE_BLOCK
    row_tok = jnp.zeros((R,), jnp.int32).at[dest].set(tok_flat[order])
    row_w = jnp.zeros((R,), w_flat.dtype).at[dest].set(w_flat[order])
    block_start = jnp.arange(NB, dtype=jnp.int32) * ROUTE_BLOCK
    block_e = jnp.minimum(jnp.searchsorted(pad_ends, block_start, side="right"), N_EXPERTS - 1).astype(jnp.int32)

    def expert_block(y, blk):
        tok, wt, e = blk
        xb = ht[tok]
        gu = xb @ w_gate_up[e] + b_gate_up[e]
        gate = jnp.minimum(gu[:, 0::2], SWIGLU_LIMIT)
        up = jnp.clip(gu[:, 1::2], -SWIGLU_LIMIT, SWIGLU_LIMIT)
        act = (up + 1.0) * gate * jax.nn.sigmoid(SWIGLU_ALPHA * gate)
        out = act @ w_down[e] + b_down[e]
        return y.at[tok].add(out * wt[:, None].astype(out.dtype)), None

    y0 = jnp.zeros((T, D), h.dtype)
    y, _ = lax.scan(expert_block, y0,
                    (row_tok.reshape(NB, ROUTE_BLOCK), row_w.reshape(NB, ROUTE_BLOCK), block_e))
    return y.reshape(B, S, D)


def setup_inputs(seed: int = 0) -> dict:
    key = jax.random.key(seed)
    ks = jax.random.split(key, 24)
    L, D, E, F = DEPTH, D_MODEL, N_EXPERTS, EXPERT_FF
    f32 = jnp.float32

    def nrm(k, shape, fan_in):
        return jax.random.normal(k, shape, f32) * (fan_in ** -0.5)

    def gain(k, shape):
        return 1.0 + 0.05 * jax.random.normal(k, shape, f32)

    def small(k, shape):
        return 0.01 * jax.random.normal(k, shape, f32)

    x = jax.random.normal(ks[0], (BATCH, SEQ, D), f32)
    mem = jax.random.normal(ks[1], (BATCH, N_MEM, D), f32)
    g_mix = gain(ks[2], (L, D))
    w_in = nrm(ks[3], (L, D, IN_WIDTH), D)
    w_in = w_in.at[:, :, OFF_F:OFF_GATE].multiply(FORGET_W_SCALE)
    b_in = small(ks[4], (L, IN_WIDTH))
    b_in = b_in.at[:, OFF_F:OFF_GATE].add(FORGET_BIAS_MEAN + 0.5 * jax.random.normal(ks[5], (L, FOX_HEADS), f32))
    w_pool = nrm(ks[6], (L, N_POOL_GROUPS, POOL_GROUP, POOL_OUT_GROUP), POOL_GROUP)
    pool_scale = gain(ks[7], (L, D))
    w_fox_o = nrm(ks[8], (L, FOX_WIDTH, D), FOX_WIDTH)
    w_out = nrm(ks[9], (L, D, D), D)
    g_mem_q = gain(ks[10], (L, D))
    g_mem_kv = gain(ks[11], (L, D))
    w_mem_q = nrm(ks[12], (L, D, MEM_WIDTH), D)
    w_mem_kv = nrm(ks[13], (L, D, 2 * MEM_WIDTH), D)
    w_mem_o = nrm(ks[14], (L, MEM_WIDTH, D), MEM_WIDTH)
    g_moe = gain(ks[15], (L, D))
    w_router = nrm(ks[16], (L, D, E), D)
    b_router = small(ks[17], (L, E))
    w_gate_up = nrm(ks[18], (L, E, D, 2 * F), D)
    b_gate_up = small(ks[19], (L, E, 2 * F))
    w_down = nrm(ks[20], (L, E, F, D), F)
    b_down = small(ks[21], (L, E, D))
    g_final = gain(ks[22], (D,))
    return {"x": x, "mem": mem, "g_mix": g_mix, "w_in": w_in, "b_in": b_in,
            "w_pool": w_pool, "pool_scale": pool_scale, "w_fox_o": w_fox_o, "w_out": w_out,
            "g_mem_q": g_mem_q, "g_mem_kv": g_mem_kv, "w_mem_q": w_mem_q, "w_mem_kv": w_mem_kv,
            "w_mem_o": w_mem_o, "g_moe": g_moe, "w_router": w_router, "b_router": b_router,
            "w_gate_up": w_gate_up, "b_gate_up": b_gate_up, "w_down": w_down, "b_down": b_down,
            "g_final": g_final}


def reference(x, mem, g_mix, w_in, b_in, w_pool, pool_scale, w_fox_o, w_out,
              g_mem_q, g_mem_kv, w_mem_q, w_mem_kv, w_mem_o, g_moe, w_router, b_router,
              w_gate_up, b_gate_up, w_down, b_down, g_final):
    B, S, D = x.shape
    for l in range(DEPTH):
        h = rmsnorm(x, g_mix[l])
        proj = jnp.einsum("bsd,de->bse", h, w_in[l]) + b_in[l]
        u_pool, q, k, v, f_logit, gate_logit = jnp.split(proj, [OFF_Q, OFF_K, OFF_V, OFF_F, OFF_GATE], axis=-1)
        pool_out = causal_pool_mixer(u_pool, w_pool[l], pool_scale[l])
        to_heads = lambda t: t.reshape(B, S, FOX_HEADS, FOX_HEAD_DIM).transpose(0, 2, 1, 3)
        log_f = jax.nn.log_sigmoid(f_logit.astype(jnp.float32)).transpose(0, 2, 1)
        att = forgetting_attention(to_heads(q), to_heads(k), to_heads(v), log_f)
        att = att.transpose(0, 2, 1, 3).reshape(B, S, FOX_WIDTH)
        fox_out = jnp.einsum("bse,ed->bsd", att, w_fox_o[l])
        gates = jax.nn.sigmoid(gate_logit).reshape(B, S, 2, D)
        merged = gates[:, :, 0] * pool_out + gates[:, :, 1] * fox_out
        x = x + jnp.einsum("bsd,de->bse", merged, w_out[l])
        x = x + memory_cross_attention(rmsnorm(x, g_mem_q[l]), rmsnorm(mem, g_mem_kv[l]),
                                       w_mem_q[l], w_mem_kv[l], w_mem_o[l])
        x = x + moe_ffn(rmsnorm(x, g_moe[l]), w_router[l], b_router[l],
                        w_gate_up[l], b_gate_up[l], w_down[l], b_down[l])
    return rmsnorm(x, g_final)
```

```python
import functools

import jax
import jax.numpy as jnp
from jax import lax
from jax.experimental import pallas as pl
from jax.experimental.pallas import tpu as pltpu

F32 = jnp.float32
BF16 = jnp.bfloat16

EPS = 1e-5
POOL_WINDOWS = (2, 4, 8, 16)
POOL_HALO = 16
MEM_HEADS = 4
TOP_K = 4
SWIGLU_LIMIT = 7.0
SWIGLU_ALPHA = 1.702
NEG_BIG = -1e30

LANES = 128
VMEM_LIMIT_BYTES = 56 * 1024 * 1024


def _tile(dim, pref):
    t = pref
    while t >= 8:
        if dim % t == 0:
            return t
        t //= 2
    return dim


def _params(*sem):
    return pltpu.CompilerParams(dimension_semantics=sem, vmem_limit_bytes=VMEM_LIMIT_BYTES)


def _rms(x, g):
    ms = jnp.mean(x * x, axis=-1, keepdims=True)
    return x * lax.rsqrt(ms + EPS) * g


def _split3(x):
    hi = x.astype(BF16)
    r1 = x - hi.astype(F32)
    mid = r1.astype(BF16)
    lo = (r1 - mid.astype(F32)).astype(BF16)
    return hi, mid, lo


def _norm_mm_kernel(*refs, has_bias, sig_from, with_f):
    it = iter(refs)
    x_ref, g_ref, w_ref = next(it), next(it), next(it)
    b_ref = next(it) if has_bias else None
    if with_f:
        wf_ref, bf_ref = next(it), next(it)
    o_ref = next(it)
    if with_f:
        lf_ref = next(it)
    h_ref = next(it)
    j = pl.program_id(1)

    @pl.when(j == 0)
    def _():
        h = _rms(x_ref[...], g_ref[...]).astype(BF16)
        h_ref[...] = h
        if with_f:
            f = lax.dot_general(wf_ref[...], h, (((1,), (1,)), ((), ())), preferred_element_type=F32) + bf_ref[...]
            lf_ref[...] = jnp.minimum(f, 0.0) - jnp.log1p(jnp.exp(-jnp.abs(f)))

    acc = jnp.dot(h_ref[...], w_ref[...], preferred_element_type=F32)
    if has_bias:
        acc = acc + b_ref[...]
    if sig_from is None:
        o_ref[...] = acc.astype(o_ref.dtype)
    else:
        @pl.when(j < sig_from)
        def _():
            o_ref[...] = acc.astype(o_ref.dtype)

        @pl.when(j >= sig_from)
        def _():
            o_ref[...] = jax.nn.sigmoid(acc).astype(o_ref.dtype)


def _norm_matmul(x, g, w, b=None, *, out_dtype, sig_from_col=None, wf_t=None, bf_t=None, tm=512, tn=1024):
    M, K = x.shape
    N = w.shape[1]
    tm, tn = _tile(M, tm), _tile(N, tn)
    with_f = wf_t is not None
    sig_from = None if sig_from_col is None else sig_from_col // tn
    if sig_from_col is not None:
        assert sig_from_col % tn == 0
    in_specs = [pl.BlockSpec((tm, K), lambda i, j: (i, 0)),
                pl.BlockSpec((1, K), lambda i, j: (0, 0)),
                pl.BlockSpec((K, tn), lambda i, j: (0, j))]
    args = [x, g.reshape(1, K), w]
    if b is not None:
        in_specs.append(pl.BlockSpec((1, tn), lambda i, j: (0, j)))
        args.append(b.reshape(1, N))
    out_shape = [jax.ShapeDtypeStruct((M, N), out_dtype)]
    out_specs = [pl.BlockSpec((tm, tn), lambda i, j: (i, j))]
    if with_f:
        H = wf_t.shape[0]
        in_specs += [pl.BlockSpec((H, K), lambda i, j: (0, 0)), pl.BlockSpec((H, 1), lambda i, j: (0, 0))]
        args += [wf_t, bf_t.reshape(H, 1)]
        out_shape.append(jax.ShapeDtypeStruct((H, M), F32))
        out_specs.append(pl.BlockSpec((H, tm), lambda i, j: (0, i)))
    res = pl.pallas_call(
        functools.partial(_norm_mm_kernel, has_bias=b is not None, sig_from=sig_from, with_f=with_f),
        grid=(M // tm, N // tn),
        in_specs=in_specs, out_specs=out_specs, out_shape=out_shape,
        scratch_shapes=[pltpu.VMEM((tm, K), BF16)],
        compiler_params=_params("parallel", "arbitrary"),
    )(*args)
    return res if with_f else res[0]


def _mm_kernel(x_ref, w_ref, *refs, mode):
    acc = jnp.dot(x_ref[...], w_ref[...], preferred_element_type=F32)
    if mode == "merge":
        pp_ref, g1_ref, o_ref = refs
        o_ref[...] = (pp_ref[...] + g1_ref[...].astype(F32) * acc).astype(o_ref.dtype)
    else:
        r_ref, o_ref = refs
        o_ref[...] = (r_ref[...] + acc).astype(o_ref.dtype)


def _matmul(x, w, extras, extra_col_off, *, mode, out_dtype, tm=512, tn=1024):
    M, K = x.shape
    N = w.shape[1]
    tm, tn = _tile(M, tm), _tile(N, tn)
    while any(off % tn for off in extra_col_off):
        tn //= 2
    assert tn % LANES == 0
    in_specs = [pl.BlockSpec((tm, K), lambda i, j: (i, 0)), pl.BlockSpec((K, tn), lambda i, j: (0, j))]
    for off in extra_col_off:
        in_specs.append(pl.BlockSpec((tm, tn), lambda i, j, o=off // tn: (i, j + o)))
    return pl.pallas_call(
        functools.partial(_mm_kernel, mode=mode),
        grid=(M // tm, N // tn),
        in_specs=in_specs,
        out_specs=pl.BlockSpec((tm, tn), lambda i, j: (i, j)),
        out_shape=jax.ShapeDtypeStruct((M, N), out_dtype),
        compiler_params=_params("parallel", "parallel"),
    )(x, w, *extras)


def _cumsum_kernel(lf_ref, c_ref):
    S = lf_ref.shape[1]
    row = lax.broadcasted_iota(jnp.int32, (S, S), 0)
    col = lax.broadcasted_iota(jnp.int32, (S, S), 1)
    upper = (row <= col).astype(BF16)
    c = jnp.zeros(lf_ref.shape, F32)
    for part in _split3(lf_ref[...]):
        c = c + jnp.dot(part, upper, preferred_element_type=F32)
    c_ref[...] = c


def _forget_cumsum(lf_t, B, S):
    H = lf_t.shape[0]
    return pl.pallas_call(
        _cumsum_kernel,
        grid=(B,),
        in_specs=[pl.BlockSpec((H, S), lambda b: (0, b))],
        out_specs=pl.BlockSpec((None, H, S), lambda b: (b, 0, 0)),
        out_shape=jax.ShapeDtypeStruct((B, H, S), F32),
        compiler_params=_params("parallel"),
    )(lf_t)


def _pool_kernel(u_ref, halo_ref, w_ref, sc_ref, g0_ref, o_ref, ext_ref, pooled_ref, *, seq_len):
    g = pl.program_id(0)
    i = pl.program_id(1)
    tp = u_ref.shape[0]
    pos0 = (i * tp) % seq_len
    u = u_ref[...]
    ext_ref[pl.ds(POOL_HALO, tp), :] = u
    ext_ref[pl.ds(0, POOL_HALO), :] = jnp.where(pos0 == 0, 0.0, halo_ref[...])
    pos = pos0 + lax.broadcasted_iota(jnp.int32, (tp, 1), 0)
    for gi, win in enumerate(POOL_WINDOWS):
        @pl.when(g == gi)
        def _(win=win):
            acc = u
            for k in range(1, win):
                acc = acc + ext_ref[pl.ds(POOL_HALO - k, tp), :]
            cnt = jnp.minimum(pos + 1, win).astype(F32)
            pooled_ref[...] = (acc / cnt - u).astype(BF16)
    y = jnp.dot(pooled_ref[...], w_ref[...], preferred_element_type=F32)
    o_ref[...] = g0_ref[...].astype(F32) * (y * sc_ref[...])


def _pool_mixer(u, w_pool, scale, gates, gate0_col, seq_len, *, tp=512):
    T = u.shape[0]
    G, C, Do = w_pool.shape
    tp = _tile(seq_len, tp)
    assert tp % POOL_HALO == 0 and gate0_col % Do == 0
    hb = tp // POOL_HALO
    return pl.pallas_call(
        functools.partial(_pool_kernel, seq_len=seq_len),
        grid=(G, T // tp),
        in_specs=[pl.BlockSpec((tp, C), lambda g, i: (i, g)),
                  pl.BlockSpec((POOL_HALO, C), lambda g, i: (jnp.maximum(i * hb - 1, 0), g)),
                  pl.BlockSpec((None, C, Do), lambda g, i: (g, 0, 0)),
                  pl.BlockSpec((1, Do), lambda g, i: (0, g)),
                  pl.BlockSpec((tp, Do), lambda g, i, o=gate0_col // Do: (i, g + o))],
        out_specs=pl.BlockSpec((tp, Do), lambda g, i: (i, g)),
        out_shape=jax.ShapeDtypeStruct((T, G * Do), F32),
        scratch_shapes=[pltpu.VMEM((tp + POOL_HALO, C), F32), pltpu.VMEM((tp, C), BF16)],
        compiler_params=_params("parallel", "parallel"),
    )(u, u, w_pool, scale.reshape(1, G * Do), gates)


def _fox_kernel(q_ref, k_ref, v_ref, cq_ref, ck_ref, o_ref, *, scale):
    h = pl.program_id(1)
    qi = pl.program_id(2)
    tq, dh = q_ref.shape
    q = q_ref[...]
    lane = lax.broadcasted_iota(jnp.int32, cq_ref.shape, 1)
    cq = jnp.sum(jnp.where(lane == h, cq_ref[...], 0.0), axis=1, keepdims=True)

    def scores(j):
        k = k_ref[pl.ds(pl.multiple_of(j * tq, tq), tq), :]
        s = lax.dot_general(q, k, (((1,), (1,)), ((), ())), preferred_element_type=F32) * scale
        ck = ck_ref[:, pl.ds(pl.multiple_of(j * tq, tq), tq)]
        return s + (cq - ck)

    def update(j, s, carry):
        m, l, acc = carry
        m_new = jnp.maximum(m, jnp.max(s, axis=1, keepdims=True))
        alpha = jnp.exp(m - m_new)
        p = jnp.exp(s - m_new)
        v = v_ref[pl.ds(pl.multiple_of(j * tq, tq), tq), :]
        acc = alpha * acc + jnp.dot(p.astype(BF16), v, preferred_element_type=F32)
        return m_new, alpha * l + jnp.sum(p, axis=1, keepdims=True), acc

    init = (jnp.full((tq, 1), NEG_BIG, F32), jnp.zeros((tq, 1), F32), jnp.zeros((tq, dh), F32))
    carry = lax.fori_loop(0, qi, lambda j, c: update(j, scores(j), c), init)
    row = lax.broadcasted_iota(jnp.int32, (tq, tq), 0)
    col = lax.broadcasted_iota(jnp.int32, (tq, tq), 1)
    s = jnp.where(col <= row, scores(qi), NEG_BIG)
    m, l, acc = update(qi, s, carry)
    o_ref[...] = (acc / l).astype(o_ref.dtype)


def _fox_attention(qkvg, c, c_t, B, S, H, dh, *, tq=256):
    T = B * S
    tq = _tile(S, tq)
    nq = S // tq
    return pl.pallas_call(
        functools.partial(_fox_kernel, scale=dh ** -0.5),
        grid=(B, H, nq),
        in_specs=[pl.BlockSpec((tq, dh), lambda b, h, i: (b * nq + i, h)),
                  pl.BlockSpec((S, dh), lambda b, h, i: (b, H + h)),
                  pl.BlockSpec((S, dh), lambda b, h, i: (b, 2 * H + h)),
                  pl.BlockSpec((None, tq, H), lambda b, h, i: (b, i, 0)),
                  pl.BlockSpec((None, None, 1, S), lambda b, h, i: (b, h, 0, 0))],
        out_specs=pl.BlockSpec((tq, dh), lambda b, h, i: (b * nq + i, h)),
        out_shape=jax.ShapeDtypeStruct((T, H * dh), BF16),
        compiler_params=_params("parallel", "parallel", "parallel"),
    )(qkvg, qkvg, qkvg, c, c_t)


def _mem_attn_kernel(q_ref, kv_ref, o_ref, *, heads):
    width = q_ref.shape[1]
    dh = width // heads
    scale = dh ** -0.5
    for hd in range(heads):
        q = q_ref[:, hd * dh:(hd + 1) * dh]
        k = kv_ref[:, hd * dh:(hd + 1) * dh]
        v = kv_ref[:, width + hd * dh:width + (hd + 1) * dh]
        s = lax.dot_general(q, k, (((1,), (1,)), ((), ())), preferred_element_type=F32) * scale
        p = jnp.exp(s - jnp.max(s, axis=1, keepdims=True))
        p = p / jnp.sum(p, axis=1, keepdims=True)
        o_ref[:, hd * dh:(hd + 1) * dh] = jnp.dot(p.astype(BF16), v, preferred_element_type=F32).astype(o_ref.dtype)


def _mem_attention(q, kv, B, S, n_mem, *, tq=512):
    T, width = q.shape
    tq = _tile(S, tq)
    nq = S // tq
    return pl.pallas_call(
        functools.partial(_mem_attn_kernel, heads=MEM_HEADS),
        grid=(B, nq),
        in_specs=[pl.BlockSpec((tq, width), lambda b, i: (b * nq + i, 0)),
                  pl.BlockSpec((n_mem, 2 * width), lambda b, i: (b, 0))],
        out_specs=pl.BlockSpec((tq, width), lambda b, i: (b * nq + i, 0)),
        out_shape=jax.ShapeDtypeStruct((T, width), BF16),
        compiler_params=_params("parallel", "parallel"),
    )(q, kv)


def _router_kernel(x_ref, g_ref, whi_ref, wlo_ref, b_ref, oi_ref, og_ref, cnt_ref, carry_ref, *, n_experts):
    i = pl.program_id(0)
    tm = x_ref.shape[0]

    @pl.when(i == 0)
    def _():
        carry_ref[...] = jnp.zeros_like(carry_ref)

    h = _rms(x_ref[...], g_ref[...])
    h_hi = h.astype(BF16)
    h_lo = (h - h_hi.astype(F32)).astype(BF16)
    logits = (jnp.dot(h_hi, whi_ref[...], preferred_element_type=F32)
              + jnp.dot(h_hi, wlo_ref[...], preferred_element_type=F32)
              + jnp.dot(h_lo, whi_ref[...], preferred_element_type=F32)) + b_ref[...]
    lane = lax.broadcasted_iota(jnp.int32, (tm, LANES), 1).astype(F32)
    work = jnp.where(lane < n_experts, logits, -jnp.inf)
    vals, idxs = [], []
    for _ in range(TOP_K):
        m = jnp.max(work, axis=1, keepdims=True)
        idx = jnp.min(jnp.where(work == m, lane, float(LANES)), axis=1, keepdims=True)
        vals.append(m)
        idxs.append(idx)
        work = jnp.where(lane == idx, -jnp.inf, work)
    exps = [jnp.exp(v - vals[0]) for v in vals]
    denom = exps[0] + exps[1] + exps[2] + exps[3]
    onehots = [(lane == idx).astype(F32) for idx in idxs]
    chosen = onehots[0] + onehots[1] + onehots[2] + onehots[3]
    row = lax.broadcasted_iota(jnp.int32, (tm, tm), 0)
    col = lax.broadcasted_iota(jnp.int32, (tm, tm), 1)
    before = jnp.dot((col < row).astype(BF16), chosen.astype(BF16), preferred_element_type=F32) + carry_ref[...]
    out_i = jnp.zeros((tm, LANES), F32)
    out_g = jnp.zeros((tm, LANES), F32)
    for k in range(TOP_K):
        rank = jnp.sum(onehots[k] * before, axis=1, keepdims=True)
        out_i = jnp.where(lane == k, idxs[k], out_i)
        out_i = jnp.where(lane == TOP_K + k, rank, out_i)
        out_g = jnp.where(lane == k, exps[k] / denom, out_g)
    oi_ref[...] = out_i.astype(jnp.int32)
    og_ref[...] = out_g
    carry_ref[...] = carry_ref[...] + jnp.sum(chosen, axis=0, keepdims=True)
    cnt_ref[...] = carry_ref[...]


def _router(x, g, w_router, b_router, *, tm=512):
    T, D = x.shape
    E = w_router.shape[1]
    tm = _tile(T, tm)
    w_pad = jnp.zeros((D, LANES), F32).at[:, :E].set(w_router)
    w_hi = w_pad.astype(BF16)
    w_lo = (w_pad - w_hi.astype(F32)).astype(BF16)
    b_pad = jnp.zeros((1, LANES), F32).at[0, :E].set(b_router)
    full = lambda i: (0, 0)
    return pl.pallas_call(
        functools.partial(_router_kernel, n_experts=E),
        grid=(T // tm,),
        in_specs=[pl.BlockSpec((tm, D), lambda i: (i, 0)), pl.BlockSpec((1, D), full),
                  pl.BlockSpec((D, LANES), full), pl.BlockSpec((D, LANES), full), pl.BlockSpec((1, LANES), full)],
        out_specs=[pl.BlockSpec((tm, LANES), lambda i: (i, 0)), pl.BlockSpec((tm, LANES), lambda i: (i, 0)),
                   pl.BlockSpec((1, LANES), full)],
        out_shape=[jax.ShapeDtypeStruct((T, LANES), jnp.int32), jax.ShapeDtypeStruct((T, LANES), F32),
                   jax.ShapeDtypeStruct((1, LANES), F32)],
        scratch_shapes=[pltpu.VMEM((1, LANES), F32)],
        compiler_params=_params("arbitrary"),
    )(x, g.reshape(1, D), w_hi, w_lo, b_pad)


def _row_copy(src_hbm, row, buf, slot, r, sem):
    return pltpu.make_async_copy(src_hbm.at[pl.ds(row, 1)], buf.at[slot, pl.ds(r, 1)], sem.at[slot])


def _start_rows(idx_ref, src_hbm, buf, slot, sem, n_rows):
    def body(r, _):
        _row_copy(src_hbm, idx_ref[0, 0, r], buf, slot, r, sem).start()
        return 0
    lax.fori_loop(0, n_rows, body, 0, unroll=8)


def _wait_rows(src_hbm, buf, slot, sem, n_rows):
    pltpu.make_async_copy(src_hbm.at[pl.ds(0, n_rows)], buf.at[slot], sem.at[slot]).wait()


def _gather_pipeline(idx_ref, nxt_ref, src_hbm, buf, sem, n_rows):
    i = pl.program_id(0)
    n = pl.num_programs(0)
    slot = i % 2

    @pl.when(i == 0)
    def _():
        _start_rows(idx_ref, src_hbm, buf, 0, sem, n_rows)

    @pl.when(i + 1 < n)
    def _():
        _start_rows(nxt_ref, src_hbm, buf, 1 - slot, sem, n_rows)

    _wait_rows(src_hbm, buf, slot, sem, n_rows)
    return slot


def _gather_norm_kernel(idx_ref, nxt_ref, g_ref, x_hbm, o_ref, buf, sem):
    tr = o_ref.shape[0]
    slot = _gather_pipeline(idx_ref, nxt_ref, x_hbm, buf, sem, tr)
    o_ref[...] = _rms(buf[slot], g_ref[...]).astype(o_ref.dtype)


def _gather_norm(x, g, row_tok, *, tr):
    T, D = x.shape
    R = row_tok.shape[0]
    nb = R // tr
    idx3 = row_tok.reshape(nb, 1, tr)
    return pl.pallas_call(
        _gather_norm_kernel,
        grid=(nb,),
        in_specs=[pl.BlockSpec((1, 1, tr), lambda i: (i, 0, 0), memory_space=pltpu.SMEM),
                  pl.BlockSpec((1, 1, tr), lambda i: (jnp.minimum(i + 1, nb - 1), 0, 0), memory_space=pltpu.SMEM),
                  pl.BlockSpec((1, D), lambda i: (0, 0)),
                  pl.BlockSpec(memory_space=pl.ANY)],
        out_specs=pl.BlockSpec((tr, D), lambda i: (i, 0)),
        out_shape=jax.ShapeDtypeStruct((R, D), BF16),
        scratch_shapes=[pltpu.VMEM((2, tr, D), F32), pltpu.SemaphoreType.DMA((2,))],
        compiler_params=_params("arbitrary"),
    )(idx3, idx3, g.reshape(1, D), x)


def _combine_kernel(idx_ref, nxt_ref, x_ref, gate_ref, g_ref, rows_hbm, o_ref, buf, sem, *, final_norm):
    tb = x_ref.shape[0]
    slot = _gather_pipeline(idx_ref, nxt_ref, rows_hbm, buf, sem, TOP_K * tb)
    y = x_ref[...]
    gates = gate_ref[...]
    for k in range(TOP_K):
        y = y + gates[:, k:k + 1] * buf[slot, pl.ds(k * tb, tb), :]
    o_ref[...] = _rms(y, g_ref[...]) if final_norm else y


def _combine(x, rows, dest, gates_slab, g, final_norm, *, tb=64):
    T, D = x.shape
    tb = _tile(T, tb)
    nb = T // tb
    idx3 = dest.reshape(nb, tb, TOP_K).transpose(0, 2, 1).reshape(nb, 1, TOP_K * tb)
    return pl.pallas_call(
        functools.partial(_combine_kernel, final_norm=final_norm),
        grid=(nb,),
        in_specs=[pl.BlockSpec((1, 1, TOP_K * tb), lambda i: (i, 0, 0), memory_space=pltpu.SMEM),
                  pl.BlockSpec((1, 1, TOP_K * tb), lambda i: (jnp.minimum(i + 1, nb - 1), 0, 0), memory_space=pltpu.SMEM),
                  pl.BlockSpec((tb, D), lambda i: (i, 0)),
                  pl.BlockSpec((tb, LANES), lambda i: (i, 0)),
                  pl.BlockSpec((1, D), lambda i: (0, 0)),
                  pl.BlockSpec(memory_space=pl.ANY)],
        out_specs=pl.BlockSpec((tb, D), lambda i: (i, 0)),
        out_shape=jax.ShapeDtypeStruct((T, D), F32),
        scratch_shapes=[pltpu.VMEM((2, TOP_K * tb, D), F32), pltpu.SemaphoreType.DMA((2,))],
        compiler_params=_params("arbitrary"),
    )(idx3, idx3, x, gates_slab, g.reshape(1, D), rows)


def _moe_up_kernel(be_ref, nu_ref, xs_ref, wg_ref, wu_ref, bg_ref, bu_ref, o_ref):
    i = pl.program_id(1)

    @pl.when(i < nu_ref[0])
    def _():
        xs = xs_ref[...]
        gate = jnp.dot(xs, wg_ref[...], preferred_element_type=F32) + bg_ref[...]
        up = jnp.dot(xs, wu_ref[...], preferred_element_type=F32) + bu_ref[...]
        gate = jnp.minimum(gate, SWIGLU_LIMIT)
        up = jnp.clip(up, -SWIGLU_LIMIT, SWIGLU_LIMIT)
        o_ref[...] = ((up + 1.0) * gate * jax.nn.sigmoid(SWIGLU_ALPHA * gate)).astype(o_ref.dtype)

    @pl.when(i >= nu_ref[0])
    def _():
        o_ref[...] = jnp.zeros_like(o_ref)


def _moe_up(xs, w_g, w_u, b_g, b_u, block_e, n_used, *, tm, tf=768):
    R, D = xs.shape
    E, _, F = w_g.shape
    tf = _tile(F, tf)
    nb = R // tm
    grid_spec = pltpu.PrefetchScalarGridSpec(
        num_scalar_prefetch=2, grid=(F // tf, nb),
        in_specs=[pl.BlockSpec((tm, D), lambda j, i, be, nu: (i, 0)),
                  pl.BlockSpec((None, D, tf), lambda j, i, be, nu: (be[i], 0, j)),
                  pl.BlockSpec((None, D, tf), lambda j, i, be, nu: (be[i], 0, j)),
                  pl.BlockSpec((None, 1, tf), lambda j, i, be, nu: (be[i], 0, j)),
                  pl.BlockSpec((None, 1, tf), lambda j, i, be, nu: (be[i], 0, j))],
        out_specs=pl.BlockSpec((tm, tf), lambda j, i, be, nu: (i, j)))
    return pl.pallas_call(
        _moe_up_kernel, grid_spec=grid_spec,
        out_shape=jax.ShapeDtypeStruct((R, F), BF16),
        compiler_params=_params("parallel", "arbitrary"),
    )(block_e, n_used, xs, w_g, w_u, b_g.reshape(E, 1, F), b_u.reshape(E, 1, F))


def _moe_down_kernel(be_ref, nu_ref, a_ref, w_ref, b_ref, o_ref):
    i = pl.program_id(1)

    @pl.when(i < nu_ref[0])
    def _():
        o_ref[...] = jnp.dot(a_ref[...], w_ref[...], preferred_element_type=F32) + b_ref[...]

    @pl.when(i >= nu_ref[0])
    def _():
        o_ref[...] = jnp.zeros_like(o_ref)


def _moe_down(act, w_d, b_d, block_e, n_used, *, tm, tn=4096):
    R, F = act.shape
    E, _, D = w_d.shape
    tn = _tile(D, tn)
    nb = R // tm
    grid_spec = pltpu.PrefetchScalarGridSpec(
        num_scalar_prefetch=2, grid=(D // tn, nb),
        in_specs=[pl.BlockSpec((tm, F), lambda j, i, be, nu: (i, 0)),
                  pl.BlockSpec((None, F, tn), lambda j, i, be, nu: (be[i], 0, j)),
                  pl.BlockSpec((None, 1, tn), lambda j, i, be, nu: (be[i], 0, j))],
        out_specs=pl.BlockSpec((tm, tn), lambda j, i, be, nu: (i, j)))
    return pl.pallas_call(
        _moe_down_kernel, grid_spec=grid_spec,
        out_shape=jax.ShapeDtypeStruct((R, D), F32),
        compiler_params=_params("parallel", "arbitrary"),
    )(block_e, n_used, act, w_d, b_d.reshape(E, 1, D))


def _moe(x, g_moe, w_router, b_router, w_gate_up, b_gate_up, w_down, b_down, g_final, final_norm, *, tm=256):
    T, D = x.shape
    E = w_router.shape[1]
    A = T * TOP_K
    tm = min(tm, A)
    slab_i, slab_g, cnt = _router(x, g_moe, w_router, b_router)
    idx = slab_i[:, :TOP_K]
    rank = slab_i[:, TOP_K:2 * TOP_K]
    counts = cnt[0, :E].astype(jnp.int32)
    padded = (counts + tm - 1) // tm * tm
    pad_ends = jnp.cumsum(padded)
    pad_starts = pad_ends - padded
    dest = pad_starts[idx] + rank
    R = (A + tm - 1) // tm * tm + E * tm
    nb = R // tm
    tok = jnp.broadcast_to(jnp.arange(T, dtype=jnp.int32)[:, None], (T, TOP_K))
    row_tok = jnp.zeros((R,), jnp.int32).at[dest.reshape(A)].set(tok.reshape(A))
    block_start = jnp.arange(nb, dtype=jnp.int32) * tm
    block_e = jnp.minimum(jnp.searchsorted(pad_ends, block_start, side="right"), E - 1).astype(jnp.int32)
    n_used = (pad_ends[-1:] // tm).astype(jnp.int32)

    xs = _gather_norm(x, g_moe, row_tok, tr=tm)
    w_g = w_gate_up[:, :, 0::2].astype(BF16)
    w_u = w_gate_up[:, :, 1::2].astype(BF16)
    act = _moe_up(xs, w_g, w_u, b_gate_up[:, 0::2], b_gate_up[:, 1::2], block_e, n_used, tm=tm)
    rows = _moe_down(act, w_down.astype(BF16), b_down, block_e, n_used, tm=tm)
    return _combine(x, rows, dest, slab_g, g_final, final_norm)


def kernel(x, mem, g_mix, w_in, b_in, w_pool, pool_scale, w_fox_o, w_out, g_mem_q, g_mem_kv, w_mem_q, w_mem_kv,
           w_mem_o, g_moe, w_router, b_router, w_gate_up, b_gate_up, w_down, b_down, g_final):
    B, S, D = x.shape
    T = B * S
    n_mem = mem.shape[1]
    depth, G, C, Do = w_pool.shape
    pool_w = G * C
    fox_w = w_fox_o.shape[1]
    H = w_in.shape[2] - pool_w - 3 * fox_w - 2 * D
    dh = fox_w // H
    off_q, off_f = pool_w, pool_w + 3 * fox_w
    off_gate = off_f + H
    xt = x.reshape(T, D)
    mt = mem.reshape(B * n_mem, D)
    for l in range(depth):
        wl, bl = w_in[l], b_in[l]
        u, lf_t = _norm_matmul(xt, g_mix[l], wl[:, :off_q].astype(BF16), bl[:off_q], out_dtype=F32,
                               wf_t=wl[:, off_f:off_gate].T.astype(BF16), bf_t=bl[off_f:off_gate])
        w_qkvg = jnp.concatenate([wl[:, off_q:off_f], wl[:, off_gate:]], axis=1).astype(BF16)
        b_qkvg = jnp.concatenate([bl[off_q:off_f], bl[off_gate:]])
        qkvg = _norm_matmul(xt, g_mix[l], w_qkvg, b_qkvg, out_dtype=BF16, sig_from_col=3 * fox_w)
        c_t = _forget_cumsum(lf_t, B, S)
        att = _fox_attention(qkvg, c_t.transpose(0, 2, 1), c_t.reshape(B, H, 1, S), B, S, H, dh)
        pp = _pool_mixer(u, w_pool[l].astype(BF16), pool_scale[l], qkvg, 3 * fox_w, S)
        merged = _matmul(att, w_fox_o[l].astype(BF16), (pp, qkvg), (0, 3 * fox_w + D), mode="merge", out_dtype=BF16)
        xt = _matmul(merged, w_out[l].astype(BF16), (xt,), (0,), mode="residual", out_dtype=F32)
        kv = _norm_matmul(mt, g_mem_kv[l], w_mem_kv[l].astype(BF16), out_dtype=BF16)
        qm = _norm_matmul(xt, g_mem_q[l], w_mem_q[l].astype(BF16), out_dtype=BF16)
        om = _mem_attention(qm, kv, B, S, n_mem)
        xt = _matmul(om, w_mem_o[l].astype(BF16), (xt,), (0,), mode="residual", out_dtype=F32)
        xt = _moe(xt, g_moe[l], w_router[l], b_router[l], w_gate_up[l], b_gate_up[l], w_down[l], b_down[l],
                  g_final, final_norm=l == depth - 1)
    return xt.reshape(B, S, D)
```

```python
import functools

import jax
import jax.numpy as jnp
from jax import lax
from jax.experimental import pallas as pl
from jax.experimental.pallas import tpu as pltpu

F32 = jnp.float32
BF16 = jnp.bfloat16

EPS = 1e-5
POOL_WINDOWS = (2, 4, 8, 16)
POOL_HALO = 16
MEM_HEADS = 4
TOP_K = 4
SWIGLU_LIMIT = 7.0
SWIGLU_ALPHA = 1.702
NEG_BIG = -1e30

LANES = 128
VMEM_LIMIT_BYTES = 56 * 1024 * 1024


def _tile(dim, pref):
    t = pref
    while t >= 8:
        if dim % t == 0:
            return t
        t //= 2
    return dim


def _params(*sem):
    return pltpu.CompilerParams(dimension_semantics=sem, vmem_limit_bytes=VMEM_LIMIT_BYTES)


def _rms(x, g):
    ms = jnp.mean(x * x, axis=-1, keepdims=True)
    return x * lax.rsqrt(ms + EPS) * g


def _split3(x):
    hi = x.astype(BF16)
    r1 = x - hi.astype(F32)
    mid = r1.astype(BF16)
    lo = (r1 - mid.astype(F32)).astype(BF16)
    return hi, mid, lo


def _norm_mm_kernel(*refs, has_bias, sig_from, with_f):
    it = iter(refs)
    x_ref, g_ref, w_ref = next(it), next(it), next(it)
    b_ref = next(it) if has_bias else None
    if with_f:
        wf_ref, bf_ref = next(it), next(it)
    o_ref = next(it)
    if with_f:
        lf_ref = next(it)
    h_ref = next(it)
    j = pl.program_id(1)

    @pl.when(j == 0)
    def _():
        h = _rms(x_ref[...], g_ref[...]).astype(BF16)
        h_ref[...] = h
        if with_f:
            f = lax.dot_general(wf_ref[...], h, (((1,), (1,)), ((), ())), preferred_element_type=F32) + bf_ref[...]
            lf_ref[...] = jnp.minimum(f, 0.0) - jnp.log1p(jnp.exp(-jnp.abs(f)))

    acc = jnp.dot(h_ref[...], w_ref[...], preferred_element_type=F32)
    if has_bias:
        acc = acc + b_ref[...]
    if sig_from is None:
        o_ref[...] = acc.astype(o_ref.dtype)
    else:
        @pl.when(j < sig_from)
        def _():
            o_ref[...] = acc.astype(o_ref.dtype)

        @pl.when(j >= sig_from)
        def _():
            o_ref[...] = jax.nn.sigmoid(acc).astype(o_ref.dtype)


def _norm_matmul(x, g, w, b=None, *, out_dtype, sig_from_col=None, wf_t=None, bf_t=None, tm=512, tn=1024):
    M, K = x.shape
    N = w.shape[1]
    tm, tn = _tile(M, tm), _tile(N, tn)
    with_f = wf_t is not None
    sig_from = None if sig_from_col is None else sig_from_col // tn
    if sig_from_col is not None:
        assert sig_from_col % tn == 0
    in_specs = [pl.BlockSpec((tm, K), lambda i, j: (i, 0)),
                pl.BlockSpec((1, K), lambda i, j: (0, 0)),
                pl.BlockSpec((K, tn), lambda i, j: (0, j))]
    args = [x, g.reshape(1, K), w]
    if b is not None:
        in_specs.append(pl.BlockSpec((1, tn), lambda i, j: (0, j)))
        args.append(b.reshape(1, N))
    out_shape = [jax.ShapeDtypeStruct((M, N), out_dtype)]
    out_specs = [pl.BlockSpec((tm, tn), lambda i, j: (i, j))]
    if with_f:
        H = wf_t.shape[0]
        in_specs += [pl.BlockSpec((H, K), lambda i, j: (0, 0)), pl.BlockSpec((H, 1), lambda i, j: (0, 0))]
        args += [wf_t, bf_t.reshape(H, 1)]
        out_shape.append(jax.ShapeDtypeStruct((H, M), F32))
        out_specs.append(pl.BlockSpec((H, tm), lambda i, j: (0, i)))
    res = pl.pallas_call(
        functools.partial(_norm_mm_kernel, has_bias=b is not None, sig_from=sig_from, with_f=with_f),
        grid=(M // tm, N // tn),
        in_specs=in_specs, out_specs=out_specs, out_shape=out_shape,
        scratch_shapes=[pltpu.VMEM((tm, K), BF16)],
        compiler_params=_params("parallel", "arbitrary"),
    )(*args)
    return res if with_f else res[0]


def _mm_kernel(x_ref, w_ref, *refs, mode):
    acc = jnp.dot(x_ref[...], w_ref[...], preferred_element_type=F32)
    if mode == "merge":
        pp_ref, g1_ref, o_ref = refs
        o_ref[...] = (pp_ref[...] + g1_ref[...].astype(F32) * acc).astype(o_ref.dtype)
    else:
        r_ref, o_ref = refs
        o_ref[...] = (r_ref[...] + acc).astype(o_ref.dtype)


def _matmul(x, w, extras, extra_col_off, *, mode, out_dtype, tm=512, tn=1024):
    M, K = x.shape
    N = w.shape[1]
    tm, tn = _tile(M, tm), _tile(N, tn)
    while any(off % tn for off in extra_col_off):
        tn //= 2
    assert tn % LANES == 0
    in_specs = [pl.BlockSpec((tm, K), lambda i, j: (i, 0)), pl.BlockSpec((K, tn), lambda i, j: (0, j))]
    for off in extra_col_off:
        in_specs.append(pl.BlockSpec((tm, tn), lambda i, j, o=off // tn: (i, j + o)))
    return pl.pallas_call(
        functools.partial(_mm_kernel, mode=mode),
        grid=(M // tm, N // tn),
        in_specs=in_specs,
        out_specs=pl.BlockSpec((tm, tn), lambda i, j: (i, j)),
        out_shape=jax.ShapeDtypeStruct((M, N), out_dtype),
        compiler_params=_params("parallel", "parallel"),
    )(x, w, *extras)


def _cumsum_kernel(lf_ref, c_ref):
    S = lf_ref.shape[1]
    row = lax.broadcasted_iota(jnp.int32, (S, S), 0)
    col = lax.broadcasted_iota(jnp.int32, (S, S), 1)
    upper = (row <= col).astype(BF16)
    c = jnp.zeros(lf_ref.shape, F32)
    for part in _split3(lf_ref[...]):
        c = c + jnp.dot(part, upper, preferred_element_type=F32)
    c_ref[...] = c


def _forget_cumsum(lf_t, B, S):
    H = lf_t.shape[0]
    return pl.pallas_call(
        _cumsum_kernel,
        grid=(B,),
        in_specs=[pl.BlockSpec((H, S), lambda b: (0, b))],
        out_specs=pl.BlockSpec((None, H, S), lambda b: (b, 0, 0)),
        out_shape=jax.ShapeDtypeStruct((B, H, S), F32),
        compiler_params=_params("parallel"),
    )(lf_t)


def _pool_kernel(u_ref, halo_ref, w_ref, sc_ref, g0_ref, o_ref, ext_ref, pooled_ref, *, seq_len):
    g = pl.program_id(0)
    i = pl.program_id(1)
    tp = u_ref.shape[0]
    pos0 = (i * tp) % seq_len
    u = u_ref[...]
    ext_ref[pl.ds(POOL_HALO, tp), :] = u
    ext_ref[pl.ds(0, POOL_HALO), :] = jnp.where(pos0 == 0, 0.0, halo_ref[...])
    pos = pos0 + lax.broadcasted_iota(jnp.int32, (tp, 1), 0)
    for gi, win in enumerate(POOL_WINDOWS):
        @pl.when(g == gi)
        def _(win=win):
            acc = u
            for k in range(1, win):
                acc = acc + ext_ref[pl.ds(POOL_HALO - k, tp), :]
            cnt = jnp.minimum(pos + 1, win).astype(F32)
            pooled_ref[...] = (acc / cnt - u).astype(BF16)
    y = jnp.dot(pooled_ref[...], w_ref[...], preferred_element_type=F32)
    o_ref[...] = g0_ref[...].astype(F32) * (y * sc_ref[...])


def _pool_mixer(u, w_pool, scale, gates, gate0_col, seq_len, *, tp=512):
    T = u.shape[0]
    G, C, Do = w_pool.shape
    tp = _tile(seq_len, tp)
    assert tp % POOL_HALO == 0 and gate0_col % Do == 0
    hb = tp // POOL_HALO
    return pl.pallas_call(
        functools.partial(_pool_kernel, seq_len=seq_len),
        grid=(G, T // tp),
        in_specs=[pl.BlockSpec((tp, C), lambda g, i: (i, g)),
                  pl.BlockSpec((POOL_HALO, C), lambda g, i: (jnp.maximum(i * hb - 1, 0), g)),
                  pl.BlockSpec((None, C, Do), lambda g, i: (g, 0, 0)),
                  pl.BlockSpec((1, Do), lambda g, i: (0, g)),
                  pl.BlockSpec((tp, Do), lambda g, i, o=gate0_col // Do: (i, g + o))],
        out_specs=pl.BlockSpec((tp, Do), lambda g, i: (i, g)),
        out_shape=jax.ShapeDtypeStruct((T, G * Do), F32),
        scratch_shapes=[pltpu.VMEM((tp + POOL_HALO, C), F32), pltpu.VMEM((tp, C), BF16)],
        compiler_params=_params("parallel", "parallel"),
    )(u, u, w_pool, scale.reshape(1, G * Do), gates)


def _fox_kernel(q_ref, k_ref, v_ref, cq_ref, ck_ref, o_ref, *, scale):
    h = pl.program_id(1)
    qi = pl.program_id(2)
    tq, dh = q_ref.shape
    q = q_ref[...]
    lane = lax.broadcasted_iota(jnp.int32, cq_ref.shape, 1)
    cq = jnp.sum(jnp.where(lane == h, cq_ref[...], 0.0), axis=1, keepdims=True)

    def scores(j):
        k = k_ref[pl.ds(pl.multiple_of(j * tq, tq), tq), :]
        s = lax.dot_general(q, k, (((1,), (1,)), ((), ())), preferred_element_type=F32) * scale
        ck = ck_ref[:, pl.ds(pl.multiple_of(j * tq, tq), tq)]
        return s + (cq - ck)

    def update(j, s, carry):
        m, l, acc = carry
        m_new = jnp.maximum(m, jnp.max(s, axis=1, keepdims=True))
        alpha = jnp.exp(m - m_new)
        p = jnp.exp(s - m_new)
        v = v_ref[pl.ds(pl.multiple_of(j * tq, tq), tq), :]
        acc = alpha * acc + jnp.dot(p.astype(BF16), v, preferred_element_type=F32)
        return m_new, alpha * l + jnp.sum(p, axis=1, keepdims=True), acc

    init = (jnp.full((tq, 1), NEG_BIG, F32), jnp.zeros((tq, 1), F32), jnp.zeros((tq, dh), F32))
    carry = lax.fori_loop(0, qi, lambda j, c: update(j, scores(j), c), init)
    row = lax.broadcasted_iota(jnp.int32, (tq, tq), 0)
    col = lax.broadcasted_iota(jnp.int32, (tq, tq), 1)
    s = jnp.where(col <= row, scores(qi), NEG_BIG)
    m, l, acc = update(qi, s, carry)
    o_ref[...] = (acc / l).astype(o_ref.dtype)


def _fox_attention(qkvg, c, c_t, B, S, H, dh, *, tq=512):
    T = B * S
    tq = _tile(S, tq)
    nq = S // tq
    return pl.pallas_call(
        functools.partial(_fox_kernel, scale=dh ** -0.5),
        grid=(B, H, nq),
        in_specs=[pl.BlockSpec((tq, dh), lambda b, h, i: (b * nq + i, h)),
                  pl.BlockSpec((S, dh), lambda b, h, i: (b, H + h)),
                  pl.BlockSpec((S, dh), lambda b, h, i: (b, 2 * H + h)),
                  pl.BlockSpec((None, tq, H), lambda b, h, i: (b, i, 0)),
                  pl.BlockSpec((None, None, 1, S), lambda b, h, i: (b, h, 0, 0))],
        out_specs=pl.BlockSpec((tq, dh), lambda b, h, i: (b * nq + i, h)),
        out_shape=jax.ShapeDtypeStruct((T, H * dh), BF16),
        compiler_params=_params("parallel", "parallel", "parallel"),
    )(qkvg, qkvg, qkvg, c, c_t)


def _mem_attn_kernel(q_ref, kv_ref, o_ref, *, heads):
    width = q_ref.shape[1]
    dh = width // heads
    scale = dh ** -0.5
    for hd in range(heads):
        q = q_ref[:, hd * dh:(hd + 1) * dh]
        k = kv_ref[:, hd * dh:(hd + 1) * dh]
        v = kv_ref[:, width + hd * dh:width + (hd + 1) * dh]
        s = lax.dot_general(q, k, (((1,), (1,)), ((), ())), preferred_element_type=F32) * scale
        p = jnp.exp(s - jnp.max(s, axis=1, keepdims=True))
        p = p / jnp.sum(p, axis=1, keepdims=True)
        o_ref[:, hd * dh:(hd + 1) * dh] = jnp.dot(p.astype(BF16), v, preferred_element_type=F32).astype(o_ref.dtype)


def _mem_attention(q, kv, B, S, n_mem, *, tq=512):
    T, width = q.shape
    tq = _tile(S, tq)
    nq = S // tq
    return pl.pallas_call(
        functools.partial(_mem_attn_kernel, heads=MEM_HEADS),
        grid=(B, nq),
        in_specs=[pl.BlockSpec((tq, width), lambda b, i: (b * nq + i, 0)),
                  pl.BlockSpec((n_mem, 2 * width), lambda b, i: (b, 0))],
        out_specs=pl.BlockSpec((tq, width), lambda b, i: (b * nq + i, 0)),
        out_shape=jax.ShapeDtypeStruct((T, width), BF16),
        compiler_params=_params("parallel", "parallel"),
    )(q, kv)


def _router_kernel(x_ref, g_ref, whi_ref, wlo_ref, b_ref, oi_ref, og_ref, cnt_ref, carry_ref, *, n_experts):
    i = pl.program_id(0)
    tm = x_ref.shape[0]

    @pl.when(i == 0)
    def _():
        carry_ref[...] = jnp.zeros_like(carry_ref)

    h = _rms(x_ref[...], g_ref[...])
    h_hi = h.astype(BF16)
    h_lo = (h - h_hi.astype(F32)).astype(BF16)
    logits = (jnp.dot(h_hi, whi_ref[...], preferred_element_type=F32)
              + jnp.dot(h_hi, wlo_ref[...], preferred_element_type=F32)
              + jnp.dot(h_lo, whi_ref[...], preferred_element_type=F32)) + b_ref[...]
    lane = lax.broadcasted_iota(jnp.int32, (tm, LANES), 1).astype(F32)
    work = jnp.where(lane < n_experts, logits, -jnp.inf)
    vals, idxs = [], []
    for _ in range(TOP_K):
        m = jnp.max(work, axis=1, keepdims=True)
        idx = jnp.min(jnp.where(work == m, lane, float(LANES)), axis=1, keepdims=True)
        vals.append(m)
        idxs.append(idx)
        work = jnp.where(lane == idx, -jnp.inf, work)
    exps = [jnp.exp(v - vals[0]) for v in vals]
    denom = exps[0] + exps[1] + exps[2] + exps[3]
    onehots = [(lane == idx).astype(F32) for idx in idxs]
    chosen = onehots[0] + onehots[1] + onehots[2] + onehots[3]
    row = lax.broadcasted_iota(jnp.int32, (tm, tm), 0)
    col = lax.broadcasted_iota(jnp.int32, (tm, tm), 1)
    before = jnp.dot((col < row).astype(BF16), chosen.astype(BF16), preferred_element_type=F32) + carry_ref[...]
    out_i = jnp.zeros((tm, LANES), F32)
    out_g = jnp.zeros((tm, LANES), F32)
    for k in range(TOP_K):
        rank = jnp.sum(onehots[k] * before, axis=1, keepdims=True)
        out_i = jnp.where(lane == k, idxs[k], out_i)
        out_i = jnp.where(lane == TOP_K + k, rank, out_i)
        out_g = jnp.where(lane == k, exps[k] / denom, out_g)
    oi_ref[...] = out_i.astype(jnp.int32)
    og_ref[...] = out_g
    carry_ref[...] = carry_ref[...] + jnp.sum(chosen, axis=0, keepdims=True)
    cnt_ref[...] = carry_ref[...]


def _router(x, g, w_router, b_router, *, tm=512):
    T, D = x.shape
    E = w_router.shape[1]
    tm = _tile(T, tm)
    w_pad = jnp.zeros((D, LANES), F32).at[:, :E].set(w_router)
    w_hi = w_pad.astype(BF16)
    w_lo = (w_pad - w_hi.astype(F32)).astype(BF16)
    b_pad = jnp.zeros((1, LANES), F32).at[0, :E].set(b_router)
    full = lambda i: (0, 0)
    return pl.pallas_call(
        functools.partial(_router_kernel, n_experts=E),
        grid=(T // tm,),
        in_specs=[pl.BlockSpec((tm, D), lambda i: (i, 0)), pl.BlockSpec((1, D), full),
                  pl.BlockSpec((D, LANES), full), pl.BlockSpec((D, LANES), full), pl.BlockSpec((1, LANES), full)],
        out_specs=[pl.BlockSpec((tm, LANES), lambda i: (i, 0)), pl.BlockSpec((tm, LANES), lambda i: (i, 0)),
                   pl.BlockSpec((1, LANES), full)],
        out_shape=[jax.ShapeDtypeStruct((T, LANES), jnp.int32), jax.ShapeDtypeStruct((T, LANES), F32),
                   jax.ShapeDtypeStruct((1, LANES), F32)],
        scratch_shapes=[pltpu.VMEM((1, LANES), F32)],
        compiler_params=_params("arbitrary"),
    )(x, g.reshape(1, D), w_hi, w_lo, b_pad)


def _row_copy(src_hbm, row, buf, slot, r, sem):
    return pltpu.make_async_copy(src_hbm.at[pl.ds(row, 1)], buf.at[slot, pl.ds(r, 1)], sem.at[slot])


def _start_rows(idx_ref, src_hbm, buf, slot, sem, n_rows):
    def body(r, _):
        _row_copy(src_hbm, idx_ref[0, 0, r], buf, slot, r, sem).start()
        return 0
    lax.fori_loop(0, n_rows, body, 0, unroll=8)


def _wait_rows(src_hbm, buf, slot, sem, n_rows):
    pltpu.make_async_copy(src_hbm.at[pl.ds(0, n_rows)], buf.at[slot], sem.at[slot]).wait()


def _gather_pipeline(idx_ref, nxt_ref, src_hbm, buf, sem, n_rows):
    i = pl.program_id(0)
    n = pl.num_programs(0)
    slot = i % 2

    @pl.when(i == 0)
    def _():
        _start_rows(idx_ref, src_hbm, buf, 0, sem, n_rows)

    @pl.when(i + 1 < n)
    def _():
        _start_rows(nxt_ref, src_hbm, buf, 1 - slot, sem, n_rows)

    _wait_rows(src_hbm, buf, slot, sem, n_rows)
    return slot


def _pack_bf16_pairs(h):
    half = h.shape[1] // 2
    bits = pltpu.bitcast(h.astype(BF16).astype(F32), jnp.uint32)
    return (bits[:, :half] >> 16) | bits[:, half:]


def _unpack_bf16_pairs(u):
    lo = pltpu.bitcast(u << 16, F32).astype(BF16)
    hi = pltpu.bitcast(u & jnp.uint32(0xFFFF0000), F32).astype(BF16)
    return lo, hi


def _scatter_norm_kernel(pe_ref, pd_ref, dest_ref, x_ref, g_ref, xs_hbm, buf, zbuf, sem, zsem, *, tm, n_experts):
    i = pl.program_id(0)
    n = pl.num_programs(0)
    tb = x_ref.shape[0]
    slot = i % 2

    def row_copy(s, t, row):
        return pltpu.make_async_copy(buf.at[s, pl.ds(t, 1)], xs_hbm.at[pl.ds(row, 1)], sem.at[s])

    def wait_slot(s):
        for _ in range(TOP_K):
            pltpu.make_async_copy(buf.at[s], xs_hbm.at[pl.ds(0, tb)], sem.at[s]).wait()

    def zero_group(first_row):
        return pltpu.make_async_copy(zbuf, xs_hbm.at[pl.ds(pl.multiple_of(first_row, tm), tm)], zsem)

    @pl.when(i == 0)
    def _():
        zbuf[...] = jnp.zeros_like(zbuf)
        n_groups = xs_hbm.shape[0] // tm
        used = pe_ref[n_experts - 1] // tm
        for e in range(n_experts):
            @pl.when(pd_ref[e] > 0)
            def _(e=e):
                zero_group(pe_ref[e] - tm).start()
        lax.fori_loop(used, n_groups, lambda gi, c: (zero_group(gi * tm).start(), c)[1], 0)
        for e in range(n_experts):
            @pl.when(pd_ref[e] > 0)
            def _(e=e):
                zero_group(pe_ref[e] - tm).wait()
        lax.fori_loop(used, n_groups, lambda gi, c: (zero_group(gi * tm).wait(), c)[1], 0)

    @pl.when(i >= 2)
    def _():
        wait_slot(slot)

    buf[slot] = _pack_bf16_pairs(_rms(x_ref[...], g_ref[...]))

    def body(t, _):
        for k in range(TOP_K):
            row_copy(slot, t, dest_ref[0, 0, t * TOP_K + k]).start()
        return 0
    lax.fori_loop(0, tb, body, 0, unroll=4)

    @pl.when(i == n - 1)
    def _():
        @pl.when(n >= 2)
        def _():
            wait_slot(1 - slot)
        wait_slot(slot)


def _scatter_norm(x, g, dest, pad_ends, padded, n_rows, *, tm, tb=128):
    T, D = x.shape
    E = pad_ends.shape[0]
    tb = _tile(T, tb)
    nb = T // tb
    grid_spec = pltpu.PrefetchScalarGridSpec(
        num_scalar_prefetch=2, grid=(nb,),
        in_specs=[pl.BlockSpec((1, 1, TOP_K * tb), lambda i, pe, pd: (i, 0, 0), memory_space=pltpu.SMEM),
                  pl.BlockSpec((tb, D), lambda i, pe, pd: (i, 0)),
                  pl.BlockSpec((1, D), lambda i, pe, pd: (0, 0))],
        out_specs=pl.BlockSpec(memory_space=pl.ANY),
        scratch_shapes=[pltpu.VMEM((2, tb, D // 2), jnp.uint32), pltpu.VMEM((tm, D // 2), jnp.uint32),
                        pltpu.SemaphoreType.DMA((2,)), pltpu.SemaphoreType.DMA(())])
    return pl.pallas_call(
        functools.partial(_scatter_norm_kernel, tm=tm, n_experts=E), grid_spec=grid_spec,
        out_shape=jax.ShapeDtypeStruct((n_rows, D // 2), jnp.uint32),
        compiler_params=_params("arbitrary"),
    )(pad_ends, padded, dest.reshape(nb, 1, TOP_K * tb), x, g.reshape(1, D))


def _combine_kernel(idx_ref, nxt_ref, x_ref, gate_ref, g_ref, rows_hbm, o_ref, buf, sem, *, final_norm):
    tb = x_ref.shape[0]
    slot = _gather_pipeline(idx_ref, nxt_ref, rows_hbm, buf, sem, TOP_K * tb)
    y = x_ref[...]
    gates = gate_ref[...]
    for k in range(TOP_K):
        y = y + gates[:, k:k + 1] * buf[slot, pl.ds(k * tb, tb), :]
    o_ref[...] = _rms(y, g_ref[...]) if final_norm else y


def _combine(x, rows, dest, gates_slab, g, final_norm, *, tb=64):
    T, D = x.shape
    tb = _tile(T, tb)
    nb = T // tb
    idx3 = dest.reshape(nb, tb, TOP_K).transpose(0, 2, 1).reshape(nb, 1, TOP_K * tb)
    return pl.pallas_call(
        functools.partial(_combine_kernel, final_norm=final_norm),
        grid=(nb,),
        in_specs=[pl.BlockSpec((1, 1, TOP_K * tb), lambda i: (i, 0, 0), memory_space=pltpu.SMEM),
                  pl.BlockSpec((1, 1, TOP_K * tb), lambda i: (jnp.minimum(i + 1, nb - 1), 0, 0), memory_space=pltpu.SMEM),
                  pl.BlockSpec((tb, D), lambda i: (i, 0)),
                  pl.BlockSpec((tb, LANES), lambda i: (i, 0)),
                  pl.BlockSpec((1, D), lambda i: (0, 0)),
                  pl.BlockSpec(memory_space=pl.ANY)],
        out_specs=pl.BlockSpec((tb, D), lambda i: (i, 0)),
        out_shape=jax.ShapeDtypeStruct((T, D), F32),
        scratch_shapes=[pltpu.VMEM((2, TOP_K * tb, D), F32), pltpu.SemaphoreType.DMA((2,))],
        compiler_params=_params("arbitrary"),
    )(idx3, idx3, x, gates_slab, g.reshape(1, D), rows)


def _moe_up_kernel(be_ref, nu_ref, xs_ref, w_ref, b_ref, o_ref):
    i = pl.program_id(1)
    half = xs_ref.shape[1]
    pair = 2 * LANES

    @pl.when(i < nu_ref[0])
    def _():
        lo, hi = _unpack_bf16_pairs(xs_ref[...])
        gu = (jnp.dot(lo, w_ref[:half, :].astype(BF16), preferred_element_type=F32)
              + jnp.dot(hi, w_ref[half:, :].astype(BF16), preferred_element_type=F32)) + b_ref[...]
        row = lax.broadcasted_iota(jnp.int32, (pair, LANES), 0)
        col = lax.broadcasted_iota(jnp.int32, (pair, LANES), 1)
        pick_even = (row == 2 * col).astype(BF16)
        even = lax.broadcasted_iota(jnp.int32, (1, pair), 1) % 2 == 0
        for c in range(gu.shape[1] // pair):
            blk = gu[:, c * pair:(c + 1) * pair]
            nxt = pltpu.roll(blk, pair - 1, axis=1)
            gate = jnp.minimum(blk, SWIGLU_LIMIT)
            up = jnp.clip(nxt, -SWIGLU_LIMIT, SWIGLU_LIMIT)
            act = (up + 1.0) * gate * jax.nn.sigmoid(SWIGLU_ALPHA * gate)
            act = jnp.where(even, act, 0.0).astype(BF16)
            o_ref[:, c * LANES:(c + 1) * LANES] = jnp.dot(act, pick_even, preferred_element_type=F32).astype(o_ref.dtype)

    @pl.when(i >= nu_ref[0])
    def _():
        o_ref[...] = jnp.zeros_like(o_ref)


def _moe_up(xs, w_gu, b_gu, block_e, n_used, *, tm, tn=1024):
    R, half = xs.shape
    E, D, F2 = w_gu.shape
    tn = _tile(F2, tn)
    assert tn % (2 * LANES) == 0 and D == 2 * half
    nb = R // tm
    grid_spec = pltpu.PrefetchScalarGridSpec(
        num_scalar_prefetch=2, grid=(F2 // tn, nb),
        in_specs=[pl.BlockSpec((tm, half), lambda j, i, be, nu: (jnp.minimum(i, nu[0] - 1), 0)),
                  pl.BlockSpec((None, D, tn), lambda j, i, be, nu: (be[i], 0, j)),
                  pl.BlockSpec((None, 1, tn), lambda j, i, be, nu: (be[i], 0, j))],
        out_specs=pl.BlockSpec((tm, tn // 2), lambda j, i, be, nu: (i, j)))
    return pl.pallas_call(
        _moe_up_kernel, grid_spec=grid_spec,
        out_shape=jax.ShapeDtypeStruct((R, F2 // 2), BF16),
        compiler_params=_params("parallel", "arbitrary"),
    )(block_e, n_used, xs, w_gu, b_gu.reshape(E, 1, F2))


def _moe_down_kernel(be_ref, nu_ref, a_ref, w_ref, b_ref, o_ref):
    i = pl.program_id(1)

    @pl.when(i < nu_ref[0])
    def _():
        o_ref[...] = jnp.dot(a_ref[...], w_ref[...].astype(BF16), preferred_element_type=F32) + b_ref[...]

    @pl.when(i >= nu_ref[0])
    def _():
        o_ref[...] = jnp.zeros_like(o_ref)


def _moe_down(act, w_d, b_d, block_e, n_used, *, tm, tn=2048):
    R, F = act.shape
    E, _, D = w_d.shape
    tn = _tile(D, tn)
    nb = R // tm
    grid_spec = pltpu.PrefetchScalarGridSpec(
        num_scalar_prefetch=2, grid=(D // tn, nb),
        in_specs=[pl.BlockSpec((tm, F), lambda j, i, be, nu: (i, 0)),
                  pl.BlockSpec((None, F, tn), lambda j, i, be, nu: (be[i], 0, j)),
                  pl.BlockSpec((None, 1, tn), lambda j, i, be, nu: (be[i], 0, j))],
        out_specs=pl.BlockSpec((tm, tn), lambda j, i, be, nu: (i, j)))
    return pl.pallas_call(
        _moe_down_kernel, grid_spec=grid_spec,
        out_shape=jax.ShapeDtypeStruct((R, D), F32),
        compiler_params=_params("parallel", "arbitrary"),
    )(block_e, n_used, act, w_d, b_d.reshape(E, 1, D))


def _moe(x, g_moe, w_router, b_router, w_gate_up, b_gate_up, w_down, b_down, g_final, final_norm, *, tm=256):
    T, D = x.shape
    E = w_router.shape[1]
    A = T * TOP_K
    tm = min(tm, A)
    slab_i, slab_g, cnt = _router(x, g_moe, w_router, b_router)
    idx = slab_i[:, :TOP_K]
    rank = slab_i[:, TOP_K:2 * TOP_K]
    counts = cnt[0, :E].astype(jnp.int32)
    padded = (counts + tm - 1) // tm * tm
    pad_ends = jnp.cumsum(padded)
    pad_starts = pad_ends - padded
    dest = pad_starts[idx] + rank
    R = (A + tm - 1) // tm * tm + E * tm
    nb = R // tm
    block_start = jnp.arange(nb, dtype=jnp.int32) * tm
    block_e = jnp.minimum(jnp.searchsorted(pad_ends, block_start, side="right"), E - 1).astype(jnp.int32)
    n_used = (pad_ends[-1:] // tm).astype(jnp.int32)

    xs = _scatter_norm(x, g_moe, dest, pad_ends, padded, R, tm=tm)
    act = _moe_up(xs, w_gate_up, b_gate_up, block_e, n_used, tm=tm)
    rows = _moe_down(act, w_down, b_down, block_e, n_used, tm=tm)
    return _combine(x, rows, dest, slab_g, g_final, final_norm)


def kernel(x, mem, g_mix, w_in, b_in, w_pool, pool_scale, w_fox_o, w_out, g_mem_q, g_mem_kv, w_mem_q, w_mem_kv,
           w_mem_o, g_moe, w_router, b_router, w_gate_up, b_gate_up, w_down, b_down, g_final):
    B, S, D = x.shape
    T = B * S
    n_mem = mem.shape[1]
    depth, G, C, Do = w_pool.shape
    pool_w = G * C
    fox_w = w_fox_o.shape[1]
    H = w_in.shape[2] - pool_w - 3 * fox_w - 2 * D
    dh = fox_w // H
    off_q, off_f = pool_w, pool_w + 3 * fox_w
    off_gate = off_f + H
    xt = x.reshape(T, D)
    mt = mem.reshape(B * n_mem, D)
    for l in range(depth):
        wl, bl = w_in[l], b_in[l]
        u, lf_t = _norm_matmul(xt, g_mix[l], wl[:, :off_q].astype(BF16), bl[:off_q], out_dtype=F32,
                               wf_t=wl[:, off_f:off_gate].T.astype(BF16), bf_t=bl[off_f:off_gate])
        w_qkvg = jnp.concatenate([wl[:, off_q:off_f], wl[:, off_gate:]], axis=1).astype(BF16)
        b_qkvg = jnp.concatenate([bl[off_q:off_f], bl[off_gate:]])
        qkvg = _norm_matmul(xt, g_mix[l], w_qkvg, b_qkvg, out_dtype=BF16, sig_from_col=3 * fox_w)
        c_t = _forget_cumsum(lf_t, B, S)
        att = _fox_attention(qkvg, c_t.transpose(0, 2, 1), c_t.reshape(B, H, 1, S), B, S, H, dh)
        pp = _pool_mixer(u, w_pool[l].astype(BF16), pool_scale[l], qkvg, 3 * fox_w, S)
        merged = _matmul(att, w_fox_o[l].astype(BF16), (pp, qkvg), (0, 3 * fox_w + D), mode="merge", out_dtype=BF16)
        xt = _matmul(merged, w_out[l].astype(BF16), (xt,), (0,), mode="residual", out_dtype=F32)
        kv = _norm_matmul(mt, g_mem_kv[l], w_mem_kv[l].astype(BF16), out_dtype=BF16)
        qm = _norm_matmul(xt, g_mem_q[l], w_mem_q[l].astype(BF16), out_dtype=BF16)
        om = _mem_attention(qm, kv, B, S, n_mem)
        xt = _matmul(om, w_mem_o[l].astype(BF16), (xt,), (0,), mode="residual", out_dtype=F32)
        xt = _moe(xt, g_moe[l], w_router[l], b_router[l], w_gate_up[l], b_gate_up[l], w_down[l], b_down[l],
                  g_final, final_norm=l == depth - 1)
    return xt.reshape(B, S, D)
```

```python
import functools

import jax
import jax.numpy as jnp
from jax import lax
from jax.experimental import pallas as pl
from jax.experimental.pallas import tpu as pltpu

F32 = jnp.float32
BF16 = jnp.bfloat16

EPS = 1e-5
POOL_WINDOWS = (2, 4, 8, 16)
POOL_HALO = 16
MEM_HEADS = 4
TOP_K = 4
SWIGLU_LIMIT = 7.0
SWIGLU_ALPHA = 1.702
NEG_BIG = -1e30

LANES = 128
VMEM_LIMIT_BYTES = 56 * 1024 * 1024


def _tile(dim, pref):
    t = pref
    while t >= 8:
        if dim % t == 0:
            return t
        t //= 2
    return dim


def _params(*sem):
    return pltpu.CompilerParams(dimension_semantics=sem, vmem_limit_bytes=VMEM_LIMIT_BYTES)


def _rms(x, g):
    ms = jnp.mean(x * x, axis=-1, keepdims=True)
    return x * lax.rsqrt(ms + EPS) * g


def _split3(x):
    hi = x.astype(BF16)
    r1 = x - hi.astype(F32)
    mid = r1.astype(BF16)
    lo = (r1 - mid.astype(F32)).astype(BF16)
    return hi, mid, lo


def _norm_kernel(x_ref, g_ref, *refs, with_f):
    h = _rms(x_ref[...], g_ref[...]).astype(BF16)
    if with_f:
        wf_ref, bf_ref, h_ref, lf_ref = refs
        f = lax.dot_general(wf_ref[...], h, (((1,), (1,)), ((), ())), preferred_element_type=F32) + bf_ref[...]
        lf_ref[...] = jnp.minimum(f, 0.0) - jnp.log1p(jnp.exp(-jnp.abs(f)))
    else:
        h_ref, = refs
    h_ref[...] = h


def _norm(x, g, wf_t=None, bf_t=None, *, tm=512):
    M, K = x.shape
    tm = _tile(M, tm)
    with_f = wf_t is not None
    in_specs = [pl.BlockSpec((tm, K), lambda i: (i, 0)), pl.BlockSpec((1, K), lambda i: (0, 0))]
    args = [x, g.reshape(1, K)]
    out_shape = [jax.ShapeDtypeStruct((M, K), BF16)]
    out_specs = [pl.BlockSpec((tm, K), lambda i: (i, 0))]
    if with_f:
        H = wf_t.shape[0]
        in_specs += [pl.BlockSpec((H, K), lambda i: (0, 0)), pl.BlockSpec((H, 1), lambda i: (0, 0))]
        args += [wf_t, bf_t.reshape(H, 1)]
        out_shape.append(jax.ShapeDtypeStruct((H, M), F32))
        out_specs.append(pl.BlockSpec((H, tm), lambda i: (0, i)))
    res = pl.pallas_call(
        functools.partial(_norm_kernel, with_f=with_f),
        grid=(M // tm,),
        in_specs=in_specs, out_specs=out_specs, out_shape=out_shape,
        compiler_params=_params("parallel"),
    )(*args)
    return res if with_f else res[0]


def _proj_kernel(x_ref, w_ref, *refs, mode, has_bias):
    refs = list(refs)
    wb_ref = refs.pop()
    o_ref = refs.pop()
    b_ref = refs.pop(0) if has_bias else None

    @pl.when(pl.program_id(1) == 0)
    def _():
        wb_ref[...] = w_ref[...].astype(BF16)

    acc = jnp.dot(x_ref[...], wb_ref[...], preferred_element_type=F32)
    if has_bias:
        acc = acc + b_ref[...]
    if mode == "sigmoid":
        acc = jax.nn.sigmoid(acc)
    elif mode == "merge":
        pp_ref, g1_ref = refs
        acc = pp_ref[...] + g1_ref[...].astype(F32) * acc
    elif mode == "residual":
        acc = refs[0][...] + acc
    o_ref[...] = acc.astype(o_ref.dtype)


def _proj(x, w, b=None, extras=(), extra_col_off=(), *, col_off=0, n_cols=None, mode="plain", out_dtype,
          tm=1024, tn=512):
    M, K = x.shape
    N = w.shape[1] - col_off if n_cols is None else n_cols
    tm, tn = _tile(M, tm), _tile(N, tn)
    while any(off % tn for off in (col_off, *extra_col_off)):
        tn //= 2
    assert tn % LANES == 0
    in_specs = [pl.BlockSpec((tm, K), lambda j, i: (i, 0)),
                pl.BlockSpec((K, tn), lambda j, i, o=col_off // tn: (0, j + o))]
    args = [x, w]
    if b is not None:
        in_specs.append(pl.BlockSpec((1, tn), lambda j, i, o=col_off // tn: (0, j + o)))
        args.append(b.reshape(1, -1))
    for off in extra_col_off:
        in_specs.append(pl.BlockSpec((tm, tn), lambda j, i, o=off // tn: (i, j + o)))
    return pl.pallas_call(
        functools.partial(_proj_kernel, mode=mode, has_bias=b is not None),
        grid=(N // tn, M // tm),
        in_specs=in_specs,
        out_specs=pl.BlockSpec((tm, tn), lambda j, i: (i, j)),
        out_shape=jax.ShapeDtypeStruct((M, N), out_dtype),
        scratch_shapes=[pltpu.VMEM((K, tn), BF16)],
        compiler_params=_params("parallel", "arbitrary"),
    )(*args, *extras)


def _cumsum_kernel(lf_ref, c_ref):
    S = lf_ref.shape[1]
    row = lax.broadcasted_iota(jnp.int32, (S, S), 0)
    col = lax.broadcasted_iota(jnp.int32, (S, S), 1)
    upper = (row <= col).astype(BF16)
    c = jnp.zeros(lf_ref.shape, F32)
    for part in _split3(lf_ref[...]):
        c = c + jnp.dot(part, upper, preferred_element_type=F32)
    c_ref[...] = c


def _forget_cumsum(lf_t, B, S):
    H = lf_t.shape[0]
    return pl.pallas_call(
        _cumsum_kernel,
        grid=(B,),
        in_specs=[pl.BlockSpec((H, S), lambda b: (0, b))],
        out_specs=pl.BlockSpec((None, H, S), lambda b: (b, 0, 0)),
        out_shape=jax.ShapeDtypeStruct((B, H, S), F32),
        compiler_params=_params("parallel"),
    )(lf_t)


def _pool_kernel(u_ref, halo_ref, w_ref, sc_ref, g0_ref, o_ref, ext_ref, pooled_ref, *, seq_len):
    g = pl.program_id(0)
    i = pl.program_id(1)
    tp = u_ref.shape[0]
    pos0 = (i * tp) % seq_len
    u = u_ref[...]
    ext_ref[pl.ds(POOL_HALO, tp), :] = u
    ext_ref[pl.ds(0, POOL_HALO), :] = jnp.where(pos0 == 0, 0.0, halo_ref[...])
    pos = pos0 + lax.broadcasted_iota(jnp.int32, (tp, 1), 0)
    for gi, win in enumerate(POOL_WINDOWS):
        @pl.when(g == gi)
        def _(win=win):
            acc = u
            for k in range(1, win):
                acc = acc + ext_ref[pl.ds(POOL_HALO - k, tp), :]
            cnt = jnp.minimum(pos + 1, win).astype(F32)
            pooled_ref[...] = (acc / cnt - u).astype(BF16)
    y = jnp.dot(pooled_ref[...], w_ref[...].astype(BF16), preferred_element_type=F32)
    o_ref[...] = g0_ref[...].astype(F32) * (y * sc_ref[...])


def _pool_mixer(u, w_pool, scale, gates, gate0_col, seq_len, *, tp=512):
    T = u.shape[0]
    G, C, Do = w_pool.shape
    tp = _tile(seq_len, tp)
    assert tp % POOL_HALO == 0 and gate0_col % Do == 0
    hb = tp // POOL_HALO
    return pl.pallas_call(
        functools.partial(_pool_kernel, seq_len=seq_len),
        grid=(G, T // tp),
        in_specs=[pl.BlockSpec((tp, C), lambda g, i: (i, g)),
                  pl.BlockSpec((POOL_HALO, C), lambda g, i: (jnp.maximum(i * hb - 1, 0), g)),
                  pl.BlockSpec((None, C, Do), lambda g, i: (g, 0, 0)),
                  pl.BlockSpec((1, Do), lambda g, i: (0, g)),
                  pl.BlockSpec((tp, Do), lambda g, i, o=gate0_col // Do: (i, g + o))],
        out_specs=pl.BlockSpec((tp, Do), lambda g, i: (i, g)),
        out_shape=jax.ShapeDtypeStruct((T, G * Do), F32),
        scratch_shapes=[pltpu.VMEM((tp + POOL_HALO, C), F32), pltpu.VMEM((tp, C), BF16)],
        compiler_params=_params("parallel", "parallel"),
    )(u, u, w_pool, scale.reshape(1, G * Do), gates)


def _fox_kernel(q_ref, k_ref, v_ref, cq_ref, ck_ref, o_ref, *, scale):
    h = pl.program_id(1)
    qi = pl.program_id(2)
    tq, dh = q_ref.shape
    q = q_ref[...]
    lane = lax.broadcasted_iota(jnp.int32, cq_ref.shape, 1)
    cq = jnp.sum(jnp.where(lane == h, cq_ref[...], 0.0), axis=1, keepdims=True)

    def scores(j):
        k = k_ref[pl.ds(pl.multiple_of(j * tq, tq), tq), :]
        s = lax.dot_general(q, k, (((1,), (1,)), ((), ())), preferred_element_type=F32) * scale
        ck = ck_ref[:, pl.ds(pl.multiple_of(j * tq, tq), tq)]
        return s + (cq - ck)

    def update(j, s, carry):
        m, l, acc = carry
        m_new = jnp.maximum(m, jnp.max(s, axis=1, keepdims=True))
        alpha = jnp.exp(m - m_new)
        p = jnp.exp(s - m_new)
        v = v_ref[pl.ds(pl.multiple_of(j * tq, tq), tq), :]
        acc = alpha * acc + jnp.dot(p.astype(BF16), v, preferred_element_type=F32)
        return m_new, alpha * l + jnp.sum(p, axis=1, keepdims=True), acc

    init = (jnp.full((tq, 1), NEG_BIG, F32), jnp.zeros((tq, 1), F32), jnp.zeros((tq, dh), F32))
    carry = lax.fori_loop(0, qi, lambda j, c: update(j, scores(j), c), init)
    row = lax.broadcasted_iota(jnp.int32, (tq, tq), 0)
    col = lax.broadcasted_iota(jnp.int32, (tq, tq), 1)
    s = jnp.where(col <= row, scores(qi), NEG_BIG)
    m, l, acc = update(qi, s, carry)
    o_ref[...] = (acc / l).astype(o_ref.dtype)


def _fox_attention(qkvg, c, c_t, B, S, H, dh, *, tq=512):
    T = B * S
    tq = _tile(S, tq)
    nq = S // tq
    return pl.pallas_call(
        functools.partial(_fox_kernel, scale=dh ** -0.5),
        grid=(B, H, nq),
        in_specs=[pl.BlockSpec((tq, dh), lambda b, h, i: (b * nq + i, h)),
                  pl.BlockSpec((S, dh), lambda b, h, i: (b, H + h)),
                  pl.BlockSpec((S, dh), lambda b, h, i: (b, 2 * H + h)),
                  pl.BlockSpec((None, tq, H), lambda b, h, i: (b, i, 0)),
                  pl.BlockSpec((None, None, 1, S), lambda b, h, i: (b, h, 0, 0))],
        out_specs=pl.BlockSpec((tq, dh), lambda b, h, i: (b * nq + i, h)),
        out_shape=jax.ShapeDtypeStruct((T, H * dh), BF16),
        compiler_params=_params("parallel", "parallel", "parallel"),
    )(qkvg, qkvg, qkvg, c, c_t)


def _mem_attn_kernel(q_ref, kv_ref, o_ref, *, heads):
    width = q_ref.shape[1]
    dh = width // heads
    scale = dh ** -0.5
    for hd in range(heads):
        q = q_ref[:, hd * dh:(hd + 1) * dh]
        k = kv_ref[:, hd * dh:(hd + 1) * dh]
        v = kv_ref[:, width + hd * dh:width + (hd + 1) * dh]
        s = lax.dot_general(q, k, (((1,), (1,)), ((), ())), preferred_element_type=F32) * scale
        p = jnp.exp(s - jnp.max(s, axis=1, keepdims=True))
        p = p / jnp.sum(p, axis=1, keepdims=True)
        o_ref[:, hd * dh:(hd + 1) * dh] = jnp.dot(p.astype(BF16), v, preferred_element_type=F32).astype(o_ref.dtype)


def _mem_attention(q, kv, B, S, n_mem, *, tq=512):
    T, width = q.shape
    tq = _tile(S, tq)
    nq = S // tq
    return pl.pallas_call(
        functools.partial(_mem_attn_kernel, heads=MEM_HEADS),
        grid=(B, nq),
        in_specs=[pl.BlockSpec((tq, width), lambda b, i: (b * nq + i, 0)),
                  pl.BlockSpec((n_mem, 2 * width), lambda b, i: (b, 0))],
        out_specs=pl.BlockSpec((tq, width), lambda b, i: (b * nq + i, 0)),
        out_shape=jax.ShapeDtypeStruct((T, width), BF16),
        compiler_params=_params("parallel", "parallel"),
    )(q, kv)


def _router_kernel(x_ref, g_ref, whi_ref, wlo_ref, b_ref, oi_ref, og_ref, cnt_ref, carry_ref, *, n_experts):
    i = pl.program_id(0)
    tm = x_ref.shape[0]

    @pl.when(i == 0)
    def _():
        carry_ref[...] = jnp.zeros_like(carry_ref)

    h = _rms(x_ref[...], g_ref[...])
    h_hi = h.astype(BF16)
    h_lo = (h - h_hi.astype(F32)).astype(BF16)
    logits = (jnp.dot(h_hi, whi_ref[...], preferred_element_type=F32)
              + jnp.dot(h_hi, wlo_ref[...], preferred_element_type=F32)
              + jnp.dot(h_lo, whi_ref[...], preferred_element_type=F32)) + b_ref[...]
    lane = lax.broadcasted_iota(jnp.int32, (tm, LANES), 1).astype(F32)
    work = jnp.where(lane < n_experts, logits, -jnp.inf)
    vals, idxs = [], []
    for _ in range(TOP_K):
        m = jnp.max(work, axis=1, keepdims=True)
        idx = jnp.min(jnp.where(work == m, lane, float(LANES)), axis=1, keepdims=True)
        vals.append(m)
        idxs.append(idx)
        work = jnp.where(lane == idx, -jnp.inf, work)
    exps = [jnp.exp(v - vals[0]) for v in vals]
    denom = exps[0] + exps[1] + exps[2] + exps[3]
    onehots = [(lane == idx).astype(F32) for idx in idxs]
    chosen = onehots[0] + onehots[1] + onehots[2] + onehots[3]
    row = lax.broadcasted_iota(jnp.int32, (tm, tm), 0)
    col = lax.broadcasted_iota(jnp.int32, (tm, tm), 1)
    before = jnp.dot((col < row).astype(BF16), chosen.astype(BF16), preferred_element_type=F32) + carry_ref[...]
    out_i = jnp.zeros((tm, LANES), F32)
    out_g = jnp.zeros((tm, LANES), F32)
    for k in range(TOP_K):
        rank = jnp.sum(onehots[k] * before, axis=1, keepdims=True)
        out_i = jnp.where(lane == k, idxs[k], out_i)
        out_i = jnp.where(lane == TOP_K + k, rank, out_i)
        out_g = jnp.where(lane == k, exps[k] / denom, out_g)
    oi_ref[...] = out_i.astype(jnp.int32)
    og_ref[...] = out_g
    carry_ref[...] = carry_ref[...] + jnp.sum(chosen, axis=0, keepdims=True)
    cnt_ref[...] = carry_ref[...]


def _router(x, g, w_router, b_router, *, tm=512):
    T, D = x.shape
    E = w_router.shape[1]
    tm = _tile(T, tm)
    w_pad = jnp.zeros((D, LANES), F32).at[:, :E].set(w_router)
    w_hi = w_pad.astype(BF16)
    w_lo = (w_pad - w_hi.astype(F32)).astype(BF16)
    b_pad = jnp.zeros((1, LANES), F32).at[0, :E].set(b_router)
    full = lambda i: (0, 0)
    return pl.pallas_call(
        functools.partial(_router_kernel, n_experts=E),
        grid=(T // tm,),
        in_specs=[pl.BlockSpec((tm, D), lambda i: (i, 0)), pl.BlockSpec((1, D), full),
                  pl.BlockSpec((D, LANES), full), pl.BlockSpec((D, LANES), full), pl.BlockSpec((1, LANES), full)],
        out_specs=[pl.BlockSpec((tm, LANES), lambda i: (i, 0)), pl.BlockSpec((tm, LANES), lambda i: (i, 0)),
                   pl.BlockSpec((1, LANES), full)],
        out_shape=[jax.ShapeDtypeStruct((T, LANES), jnp.int32), jax.ShapeDtypeStruct((T, LANES), F32),
                   jax.ShapeDtypeStruct((1, LANES), F32)],
        scratch_shapes=[pltpu.VMEM((1, LANES), F32)],
        compiler_params=_params("arbitrary"),
    )(x, g.reshape(1, D), w_hi, w_lo, b_pad)


def _row_copy(src_hbm, row, buf, slot, r, sem):
    return pltpu.make_async_copy(src_hbm.at[pl.ds(row, 1)], buf.at[slot, pl.ds(r, 1)], sem.at[slot])


def _start_rows(idx_ref, src_hbm, buf, slot, sem, n_rows):
    def body(r, _):
        _row_copy(src_hbm, idx_ref[0, 0, r], buf, slot, r, sem).start()
        return 0
    lax.fori_loop(0, n_rows, body, 0, unroll=8)


def _wait_rows(src_hbm, buf, slot, sem, n_rows):
    pltpu.make_async_copy(src_hbm.at[pl.ds(0, n_rows)], buf.at[slot], sem.at[slot]).wait()


def _gather_pipeline(idx_ref, nxt_ref, src_hbm, buf, sem, n_rows):
    i = pl.program_id(0)
    n = pl.num_programs(0)
    slot = i % 2

    @pl.when(i == 0)
    def _():
        _start_rows(idx_ref, src_hbm, buf, 0, sem, n_rows)

    @pl.when(i + 1 < n)
    def _():
        _start_rows(nxt_ref, src_hbm, buf, 1 - slot, sem, n_rows)

    _wait_rows(src_hbm, buf, slot, sem, n_rows)
    return slot


def _pack_bf16_pairs(h):
    half = h.shape[1] // 2
    bits = pltpu.bitcast(h.astype(BF16).astype(F32), jnp.uint32)
    return (bits[:, :half] >> 16) | bits[:, half:]


def _unpack_bf16_pairs(u):
    lo = pltpu.bitcast(u << 16, F32).astype(BF16)
    hi = pltpu.bitcast(u & jnp.uint32(0xFFFF0000), F32).astype(BF16)
    return lo, hi


def _scatter_norm_kernel(pe_ref, pd_ref, dest_ref, x_ref, g_ref, xs_hbm, buf, zbuf, sem, zsem, *, tm, n_experts):
    i = pl.program_id(0)
    n = pl.num_programs(0)
    tb = x_ref.shape[0]
    slot = i % 2

    def row_copy(s, t, row):
        return pltpu.make_async_copy(buf.at[s, pl.ds(t, 1)], xs_hbm.at[pl.ds(row, 1)], sem.at[s])

    def wait_slot(s):
        for _ in range(TOP_K):
            pltpu.make_async_copy(buf.at[s], xs_hbm.at[pl.ds(0, tb)], sem.at[s]).wait()

    def zero_group(first_row):
        return pltpu.make_async_copy(zbuf, xs_hbm.at[pl.ds(pl.multiple_of(first_row, tm), tm)], zsem)

    @pl.when(i == 0)
    def _():
        zbuf[...] = jnp.zeros_like(zbuf)
        n_groups = xs_hbm.shape[0] // tm
        used = pe_ref[n_experts - 1] // tm
        for e in range(n_experts):
            @pl.when(pd_ref[e] > 0)
            def _(e=e):
                zero_group(pe_ref[e] - tm).start()
        lax.fori_loop(used, n_groups, lambda gi, c: (zero_group(gi * tm).start(), c)[1], 0)
        for e in range(n_experts):
            @pl.when(pd_ref[e] > 0)
            def _(e=e):
                zero_group(pe_ref[e] - tm).wait()
        lax.fori_loop(used, n_groups, lambda gi, c: (zero_group(gi * tm).wait(), c)[1], 0)

    @pl.when(i >= 2)
    def _():
        wait_slot(slot)

    buf[slot] = _pack_bf16_pairs(_rms(x_ref[...], g_ref[...]))

    def body(t, _):
        for k in range(TOP_K):
            row_copy(slot, t, dest_ref[0, 0, t * TOP_K + k]).start()
        return 0
    lax.fori_loop(0, tb, body, 0, unroll=4)

    @pl.when(i == n - 1)
    def _():
        @pl.when(n >= 2)
        def _():
            wait_slot(1 - slot)
        wait_slot(slot)


def _scatter_norm(x, g, dest, pad_ends, padded, n_rows, *, tm, tb=128):
    T, D = x.shape
    E = pad_ends.shape[0]
    tb = _tile(T, tb)
    nb = T // tb
    grid_spec = pltpu.PrefetchScalarGridSpec(
        num_scalar_prefetch=2, grid=(nb,),
        in_specs=[pl.BlockSpec((1, 1, TOP_K * tb), lambda i, pe, pd: (i, 0, 0), memory_space=pltpu.SMEM),
                  pl.BlockSpec((tb, D), lambda i, pe, pd: (i, 0)),
                  pl.BlockSpec((1, D), lambda i, pe, pd: (0, 0))],
        out_specs=pl.BlockSpec(memory_space=pl.ANY),
        scratch_shapes=[pltpu.VMEM((2, tb, D // 2), jnp.uint32), pltpu.VMEM((tm, D // 2), jnp.uint32),
                        pltpu.SemaphoreType.DMA((2,)), pltpu.SemaphoreType.DMA(())])
    return pl.pallas_call(
        functools.partial(_scatter_norm_kernel, tm=tm, n_experts=E), grid_spec=grid_spec,
        out_shape=jax.ShapeDtypeStruct((n_rows, D // 2), jnp.uint32),
        compiler_params=_params("arbitrary"),
    )(pad_ends, padded, dest.reshape(nb, 1, TOP_K * tb), x, g.reshape(1, D))


def _combine_kernel(idx_ref, nxt_ref, x_ref, gate_ref, g_ref, rows_hbm, o_ref, buf, sem, *, final_norm):
    tb = x_ref.shape[0]
    slot = _gather_pipeline(idx_ref, nxt_ref, rows_hbm, buf, sem, TOP_K * tb)
    y = x_ref[...]
    gates = gate_ref[...]
    for k in range(TOP_K):
        y = y + gates[:, k:k + 1] * buf[slot, pl.ds(k * tb, tb), :]
    o_ref[...] = _rms(y, g_ref[...]) if final_norm else y


def _combine(x, rows, dest, gates_slab, g, final_norm, *, tb=64):
    T, D = x.shape
    tb = _tile(T, tb)
    nb = T // tb
    idx3 = dest.reshape(nb, tb, TOP_K).transpose(0, 2, 1).reshape(nb, 1, TOP_K * tb)
    return pl.pallas_call(
        functools.partial(_combine_kernel, final_norm=final_norm),
        grid=(nb,),
        in_specs=[pl.BlockSpec((1, 1, TOP_K * tb), lambda i: (i, 0, 0), memory_space=pltpu.SMEM),
                  pl.BlockSpec((1, 1, TOP_K * tb), lambda i: (jnp.minimum(i + 1, nb - 1), 0, 0), memory_space=pltpu.SMEM),
                  pl.BlockSpec((tb, D), lambda i: (i, 0)),
                  pl.BlockSpec((tb, LANES), lambda i: (i, 0)),
                  pl.BlockSpec((1, D), lambda i: (0, 0)),
                  pl.BlockSpec(memory_space=pl.ANY)],
        out_specs=pl.BlockSpec((tb, D), lambda i: (i, 0)),
        out_shape=jax.ShapeDtypeStruct((T, D), F32),
        scratch_shapes=[pltpu.VMEM((2, TOP_K * tb, D), F32), pltpu.SemaphoreType.DMA((2,))],
        compiler_params=_params("arbitrary"),
    )(idx3, idx3, x, gates_slab, g.reshape(1, D), rows)


def _moe_up_kernel(be_ref, nu_ref, xs_ref, w_ref, b_ref, o_ref):
    i = pl.program_id(1)
    half = xs_ref.shape[1]
    pair = 2 * LANES

    @pl.when(i < nu_ref[0])
    def _():
        lo, hi = _unpack_bf16_pairs(xs_ref[...])
        gu = (jnp.dot(lo, w_ref[:half, :].astype(BF16), preferred_element_type=F32)
              + jnp.dot(hi, w_ref[half:, :].astype(BF16), preferred_element_type=F32)) + b_ref[...]
        row = lax.broadcasted_iota(jnp.int32, (pair, LANES), 0)
        col = lax.broadcasted_iota(jnp.int32, (pair, LANES), 1)
        pick_even = (row == 2 * col).astype(BF16)
        even = lax.broadcasted_iota(jnp.int32, (1, pair), 1) % 2 == 0
        for c in range(gu.shape[1] // pair):
            blk = gu[:, c * pair:(c + 1) * pair]
            nxt = pltpu.roll(blk, pair - 1, axis=1)
            gate = jnp.minimum(blk, SWIGLU_LIMIT)
            up = jnp.clip(nxt, -SWIGLU_LIMIT, SWIGLU_LIMIT)
            act = (up + 1.0) * gate * jax.nn.sigmoid(SWIGLU_ALPHA * gate)
            act = jnp.where(even, act, 0.0).astype(BF16)
            o_ref[:, c * LANES:(c + 1) * LANES] = jnp.dot(act, pick_even, preferred_element_type=F32).astype(o_ref.dtype)

    @pl.when(i >= nu_ref[0])
    def _():
        o_ref[...] = jnp.zeros_like(o_ref)


def _moe_up(xs, w_gu, b_gu, block_e, n_used, *, tm, tn=512):
    R, half = xs.shape
    E, D, F2 = w_gu.shape
    tn = _tile(F2, tn)
    assert tn % (2 * LANES) == 0 and D == 2 * half
    nb = R // tm
    grid_spec = pltpu.PrefetchScalarGridSpec(
        num_scalar_prefetch=2, grid=(F2 // tn, nb),
        in_specs=[pl.BlockSpec((tm, half), lambda j, i, be, nu: (jnp.minimum(i, nu[0] - 1), 0)),
                  pl.BlockSpec((None, D, tn), lambda j, i, be, nu: (be[i], 0, j)),
                  pl.BlockSpec((None, 1, tn), lambda j, i, be, nu: (be[i], 0, j))],
        out_specs=pl.BlockSpec((tm, tn // 2), lambda j, i, be, nu: (i, j)))
    return pl.pallas_call(
        _moe_up_kernel, grid_spec=grid_spec,
        out_shape=jax.ShapeDtypeStruct((R, F2 // 2), BF16),
        compiler_params=_params("parallel", "arbitrary"),
    )(block_e, n_used, xs, w_gu, b_gu.reshape(E, 1, F2))


def _moe_down_kernel(be_ref, nu_ref, a_ref, w_ref, b_ref, o_ref):
    i = pl.program_id(1)

    @pl.when(i < nu_ref[0])
    def _():
        o_ref[...] = jnp.dot(a_ref[...], w_ref[...].astype(BF16), preferred_element_type=F32) + b_ref[...]

    @pl.when(i >= nu_ref[0])
    def _():
        o_ref[...] = jnp.zeros_like(o_ref)


def _moe_down(act, w_d, b_d, block_e, n_used, *, tm, tn=2048):
    R, F = act.shape
    E, _, D = w_d.shape
    tn = _tile(D, tn)
    nb = R // tm
    grid_spec = pltpu.PrefetchScalarGridSpec(
        num_scalar_prefetch=2, grid=(D // tn, nb),
        in_specs=[pl.BlockSpec((tm, F), lambda j, i, be, nu: (i, 0)),
                  pl.BlockSpec((None, F, tn), lambda j, i, be, nu: (be[i], 0, j)),
                  pl.BlockSpec((None, 1, tn), lambda j, i, be, nu: (be[i], 0, j))],
        out_specs=pl.BlockSpec((tm, tn), lambda j, i, be, nu: (i, j)))
    return pl.pallas_call(
        _moe_down_kernel, grid_spec=grid_spec,
        out_shape=jax.ShapeDtypeStruct((R, D), F32),
        compiler_params=_params("parallel", "arbitrary"),
    )(block_e, n_used, act, w_d, b_d.reshape(E, 1, D))


def _moe(x, g_moe, w_router, b_router, w_gate_up, b_gate_up, w_down, b_down, g_final, final_norm, *, tm=512):
    T, D = x.shape
    E = w_router.shape[1]
    A = T * TOP_K
    tm = min(tm, A)
    slab_i, slab_g, cnt = _router(x, g_moe, w_router, b_router)
    idx = slab_i[:, :TOP_K]
    rank = slab_i[:, TOP_K:2 * TOP_K]
    counts = cnt[0, :E].astype(jnp.int32)
    padded = (counts + tm - 1) // tm * tm
    pad_ends = jnp.cumsum(padded)
    pad_starts = pad_ends - padded
    dest = pad_starts[idx] + rank
    R = (A + tm - 1) // tm * tm + E * tm
    nb = R // tm
    block_start = jnp.arange(nb, dtype=jnp.int32) * tm
    block_e = jnp.minimum(jnp.sum(block_start[:, None] >= pad_ends[None, :], axis=1), E - 1).astype(jnp.int32)
    n_used = (pad_ends[-1:] // tm).astype(jnp.int32)

    xs = _scatter_norm(x, g_moe, dest, pad_ends, padded, R, tm=tm)
    act = _moe_up(xs, w_gate_up, b_gate_up, block_e, n_used, tm=tm)
    rows = _moe_down(act, w_down, b_down, block_e, n_used, tm=tm)
    return _combine(x, rows, dest, slab_g, g_final, final_norm)


def kernel(x, mem, g_mix, w_in, b_in, w_pool, pool_scale, w_fox_o, w_out, g_mem_q, g_mem_kv, w_mem_q, w_mem_kv,
           w_mem_o, g_moe, w_router, b_router, w_gate_up, b_gate_up, w_down, b_down, g_final):
    B, S, D = x.shape
    T = B * S
    n_mem = mem.shape[1]
    depth, G, C, Do = w_pool.shape
    pool_w = G * C
    fox_w = w_fox_o.shape[1]
    H = w_in.shape[2] - pool_w - 3 * fox_w - 2 * D
    dh = fox_w // H
    off_q, off_f = pool_w, pool_w + 3 * fox_w
    off_gate = off_f + H
    xt = x.reshape(T, D)
    mt = mem.reshape(B * n_mem, D)
    for l in range(depth):
        wl, bl = w_in[l], b_in[l]
        h, lf_t = _norm(xt, g_mix[l], wl[:, off_f:off_gate].T.astype(BF16), bl[off_f:off_gate])
        u = _proj(h, wl, bl, col_off=0, n_cols=off_q, out_dtype=F32)
        qkv = _proj(h, wl, bl, col_off=off_q, n_cols=off_f - off_q, out_dtype=BF16)
        gates = _proj(h, wl[:, off_gate:], bl[off_gate:], mode="sigmoid", out_dtype=BF16)
        c_t = _forget_cumsum(lf_t, B, S)
        att = _fox_attention(qkv, c_t.transpose(0, 2, 1), c_t.reshape(B, H, 1, S), B, S, H, dh)
        pp = _pool_mixer(u, w_pool[l], pool_scale[l], gates, 0, S)
        merged = _proj(att, w_fox_o[l], None, (pp, gates), (0, D), mode="merge", out_dtype=BF16)
        xt = _proj(merged, w_out[l], None, (xt,), (0,), mode="residual", out_dtype=F32)
        kv = _proj(_norm(mt, g_mem_kv[l]), w_mem_kv[l], out_dtype=BF16)
        qm = _proj(_norm(xt, g_mem_q[l]), w_mem_q[l], out_dtype=BF16)
        om = _mem_attention(qm, kv, B, S, n_mem)
        xt = _proj(om, w_mem_o[l], None, (xt,), (0,), mode="residual", out_dtype=F32)
        xt = _moe(xt, g_moe[l], w_router[l], b_router[l], w_gate_up[l], b_gate_up[l], w_down[l], b_down[l],
                  g_final, final_norm=l == depth - 1)
    return xt.reshape(B, S, D)
```

```python
import functools

import jax
import jax.numpy as jnp
from jax import lax
from jax.experimental import pallas as pl
from jax.experimental.pallas import tpu as pltpu

F32 = jnp.float32
BF16 = jnp.bfloat16

EPS = 1e-5
POOL_WINDOWS = (2, 4, 8, 16)
POOL_HALO = 16
MEM_HEADS = 4
TOP_K = 4
SWIGLU_LIMIT = 7.0
SWIGLU_ALPHA = 1.702
NEG_BIG = -1e30

LANES = 128
VMEM_LIMIT_BYTES = 56 * 1024 * 1024


def _tile(dim, pref):
    t = pref
    while t >= 8:
        if dim % t == 0:
            return t
        t //= 2
    return dim


def _params(*sem):
    return pltpu.CompilerParams(dimension_semantics=sem, vmem_limit_bytes=VMEM_LIMIT_BYTES)


def _rms(x, g):
    ms = jnp.mean(x * x, axis=-1, keepdims=True)
    return x * lax.rsqrt(ms + EPS) * g


def _split3(x):
    hi = x.astype(BF16)
    r1 = x - hi.astype(F32)
    mid = r1.astype(BF16)
    lo = (r1 - mid.astype(F32)).astype(BF16)
    return hi, mid, lo


def _norm_kernel(x_ref, g_ref, *refs, with_f):
    h = _rms(x_ref[...], g_ref[...]).astype(BF16)
    if with_f:
        wf_ref, bf_ref, h_ref, lf_ref = refs
        f = lax.dot_general(wf_ref[...], h, (((1,), (1,)), ((), ())), preferred_element_type=F32) + bf_ref[...]
        lf_ref[...] = jnp.minimum(f, 0.0) - jnp.log1p(jnp.exp(-jnp.abs(f)))
    else:
        h_ref, = refs
    h_ref[...] = h


def _norm(x, g, wf_t=None, bf_t=None, *, tm=512):
    M, K = x.shape
    tm = _tile(M, tm)
    with_f = wf_t is not None
    in_specs = [pl.BlockSpec((tm, K), lambda i: (i, 0)), pl.BlockSpec((1, K), lambda i: (0, 0))]
    args = [x, g.reshape(1, K)]
    out_shape = [jax.ShapeDtypeStruct((M, K), BF16)]
    out_specs = [pl.BlockSpec((tm, K), lambda i: (i, 0))]
    if with_f:
        H = wf_t.shape[0]
        in_specs += [pl.BlockSpec((H, K), lambda i: (0, 0)), pl.BlockSpec((H, 1), lambda i: (0, 0))]
        args += [wf_t, bf_t.reshape(H, 1)]
        out_shape.append(jax.ShapeDtypeStruct((H, M), F32))
        out_specs.append(pl.BlockSpec((H, tm), lambda i: (0, i)))
    res = pl.pallas_call(
        functools.partial(_norm_kernel, with_f=with_f),
        grid=(M // tm,),
        in_specs=in_specs, out_specs=out_specs, out_shape=out_shape,
        compiler_params=_params("parallel"),
    )(*args)
    return res if with_f else res[0]


def _proj_kernel(x_ref, w_ref, *refs, mode, has_bias):
    refs = list(refs)
    wb_ref = refs.pop()
    o_ref = refs.pop()
    b_ref = refs.pop(0) if has_bias else None

    @pl.when(pl.program_id(1) == 0)
    def _():
        wb_ref[...] = w_ref[...].astype(BF16)

    acc = jnp.dot(x_ref[...], wb_ref[...], preferred_element_type=F32)
    if has_bias:
        acc = acc + b_ref[...]
    if mode == "sigmoid":
        acc = jax.nn.sigmoid(acc)
    elif mode == "merge":
        pp_ref, g1_ref = refs
        acc = pp_ref[...] + g1_ref[...].astype(F32) * acc
    elif mode == "residual":
        acc = refs[0][...] + acc
    o_ref[...] = acc.astype(o_ref.dtype)


def _proj(x, w, b=None, extras=(), extra_col_off=(), *, col_off=0, n_cols=None, mode="plain", out_dtype,
          tm=1024, tn=512):
    M, K = x.shape
    N = w.shape[1] - col_off if n_cols is None else n_cols
    tm, tn = _tile(M, tm), _tile(N, tn)
    while any(off % tn for off in (col_off, *extra_col_off)):
        tn //= 2
    assert tn % LANES == 0
    in_specs = [pl.BlockSpec((tm, K), lambda j, i: (i, 0)),
                pl.BlockSpec((K, tn), lambda j, i, o=col_off // tn: (0, j + o))]
    args = [x, w]
    if b is not None:
        in_specs.append(pl.BlockSpec((1, tn), lambda j, i, o=col_off // tn: (0, j + o)))
        args.append(b.reshape(1, -1))
    for off in extra_col_off:
        in_specs.append(pl.BlockSpec((tm, tn), lambda j, i, o=off // tn: (i, j + o)))
    return pl.pallas_call(
        functools.partial(_proj_kernel, mode=mode, has_bias=b is not None),
        grid=(N // tn, M // tm),
        in_specs=in_specs,
        out_specs=pl.BlockSpec((tm, tn), lambda j, i: (i, j)),
        out_shape=jax.ShapeDtypeStruct((M, N), out_dtype),
        scratch_shapes=[pltpu.VMEM((K, tn), BF16)],
        compiler_params=_params("parallel", "arbitrary"),
    )(*args, *extras)


def _cumsum_kernel(lf_ref, c_ref):
    S = lf_ref.shape[1]
    row = lax.broadcasted_iota(jnp.int32, (S, S), 0)
    col = lax.broadcasted_iota(jnp.int32, (S, S), 1)
    upper = (row <= col).astype(BF16)
    c = jnp.zeros(lf_ref.shape, F32)
    for part in _split3(lf_ref[...]):
        c = c + jnp.dot(part, upper, preferred_element_type=F32)
    c_ref[...] = c


def _forget_cumsum(lf_t, B, S):
    H = lf_t.shape[0]
    return pl.pallas_call(
        _cumsum_kernel,
        grid=(B,),
        in_specs=[pl.BlockSpec((H, S), lambda b: (0, b))],
        out_specs=pl.BlockSpec((None, H, S), lambda b: (b, 0, 0)),
        out_shape=jax.ShapeDtypeStruct((B, H, S), F32),
        compiler_params=_params("parallel"),
    )(lf_t)


def _pool_kernel(u_ref, halo_ref, w_ref, sc_ref, g0_ref, o_ref, ext_ref, pooled_ref, *, seq_len):
    g = pl.program_id(0)
    i = pl.program_id(1)
    tp = u_ref.shape[0]
    pos0 = (i * tp) % seq_len
    u = u_ref[...]
    ext_ref[pl.ds(POOL_HALO, tp), :] = u
    ext_ref[pl.ds(0, POOL_HALO), :] = jnp.where(pos0 == 0, 0.0, halo_ref[...])
    pos = pos0 + lax.broadcasted_iota(jnp.int32, (tp, 1), 0)
    for gi, win in enumerate(POOL_WINDOWS):
        @pl.when(g == gi)
        def _(win=win):
            acc = u
            for k in range(1, win):
                acc = acc + ext_ref[pl.ds(POOL_HALO - k, tp), :]
            cnt = jnp.minimum(pos + 1, win).astype(F32)
            pooled_ref[...] = (acc / cnt - u).astype(BF16)
    y = jnp.dot(pooled_ref[...], w_ref[...].astype(BF16), preferred_element_type=F32)
    o_ref[...] = g0_ref[...].astype(F32) * (y * sc_ref[...])


def _pool_mixer(u, w_pool, scale, gates, gate0_col, seq_len, *, tp=512):
    T = u.shape[0]
    G, C, Do = w_pool.shape
    tp = _tile(seq_len, tp)
    assert tp % POOL_HALO == 0 and gate0_col % Do == 0
    hb = tp // POOL_HALO
    return pl.pallas_call(
        functools.partial(_pool_kernel, seq_len=seq_len),
        grid=(G, T // tp),
        in_specs=[pl.BlockSpec((tp, C), lambda g, i: (i, g)),
                  pl.BlockSpec((POOL_HALO, C), lambda g, i: (jnp.maximum(i * hb - 1, 0), g)),
                  pl.BlockSpec((None, C, Do), lambda g, i: (g, 0, 0)),
                  pl.BlockSpec((1, Do), lambda g, i: (0, g)),
                  pl.BlockSpec((tp, Do), lambda g, i, o=gate0_col // Do: (i, g + o))],
        out_specs=pl.BlockSpec((tp, Do), lambda g, i: (i, g)),
        out_shape=jax.ShapeDtypeStruct((T, G * Do), F32),
        scratch_shapes=[pltpu.VMEM((tp + POOL_HALO, C), F32), pltpu.VMEM((tp, C), BF16)],
        compiler_params=_params("parallel", "parallel"),
    )(u, u, w_pool, scale.reshape(1, G * Do), gates)


def _fox_kernel(q_ref, k_ref, v_ref, cq_ref, ck_ref, o_ref, *, scale, dh):
    hg = pl.program_id(1)
    qi = pl.program_id(2)
    tq = q_ref.shape[0]
    n_heads = q_ref.shape[1] // dh
    lane = lax.broadcasted_iota(jnp.int32, cq_ref.shape, 1)
    cq_all = cq_ref[...]
    q = [q_ref[:, n * dh:(n + 1) * dh] for n in range(n_heads)]
    cq = [jnp.sum(jnp.where(lane == hg * n_heads + n, cq_all, 0.0), axis=1, keepdims=True) for n in range(n_heads)]

    def scores(n, j):
        rows = pl.ds(pl.multiple_of(j * tq, tq), tq)
        s = lax.dot_general(q[n], k_ref[rows, n * dh:(n + 1) * dh], (((1,), (1,)), ((), ())),
                            preferred_element_type=F32) * scale
        return s + (cq[n] - ck_ref[n:n + 1, rows])

    def update(n, j, s, carry):
        m, l, acc = carry
        m_new = jnp.maximum(m, jnp.max(s, axis=1, keepdims=True))
        alpha = jnp.exp(m - m_new)
        p = jnp.exp(s - m_new)
        v = v_ref[pl.ds(pl.multiple_of(j * tq, tq), tq), n * dh:(n + 1) * dh]
        acc = alpha * acc + jnp.dot(p.astype(BF16), v, preferred_element_type=F32)
        return m_new, alpha * l + jnp.sum(p, axis=1, keepdims=True), acc

    init = (jnp.full((tq, 1), NEG_BIG, F32), jnp.zeros((tq, 1), F32), jnp.zeros((tq, dh), F32))
    carries = lax.fori_loop(
        0, qi, lambda j, cs: tuple(update(n, j, scores(n, j), cs[n]) for n in range(n_heads)), (init,) * n_heads)
    row = lax.broadcasted_iota(jnp.int32, (tq, tq), 0)
    col = lax.broadcasted_iota(jnp.int32, (tq, tq), 1)
    for n in range(n_heads):
        s = jnp.where(col <= row, scores(n, qi), NEG_BIG)
        m, l, acc = update(n, qi, s, carries[n])
        o_ref[:, n * dh:(n + 1) * dh] = (acc / l).astype(o_ref.dtype)


def _fox_attention(qkv, c, c_t, B, S, H, dh, *, tq=512, heads_per_step=2):
    T = B * S
    tq = _tile(S, tq)
    nq = S // tq
    hp = heads_per_step if H % heads_per_step == 0 else 1
    ng = H // hp
    w = hp * dh
    return pl.pallas_call(
        functools.partial(_fox_kernel, scale=dh ** -0.5, dh=dh),
        grid=(B, ng, nq),
        in_specs=[pl.BlockSpec((tq, w), lambda b, h, i: (b * nq + i, h)),
                  pl.BlockSpec((S, w), lambda b, h, i: (b, ng + h)),
                  pl.BlockSpec((S, w), lambda b, h, i: (b, 2 * ng + h)),
                  pl.BlockSpec((None, tq, H), lambda b, h, i: (b, i, 0)),
                  pl.BlockSpec((None, None, hp, S), lambda b, h, i: (b, h, 0, 0))],
        out_specs=pl.BlockSpec((tq, w), lambda b, h, i: (b * nq + i, h)),
        out_shape=jax.ShapeDtypeStruct((T, H * dh), BF16),
        compiler_params=_params("parallel", "parallel", "parallel"),
    )(qkv, qkv, qkv, c, c_t.reshape(B, ng, hp, S))


def _mem_attn_kernel(q_ref, kv_ref, o_ref, *, heads):
    width = q_ref.shape[1]
    dh = width // heads
    scale = dh ** -0.5
    for hd in range(heads):
        q = q_ref[:, hd * dh:(hd + 1) * dh]
        k = kv_ref[:, hd * dh:(hd + 1) * dh]
        v = kv_ref[:, width + hd * dh:width + (hd + 1) * dh]
        s = lax.dot_general(q, k, (((1,), (1,)), ((), ())), preferred_element_type=F32) * scale
        p = jnp.exp(s - jnp.max(s, axis=1, keepdims=True))
        p = p / jnp.sum(p, axis=1, keepdims=True)
        o_ref[:, hd * dh:(hd + 1) * dh] = jnp.dot(p.astype(BF16), v, preferred_element_type=F32).astype(o_ref.dtype)


def _mem_attention(q, kv, B, S, n_mem, *, tq=512):
    T, width = q.shape
    tq = _tile(S, tq)
    nq = S // tq
    return pl.pallas_call(
        functools.partial(_mem_attn_kernel, heads=MEM_HEADS),
        grid=(B, nq),
        in_specs=[pl.BlockSpec((tq, width), lambda b, i: (b * nq + i, 0)),
                  pl.BlockSpec((n_mem, 2 * width), lambda b, i: (b, 0))],
        out_specs=pl.BlockSpec((tq, width), lambda b, i: (b * nq + i, 0)),
        out_shape=jax.ShapeDtypeStruct((T, width), BF16),
        compiler_params=_params("parallel", "parallel"),
    )(q, kv)


def _router_kernel(x_ref, g_ref, whi_ref, wlo_ref, b_ref, oi_ref, og_ref, cnt_ref, carry_ref, *, n_experts):
    i = pl.program_id(0)
    tm = x_ref.shape[0]

    @pl.when(i == 0)
    def _():
        carry_ref[...] = jnp.zeros_like(carry_ref)

    h = _rms(x_ref[...], g_ref[...])
    h_hi = h.astype(BF16)
    h_lo = (h - h_hi.astype(F32)).astype(BF16)
    logits = (jnp.dot(h_hi, whi_ref[...], preferred_element_type=F32)
              + jnp.dot(h_hi, wlo_ref[...], preferred_element_type=F32)
              + jnp.dot(h_lo, whi_ref[...], preferred_element_type=F32)) + b_ref[...]
    lane = lax.broadcasted_iota(jnp.int32, (tm, LANES), 1).astype(F32)
    work = jnp.where(lane < n_experts, logits, -jnp.inf)
    vals, idxs = [], []
    for _ in range(TOP_K):
        m = jnp.max(work, axis=1, keepdims=True)
        idx = jnp.min(jnp.where(work == m, lane, float(LANES)), axis=1, keepdims=True)
        vals.append(m)
        idxs.append(idx)
        work = jnp.where(lane == idx, -jnp.inf, work)
    exps = [jnp.exp(v - vals[0]) for v in vals]
    denom = exps[0] + exps[1] + exps[2] + exps[3]
    onehots = [(lane == idx).astype(F32) for idx in idxs]
    chosen = onehots[0] + onehots[1] + onehots[2] + onehots[3]
    row = lax.broadcasted_iota(jnp.int32, (tm, tm), 0)
    col = lax.broadcasted_iota(jnp.int32, (tm, tm), 1)
    before = jnp.dot((col < row).astype(BF16), chosen.astype(BF16), preferred_element_type=F32) + carry_ref[...]
    out_i = jnp.zeros((tm, LANES), F32)
    out_g = jnp.zeros((tm, LANES), F32)
    for k in range(TOP_K):
        rank = jnp.sum(onehots[k] * before, axis=1, keepdims=True)
        out_i = jnp.where(lane == k, idxs[k], out_i)
        out_i = jnp.where(lane == TOP_K + k, rank, out_i)
        out_g = jnp.where(lane == k, exps[k] / denom, out_g)
    oi_ref[...] = out_i.astype(jnp.int32)
    og_ref[...] = out_g
    carry_ref[...] = carry_ref[...] + jnp.sum(chosen, axis=0, keepdims=True)
    cnt_ref[...] = carry_ref[...]


def _router(x, g, w_router, b_router, *, tm=512):
    T, D = x.shape
    E = w_router.shape[1]
    tm = _tile(T, tm)
    w_pad = jnp.zeros((D, LANES), F32).at[:, :E].set(w_router)
    w_hi = w_pad.astype(BF16)
    w_lo = (w_pad - w_hi.astype(F32)).astype(BF16)
    b_pad = jnp.zeros((1, LANES), F32).at[0, :E].set(b_router)
    full = lambda i: (0, 0)
    return pl.pallas_call(
        functools.partial(_router_kernel, n_experts=E),
        grid=(T // tm,),
        in_specs=[pl.BlockSpec((tm, D), lambda i: (i, 0)), pl.BlockSpec((1, D), full),
                  pl.BlockSpec((D, LANES), full), pl.BlockSpec((D, LANES), full), pl.BlockSpec((1, LANES), full)],
        out_specs=[pl.BlockSpec((tm, LANES), lambda i: (i, 0)), pl.BlockSpec((tm, LANES), lambda i: (i, 0)),
                   pl.BlockSpec((1, LANES), full)],
        out_shape=[jax.ShapeDtypeStruct((T, LANES), jnp.int32), jax.ShapeDtypeStruct((T, LANES), F32),
                   jax.ShapeDtypeStruct((1, LANES), F32)],
        scratch_shapes=[pltpu.VMEM((1, LANES), F32)],
        compiler_params=_params("arbitrary"),
    )(x, g.reshape(1, D), w_hi, w_lo, b_pad)


def _row_copy(src_hbm, row, buf, slot, r, sem):
    return pltpu.make_async_copy(src_hbm.at[pl.ds(row, 1)], buf.at[slot, pl.ds(r, 1)], sem.at[slot])


def _start_rows(idx_ref, src_hbm, buf, slot, sem, n_rows):
    def body(r, _):
        _row_copy(src_hbm, idx_ref[0, 0, r], buf, slot, r, sem).start()
        return 0
    lax.fori_loop(0, n_rows, body, 0, unroll=8)


def _wait_rows(src_hbm, buf, slot, sem, n_rows):
    pltpu.make_async_copy(src_hbm.at[pl.ds(0, n_rows)], buf.at[slot], sem.at[slot]).wait()


def _gather_pipeline(idx_ref, nxt_ref, src_hbm, buf, sem, n_rows):
    i = pl.program_id(0)
    n = pl.num_programs(0)
    slot = i % 2

    @pl.when(i == 0)
    def _():
        _start_rows(idx_ref, src_hbm, buf, 0, sem, n_rows)

    @pl.when(i + 1 < n)
    def _():
        _start_rows(nxt_ref, src_hbm, buf, 1 - slot, sem, n_rows)

    _wait_rows(src_hbm, buf, slot, sem, n_rows)
    return slot


def _pack_bf16_pairs(h):
    half = h.shape[1] // 2
    bits = pltpu.bitcast(h.astype(BF16).astype(F32), jnp.uint32)
    return (bits[:, :half] >> 16) | bits[:, half:]


def _unpack_bf16_pairs(u):
    lo = pltpu.bitcast(u << 16, F32).astype(BF16)
    hi = pltpu.bitcast(u & jnp.uint32(0xFFFF0000), F32).astype(BF16)
    return lo, hi


def _scatter_norm_kernel(pe_ref, pd_ref, dest_ref, x_ref, g_ref, xs_hbm, buf, zbuf, sem, zsem, *, tm, n_experts):
    i = pl.program_id(0)
    n = pl.num_programs(0)
    tb = x_ref.shape[0]
    slot = i % 2

    def row_copy(s, t, row):
        return pltpu.make_async_copy(buf.at[s, pl.ds(t, 1)], xs_hbm.at[pl.ds(row, 1)], sem.at[s])

    def wait_slot(s):
        for _ in range(TOP_K):
            pltpu.make_async_copy(buf.at[s], xs_hbm.at[pl.ds(0, tb)], sem.at[s]).wait()

    def zero_group(first_row):
        return pltpu.make_async_copy(zbuf, xs_hbm.at[pl.ds(pl.multiple_of(first_row, tm), tm)], zsem)

    @pl.when(i == 0)
    def _():
        zbuf[...] = jnp.zeros_like(zbuf)
        n_groups = xs_hbm.shape[0] // tm
        used = pe_ref[n_experts - 1] // tm
        for e in range(n_experts):
            @pl.when(pd_ref[e] > 0)
            def _(e=e):
                zero_group(pe_ref[e] - tm).start()
        lax.fori_loop(used, n_groups, lambda gi, c: (zero_group(gi * tm).start(), c)[1], 0)
        for e in range(n_experts):
            @pl.when(pd_ref[e] > 0)
            def _(e=e):
                zero_group(pe_ref[e] - tm).wait()
        lax.fori_loop(used, n_groups, lambda gi, c: (zero_group(gi * tm).wait(), c)[1], 0)

    @pl.when(i >= 2)
    def _():
        wait_slot(slot)

    buf[slot] = _pack_bf16_pairs(_rms(x_ref[...], g_ref[...]))

    def body(t, _):
        for k in range(TOP_K):
            row_copy(slot, t, dest_ref[0, 0, t * TOP_K + k]).start()
        return 0
    lax.fori_loop(0, tb, body, 0, unroll=4)

    @pl.when(i == n - 1)
    def _():
        @pl.when(n >= 2)
        def _():
            wait_slot(1 - slot)
        wait_slot(slot)


def _scatter_norm(x, g, dest, pad_ends, padded, n_rows, *, tm, tb=128):
    T, D = x.shape
    E = pad_ends.shape[0]
    tb = _tile(T, tb)
    nb = T // tb
    grid_spec = pltpu.PrefetchScalarGridSpec(
        num_scalar_prefetch=2, grid=(nb,),
        in_specs=[pl.BlockSpec((1, 1, TOP_K * tb), lambda i, pe, pd: (i, 0, 0), memory_space=pltpu.SMEM),
                  pl.BlockSpec((tb, D), lambda i, pe, pd: (i, 0)),
                  pl.BlockSpec((1, D), lambda i, pe, pd: (0, 0))],
        out_specs=pl.BlockSpec(memory_space=pl.ANY),
        scratch_shapes=[pltpu.VMEM((2, tb, D // 2), jnp.uint32), pltpu.VMEM((tm, D // 2), jnp.uint32),
                        pltpu.SemaphoreType.DMA((2,)), pltpu.SemaphoreType.DMA(())])
    return pl.pallas_call(
        functools.partial(_scatter_norm_kernel, tm=tm, n_experts=E), grid_spec=grid_spec,
        out_shape=jax.ShapeDtypeStruct((n_rows, D // 2), jnp.uint32),
        compiler_params=_params("arbitrary"),
    )(pad_ends, padded, dest.reshape(nb, 1, TOP_K * tb), x, g.reshape(1, D))


def _combine_kernel(idx_ref, nxt_ref, x_ref, gate_ref, g_ref, rows_hbm, o_ref, buf, sem, *, final_norm):
    tb = x_ref.shape[0]
    slot = _gather_pipeline(idx_ref, nxt_ref, rows_hbm, buf, sem, TOP_K * tb)
    y = x_ref[...]
    gates = gate_ref[...]
    for k in range(TOP_K):
        y = y + gates[:, k:k + 1] * buf[slot, pl.ds(k * tb, tb), :]
    o_ref[...] = _rms(y, g_ref[...]) if final_norm else y


def _combine(x, rows, dest, gates_slab, g, final_norm, *, tb=64):
    T, D = x.shape
    tb = _tile(T, tb)
    nb = T // tb
    idx3 = dest.reshape(nb, tb, TOP_K).transpose(0, 2, 1).reshape(nb, 1, TOP_K * tb)
    return pl.pallas_call(
        functools.partial(_combine_kernel, final_norm=final_norm),
        grid=(nb,),
        in_specs=[pl.BlockSpec((1, 1, TOP_K * tb), lambda i: (i, 0, 0), memory_space=pltpu.SMEM),
                  pl.BlockSpec((1, 1, TOP_K * tb), lambda i: (jnp.minimum(i + 1, nb - 1), 0, 0), memory_space=pltpu.SMEM),
                  pl.BlockSpec((tb, D), lambda i: (i, 0)),
                  pl.BlockSpec((tb, LANES), lambda i: (i, 0)),
                  pl.BlockSpec((1, D), lambda i: (0, 0)),
                  pl.BlockSpec(memory_space=pl.ANY)],
        out_specs=pl.BlockSpec((tb, D), lambda i: (i, 0)),
        out_shape=jax.ShapeDtypeStruct((T, D), F32),
        scratch_shapes=[pltpu.VMEM((2, TOP_K * tb, D), F32), pltpu.SemaphoreType.DMA((2,))],
        compiler_params=_params("arbitrary"),
    )(idx3, idx3, x, gates_slab, g.reshape(1, D), rows)


def _new_expert(be_ref, i):
    return jnp.logical_or(i == 0, be_ref[i] != be_ref[jnp.maximum(i - 1, 0)])


def _moe_up_kernel(be_ref, nu_ref, xs_ref, w_ref, b_ref, o_ref, wb_ref):
    i = pl.program_id(1)
    half = xs_ref.shape[1]
    pair = 2 * LANES

    @pl.when(_new_expert(be_ref, i))
    def _():
        wb_ref[...] = w_ref[...].astype(BF16)

    @pl.when(i < nu_ref[0])
    def _():
        lo, hi = _unpack_bf16_pairs(xs_ref[...])
        gu = (jnp.dot(lo, wb_ref[:half, :], preferred_element_type=F32)
              + jnp.dot(hi, wb_ref[half:, :], preferred_element_type=F32)) + b_ref[...]
        row = lax.broadcasted_iota(jnp.int32, (pair, LANES), 0)
        col = lax.broadcasted_iota(jnp.int32, (pair, LANES), 1)
        pick_even = (row == 2 * col).astype(BF16)
        even = lax.broadcasted_iota(jnp.int32, (1, pair), 1) % 2 == 0
        for c in range(gu.shape[1] // pair):
            blk = gu[:, c * pair:(c + 1) * pair]
            nxt = pltpu.roll(blk, pair - 1, axis=1)
            gate = jnp.minimum(blk, SWIGLU_LIMIT)
            up = jnp.clip(nxt, -SWIGLU_LIMIT, SWIGLU_LIMIT)
            act = (up + 1.0) * gate * jax.nn.sigmoid(SWIGLU_ALPHA * gate)
            act = jnp.where(even, act, 0.0).astype(BF16)
            o_ref[:, c * LANES:(c + 1) * LANES] = jnp.dot(act, pick_even, preferred_element_type=F32).astype(o_ref.dtype)

    @pl.when(i >= nu_ref[0])
    def _():
        o_ref[...] = jnp.zeros_like(o_ref)


def _moe_up(xs, w_gu, b_gu, block_e, n_used, *, tm, tn=512):
    R, half = xs.shape
    E, D, F2 = w_gu.shape
    tn = _tile(F2, tn)
    assert tn % (2 * LANES) == 0 and D == 2 * half
    nb = R // tm
    grid_spec = pltpu.PrefetchScalarGridSpec(
        num_scalar_prefetch=2, grid=(F2 // tn, nb),
        in_specs=[pl.BlockSpec((tm, half), lambda j, i, be, nu: (jnp.minimum(i, nu[0] - 1), 0)),
                  pl.BlockSpec((None, D, tn), lambda j, i, be, nu: (be[i], 0, j)),
                  pl.BlockSpec((None, 1, tn), lambda j, i, be, nu: (be[i], 0, j))],
        out_specs=pl.BlockSpec((tm, tn // 2), lambda j, i, be, nu: (i, j)),
        scratch_shapes=[pltpu.VMEM((D, tn), BF16)])
    return pl.pallas_call(
        _moe_up_kernel, grid_spec=grid_spec,
        out_shape=jax.ShapeDtypeStruct((R, F2 // 2), BF16),
        compiler_params=_params("parallel", "arbitrary"),
    )(block_e, n_used, xs, w_gu, b_gu.reshape(E, 1, F2))


def _moe_down_kernel(be_ref, nu_ref, a_ref, w_ref, b_ref, o_ref, wb_ref):
    i = pl.program_id(1)

    @pl.when(_new_expert(be_ref, i))
    def _():
        wb_ref[...] = w_ref[...].astype(BF16)

    @pl.when(i < nu_ref[0])
    def _():
        o_ref[...] = jnp.dot(a_ref[...], wb_ref[...], preferred_element_type=F32) + b_ref[...]

    @pl.when(i >= nu_ref[0])
    def _():
        o_ref[...] = jnp.zeros_like(o_ref)


def _moe_down(act, w_d, b_d, block_e, n_used, *, tm, tn=2048):
    R, F = act.shape
    E, _, D = w_d.shape
    tn = _tile(D, tn)
    nb = R // tm
    grid_spec = pltpu.PrefetchScalarGridSpec(
        num_scalar_prefetch=2, grid=(D // tn, nb),
        in_specs=[pl.BlockSpec((tm, F), lambda j, i, be, nu: (i, 0)),
                  pl.BlockSpec((None, F, tn), lambda j, i, be, nu: (be[i], 0, j)),
                  pl.BlockSpec((None, 1, tn), lambda j, i, be, nu: (be[i], 0, j))],
        out_specs=pl.BlockSpec((tm, tn), lambda j, i, be, nu: (i, j)),
        scratch_shapes=[pltpu.VMEM((F, tn), BF16)])
    return pl.pallas_call(
        _moe_down_kernel, grid_spec=grid_spec,
        out_shape=jax.ShapeDtypeStruct((R, D), F32),
        compiler_params=_params("parallel", "arbitrary"),
    )(block_e, n_used, act, w_d, b_d.reshape(E, 1, D))


def _moe(x, g_moe, w_router, b_router, w_gate_up, b_gate_up, w_down, b_down, g_final, final_norm, *, tm=512):
    T, D = x.shape
    E = w_router.shape[1]
    A = T * TOP_K
    tm = min(tm, A)
    slab_i, slab_g, cnt = _router(x, g_moe, w_router, b_router)
    idx = slab_i[:, :TOP_K]
    rank = slab_i[:, TOP_K:2 * TOP_K]
    counts = cnt[0, :E].astype(jnp.int32)
    padded = (counts + tm - 1) // tm * tm
    pad_ends = jnp.cumsum(padded)
    pad_starts = pad_ends - padded
    dest = pad_starts[idx] + rank
    R = (A + tm - 1) // tm * tm + E * tm
    nb = R // tm
    block_start = jnp.arange(nb, dtype=jnp.int32) * tm
    block_e = jnp.minimum(jnp.sum(block_start[:, None] >= pad_ends[None, :], axis=1), E - 1).astype(jnp.int32)
    n_used = (pad_ends[-1:] // tm).astype(jnp.int32)

    xs = _scatter_norm(x, g_moe, dest, pad_ends, padded, R, tm=tm)
    act = _moe_up(xs, w_gate_up, b_gate_up, block_e, n_used, tm=tm)
    rows = _moe_down(act, w_down, b_down, block_e, n_used, tm=tm)
    return _combine(x, rows, dest, slab_g, g_final, final_norm)


def kernel(x, mem, g_mix, w_in, b_in, w_pool, pool_scale, w_fox_o, w_out, g_mem_q, g_mem_kv, w_mem_q, w_mem_kv,
           w_mem_o, g_moe, w_router, b_router, w_gate_up, b_gate_up, w_down, b_down, g_final):
    B, S, D = x.shape
    T = B * S
    n_mem = mem.shape[1]
    depth, G, C, Do = w_pool.shape
    pool_w = G * C
    fox_w = w_fox_o.shape[1]
    H = w_in.shape[2] - pool_w - 3 * fox_w - 2 * D
    dh = fox_w // H
    off_q, off_f = pool_w, pool_w + 3 * fox_w
    off_gate = off_f + H
    xt = x.reshape(T, D)
    mt = mem.reshape(B * n_mem, D)
    for l in range(depth):
        wl, bl = w_in[l], b_in[l]
        h, lf_t = _norm(xt, g_mix[l], wl[:, off_f:off_gate].T.astype(BF16), bl[off_f:off_gate])
        u = _proj(h, wl, bl, col_off=0, n_cols=off_q, out_dtype=F32)
        qkv = _proj(h, wl, bl, col_off=off_q, n_cols=off_f - off_q, out_dtype=BF16)
        gates = _proj(h, wl[:, off_gate:], bl[off_gate:], mode="sigmoid", out_dtype=BF16)
        c_t = _forget_cumsum(lf_t, B, S)
        att = _fox_attention(qkv, c_t.transpose(0, 2, 1), c_t, B, S, H, dh)
        pp = _pool_mixer(u, w_pool[l], pool_scale[l], gates, 0, S)
        merged = _proj(att, w_fox_o[l], None, (pp, gates), (0, D), mode="merge", out_dtype=BF16)
        xt = _proj(merged, w_out[l], None, (xt,), (0,), mode="residual", out_dtype=F32)
        kv = _proj(_norm(mt, g_mem_kv[l]), w_mem_kv[l], out_dtype=BF16)
        qm = _proj(_norm(xt, g_mem_q[l]), w_mem_q[l], out_dtype=BF16)
        om = _mem_attention(qm, kv, B, S, n_mem)
        xt = _proj(om, w_mem_o[l], None, (xt,), (0,), mode="residual", out_dtype=F32)
        xt = _moe(xt, g_moe[l], w_router[l], b_router[l], w_gate_up[l], b_gate_up[l], w_down[l], b_down[l],
                  g_final, final_norm=l == depth - 1)
    return xt.reshape(B, S, D)
```

```python
import functools

import jax
import jax.numpy as jnp
from jax import lax
from jax.experimental import pallas as pl
from jax.experimental.pallas import tpu as pltpu

F32 = jnp.float32
BF16 = jnp.bfloat16

EPS = 1e-5
POOL_WINDOWS = (2, 4, 8, 16)
POOL_HALO = 16
MEM_HEADS = 4
TOP_K = 4
SWIGLU_LIMIT = 7.0
SWIGLU_ALPHA = 1.702
NEG_BIG = -1e30

LANES = 128
VMEM_LIMIT_BYTES = 56 * 1024 * 1024


def _tile(dim, pref):
    t = pref
    while t >= 8:
        if dim % t == 0:
            return t
        t //= 2
    return dim


def _params(*sem):
    return pltpu.CompilerParams(dimension_semantics=sem, vmem_limit_bytes=VMEM_LIMIT_BYTES)


def _rms(x, g):
    ms = jnp.mean(x * x, axis=-1, keepdims=True)
    return x * lax.rsqrt(ms + EPS) * g


def _split3(x):
    hi = x.astype(BF16)
    r1 = x - hi.astype(F32)
    mid = r1.astype(BF16)
    lo = (r1 - mid.astype(F32)).astype(BF16)
    return hi, mid, lo


def _norm_kernel(x_ref, g_ref, *refs, with_f):
    h = _rms(x_ref[...], g_ref[...]).astype(BF16)
    if with_f:
        wf_ref, bf_ref, h_ref, lf_ref = refs
        f = lax.dot_general(wf_ref[...], h, (((1,), (1,)), ((), ())), preferred_element_type=F32) + bf_ref[...]
        lf_ref[...] = jnp.minimum(f, 0.0) - jnp.log1p(jnp.exp(-jnp.abs(f)))
    else:
        h_ref, = refs
    h_ref[...] = h


def _norm(x, g, wf_t=None, bf_t=None, *, tm=512):
    M, K = x.shape
    tm = _tile(M, tm)
    with_f = wf_t is not None
    in_specs = [pl.BlockSpec((tm, K), lambda i: (i, 0)), pl.BlockSpec((1, K), lambda i: (0, 0))]
    args = [x, g.reshape(1, K)]
    out_shape = [jax.ShapeDtypeStruct((M, K), BF16)]
    out_specs = [pl.BlockSpec((tm, K), lambda i: (i, 0))]
    if with_f:
        H = wf_t.shape[0]
        in_specs += [pl.BlockSpec((H, K), lambda i: (0, 0)), pl.BlockSpec((H, 1), lambda i: (0, 0))]
        args += [wf_t, bf_t.reshape(H, 1)]
        out_shape.append(jax.ShapeDtypeStruct((H, M), F32))
        out_specs.append(pl.BlockSpec((H, tm), lambda i: (0, i)))
    res = pl.pallas_call(
        functools.partial(_norm_kernel, with_f=with_f),
        grid=(M // tm,),
        in_specs=in_specs, out_specs=out_specs, out_shape=out_shape,
        compiler_params=_params("parallel"),
    )(*args)
    return res if with_f else res[0]


def _proj_kernel(x_ref, w_ref, *refs, mode, has_bias):
    refs = list(refs)
    wb_ref = refs.pop()
    o_ref = refs.pop()
    b_ref = refs.pop(0) if has_bias else None

    @pl.when(pl.program_id(1) == 0)
    def _():
        wb_ref[...] = w_ref[...].astype(BF16)

    acc = jnp.dot(x_ref[...], wb_ref[...], preferred_element_type=F32)
    if has_bias:
        acc = acc + b_ref[...]
    if mode == "sigmoid":
        acc = jax.nn.sigmoid(acc)
    elif mode == "merge":
        pp_ref, g1_ref = refs
        acc = pp_ref[...] + g1_ref[...].astype(F32) * acc
    elif mode == "residual":
        acc = refs[0][...] + acc
    o_ref[...] = acc.astype(o_ref.dtype)


def _proj(x, w, b=None, extras=(), extra_col_off=(), *, col_off=0, n_cols=None, mode="plain", out_dtype,
          tm=1024, tn=512):
    M, K = x.shape
    N = w.shape[1] - col_off if n_cols is None else n_cols
    tm, tn = _tile(M, tm), _tile(N, tn)
    while any(off % tn for off in (col_off, *extra_col_off)):
        tn //= 2
    assert tn % LANES == 0
    in_specs = [pl.BlockSpec((tm, K), lambda j, i: (i, 0)),
                pl.BlockSpec((K, tn), lambda j, i, o=col_off // tn: (0, j + o))]
    args = [x, w]
    if b is not None:
        in_specs.append(pl.BlockSpec((1, tn), lambda j, i, o=col_off // tn: (0, j + o)))
        args.append(b.reshape(1, -1))
    for off in extra_col_off:
        in_specs.append(pl.BlockSpec((tm, tn), lambda j, i, o=off // tn: (i, j + o)))
    return pl.pallas_call(
        functools.partial(_proj_kernel, mode=mode, has_bias=b is not None),
        grid=(N // tn, M // tm),
        in_specs=in_specs,
        out_specs=pl.BlockSpec((tm, tn), lambda j, i: (i, j)),
        out_shape=jax.ShapeDtypeStruct((M, N), out_dtype),
        scratch_shapes=[pltpu.VMEM((K, tn), BF16)],
        compiler_params=_params("parallel", "arbitrary"),
    )(*args, *extras)


def _cumsum_kernel(lf_ref, c_ref):
    S = lf_ref.shape[1]
    row = lax.broadcasted_iota(jnp.int32, (S, S), 0)
    col = lax.broadcasted_iota(jnp.int32, (S, S), 1)
    upper = (row <= col).astype(BF16)
    c = jnp.zeros(lf_ref.shape, F32)
    for part in _split3(lf_ref[...]):
        c = c + jnp.dot(part, upper, preferred_element_type=F32)
    c_ref[...] = c


def _forget_cumsum(lf_t, B, S):
    H = lf_t.shape[0]
    return pl.pallas_call(
        _cumsum_kernel,
        grid=(B,),
        in_specs=[pl.BlockSpec((H, S), lambda b: (0, b))],
        out_specs=pl.BlockSpec((None, H, S), lambda b: (b, 0, 0)),
        out_shape=jax.ShapeDtypeStruct((B, H, S), F32),
        compiler_params=_params("parallel"),
    )(lf_t)


def _pool_kernel(u_ref, halo_ref, w_ref, sc_ref, g0_ref, o_ref, ext_ref, pooled_ref, *, seq_len):
    g = pl.program_id(0)
    i = pl.program_id(1)
    tp = u_ref.shape[0]
    pos0 = (i * tp) % seq_len
    u = u_ref[...]
    ext_ref[pl.ds(POOL_HALO, tp), :] = u
    ext_ref[pl.ds(0, POOL_HALO), :] = jnp.where(pos0 == 0, 0.0, halo_ref[...])
    pos = pos0 + lax.broadcasted_iota(jnp.int32, (tp, 1), 0)
    for gi, win in enumerate(POOL_WINDOWS):
        @pl.when(g == gi)
        def _(win=win):
            acc = u
            for k in range(1, win):
                acc = acc + ext_ref[pl.ds(POOL_HALO - k, tp), :]
            cnt = jnp.minimum(pos + 1, win).astype(F32)
            pooled_ref[...] = (acc / cnt - u).astype(BF16)
    y = jnp.dot(pooled_ref[...], w_ref[...].astype(BF16), preferred_element_type=F32)
    o_ref[...] = g0_ref[...].astype(F32) * (y * sc_ref[...])


def _pool_mixer(u, w_pool, scale, gates, gate0_col, seq_len, *, tp=512):
    T = u.shape[0]
    G, C, Do = w_pool.shape
    tp = _tile(seq_len, tp)
    assert tp % POOL_HALO == 0 and gate0_col % Do == 0
    hb = tp // POOL_HALO
    return pl.pallas_call(
        functools.partial(_pool_kernel, seq_len=seq_len),
        grid=(G, T // tp),
        in_specs=[pl.BlockSpec((tp, C), lambda g, i: (i, g)),
                  pl.BlockSpec((POOL_HALO, C), lambda g, i: (jnp.maximum(i * hb - 1, 0), g)),
                  pl.BlockSpec((None, C, Do), lambda g, i: (g, 0, 0)),
                  pl.BlockSpec((1, Do), lambda g, i: (0, g)),
                  pl.BlockSpec((tp, Do), lambda g, i, o=gate0_col // Do: (i, g + o))],
        out_specs=pl.BlockSpec((tp, Do), lambda g, i: (i, g)),
        out_shape=jax.ShapeDtypeStruct((T, G * Do), F32),
        scratch_shapes=[pltpu.VMEM((tp + POOL_HALO, C), F32), pltpu.VMEM((tp, C), BF16)],
        compiler_params=_params("parallel", "parallel"),
    )(u, u, w_pool, scale.reshape(1, G * Do), gates)


def _fox_kernel(q_ref, k_ref, v_ref, cq_ref, ck_ref, o_ref, *, scale, dh):
    hg = pl.program_id(1)
    qi = pl.program_id(2)
    tq = q_ref.shape[0]
    n_heads = q_ref.shape[1] // dh
    lane = lax.broadcasted_iota(jnp.int32, cq_ref.shape, 1)
    cq_all = cq_ref[...]
    q = [q_ref[:, n * dh:(n + 1) * dh] for n in range(n_heads)]
    cq = [jnp.sum(jnp.where(lane == hg * n_heads + n, cq_all, 0.0), axis=1, keepdims=True) for n in range(n_heads)]

    def scores(n, j):
        rows = pl.ds(pl.multiple_of(j * tq, tq), tq)
        s = lax.dot_general(q[n], k_ref[rows, n * dh:(n + 1) * dh], (((1,), (1,)), ((), ())),
                            preferred_element_type=F32) * scale
        return s + (cq[n] - ck_ref[n:n + 1, rows])

    def update(n, j, s, carry):
        m, l, acc = carry
        m_new = jnp.maximum(m, jnp.max(s, axis=1, keepdims=True))
        alpha = jnp.exp(m - m_new)
        p = jnp.exp(s - m_new)
        v = v_ref[pl.ds(pl.multiple_of(j * tq, tq), tq), n * dh:(n + 1) * dh]
        acc = alpha * acc + jnp.dot(p.astype(BF16), v, preferred_element_type=F32)
        return m_new, alpha * l + jnp.sum(p, axis=1, keepdims=True), acc

    init = (jnp.full((tq, 1), NEG_BIG, F32), jnp.zeros((tq, 1), F32), jnp.zeros((tq, dh), F32))
    carries = lax.fori_loop(
        0, qi, lambda j, cs: tuple(update(n, j, scores(n, j), cs[n]) for n in range(n_heads)), (init,) * n_heads)
    row = lax.broadcasted_iota(jnp.int32, (tq, tq), 0)
    col = lax.broadcasted_iota(jnp.int32, (tq, tq), 1)
    for n in range(n_heads):
        s = jnp.where(col <= row, scores(n, qi), NEG_BIG)
        m, l, acc = update(n, qi, s, carries[n])
        o_ref[:, n * dh:(n + 1) * dh] = (acc / l).astype(o_ref.dtype)


def _fox_attention(qkv, c, c_t, B, S, H, dh, *, tq=512, heads_per_step=2):
    T = B * S
    tq = _tile(S, tq)
    nq = S // tq
    hp = heads_per_step if H % heads_per_step == 0 else 1
    ng = H // hp
    w = hp * dh
    return pl.pallas_call(
        functools.partial(_fox_kernel, scale=dh ** -0.5, dh=dh),
        grid=(B, ng, nq),
        in_specs=[pl.BlockSpec((tq, w), lambda b, h, i: (b * nq + i, h)),
                  pl.BlockSpec((S, w), lambda b, h, i: (b, ng + h)),
                  pl.BlockSpec((S, w), lambda b, h, i: (b, 2 * ng + h)),
                  pl.BlockSpec((None, tq, H), lambda b, h, i: (b, i, 0)),
                  pl.BlockSpec((None, None, hp, S), lambda b, h, i: (b, h, 0, 0))],
        out_specs=pl.BlockSpec((tq, w), lambda b, h, i: (b * nq + i, h)),
        out_shape=jax.ShapeDtypeStruct((T, H * dh), BF16),
        compiler_params=_params("parallel", "parallel", "parallel"),
    )(qkv, qkv, qkv, c, c_t.reshape(B, ng, hp, S))


def _mem_attn_kernel(q_ref, kv_ref, o_ref, *, heads):
    width = q_ref.shape[1]
    dh = width // heads
    scale = dh ** -0.5
    for hd in range(heads):
        q = q_ref[:, hd * dh:(hd + 1) * dh]
        k = kv_ref[:, hd * dh:(hd + 1) * dh]
        v = kv_ref[:, width + hd * dh:width + (hd + 1) * dh]
        s = lax.dot_general(q, k, (((1,), (1,)), ((), ())), preferred_element_type=F32) * scale
        p = jnp.exp(s - jnp.max(s, axis=1, keepdims=True))
        p = p / jnp.sum(p, axis=1, keepdims=True)
        o_ref[:, hd * dh:(hd + 1) * dh] = jnp.dot(p.astype(BF16), v, preferred_element_type=F32).astype(o_ref.dtype)


def _mem_attention(q, kv, B, S, n_mem, *, tq=512):
    T, width = q.shape
    tq = _tile(S, tq)
    nq = S // tq
    return pl.pallas_call(
        functools.partial(_mem_attn_kernel, heads=MEM_HEADS),
        grid=(B, nq),
        in_specs=[pl.BlockSpec((tq, width), lambda b, i: (b * nq + i, 0)),
                  pl.BlockSpec((n_mem, 2 * width), lambda b, i: (b, 0))],
        out_specs=pl.BlockSpec((tq, width), lambda b, i: (b * nq + i, 0)),
        out_shape=jax.ShapeDtypeStruct((T, width), BF16),
        compiler_params=_params("parallel", "parallel"),
    )(q, kv)


def _router_kernel(x_ref, g_ref, whi_ref, wlo_ref, b_ref, oi_ref, og_ref, cnt_ref, carry_ref, *, n_experts):
    i = pl.program_id(0)
    tm = x_ref.shape[0]

    @pl.when(i == 0)
    def _():
        carry_ref[...] = jnp.zeros_like(carry_ref)

    h = _rms(x_ref[...], g_ref[...])
    h_hi = h.astype(BF16)
    h_lo = (h - h_hi.astype(F32)).astype(BF16)
    logits = (jnp.dot(h_hi, whi_ref[...], preferred_element_type=F32)
              + jnp.dot(h_hi, wlo_ref[...], preferred_element_type=F32)
              + jnp.dot(h_lo, whi_ref[...], preferred_element_type=F32)) + b_ref[...]
    lane = lax.broadcasted_iota(jnp.int32, (tm, LANES), 1).astype(F32)
    work = jnp.where(lane < n_experts, logits, -jnp.inf)
    vals, idxs = [], []
    for _ in range(TOP_K):
        m = jnp.max(work, axis=1, keepdims=True)
        idx = jnp.min(jnp.where(work == m, lane, float(LANES)), axis=1, keepdims=True)
        vals.append(m)
        idxs.append(idx)
        work = jnp.where(lane == idx, -jnp.inf, work)
    exps = [jnp.exp(v - vals[0]) for v in vals]
    denom = exps[0] + exps[1] + exps[2] + exps[3]
    onehots = [(lane == idx).astype(F32) for idx in idxs]
    chosen = onehots[0] + onehots[1] + onehots[2] + onehots[3]
    row = lax.broadcasted_iota(jnp.int32, (tm, tm), 0)
    col = lax.broadcasted_iota(jnp.int32, (tm, tm), 1)
    before = jnp.dot((col < row).astype(BF16), chosen.astype(BF16), preferred_element_type=F32) + carry_ref[...]
    out_i = jnp.zeros((tm, LANES), F32)
    out_g = jnp.zeros((tm, LANES), F32)
    for k in range(TOP_K):
        rank = jnp.sum(onehots[k] * before, axis=1, keepdims=True)
        out_i = jnp.where(lane == k, idxs[k], out_i)
        out_i = jnp.where(lane == TOP_K + k, rank, out_i)
        out_g = jnp.where(lane == k, exps[k] / denom, out_g)
    oi_ref[...] = out_i.astype(jnp.int32)
    og_ref[...] = out_g
    carry_ref[...] = carry_ref[...] + jnp.sum(chosen, axis=0, keepdims=True)
    cnt_ref[...] = carry_ref[...]


def _router(x, g, w_router, b_router, *, tm=512):
    T, D = x.shape
    E = w_router.shape[1]
    tm = _tile(T, tm)
    w_pad = jnp.zeros((D, LANES), F32).at[:, :E].set(w_router)
    w_hi = w_pad.astype(BF16)
    w_lo = (w_pad - w_hi.astype(F32)).astype(BF16)
    b_pad = jnp.zeros((1, LANES), F32).at[0, :E].set(b_router)
    full = lambda i: (0, 0)
    return pl.pallas_call(
        functools.partial(_router_kernel, n_experts=E),
        grid=(T // tm,),
        in_specs=[pl.BlockSpec((tm, D), lambda i: (i, 0)), pl.BlockSpec((1, D), full),
                  pl.BlockSpec((D, LANES), full), pl.BlockSpec((D, LANES), full), pl.BlockSpec((1, LANES), full)],
        out_specs=[pl.BlockSpec((tm, LANES), lambda i: (i, 0)), pl.BlockSpec((tm, LANES), lambda i: (i, 0)),
                   pl.BlockSpec((1, LANES), full)],
        out_shape=[jax.ShapeDtypeStruct((T, LANES), jnp.int32), jax.ShapeDtypeStruct((T, LANES), F32),
                   jax.ShapeDtypeStruct((1, LANES), F32)],
        scratch_shapes=[pltpu.VMEM((1, LANES), F32)],
        compiler_params=_params("arbitrary"),
    )(x, g.reshape(1, D), w_hi, w_lo, b_pad)


def _row_copy(src_hbm, row, buf, slot, r, sem):
    return pltpu.make_async_copy(src_hbm.at[pl.ds(row, 1)], buf.at[slot, pl.ds(r, 1)], sem.at[slot])


def _start_rows(idx_ref, src_hbm, buf, slot, sem, n_rows):
    def body(r, _):
        _row_copy(src_hbm, idx_ref[0, 0, r], buf, slot, r, sem).start()
        return 0
    lax.fori_loop(0, n_rows, body, 0, unroll=8)


def _wait_rows(src_hbm, buf, slot, sem, n_rows):
    pltpu.make_async_copy(src_hbm.at[pl.ds(0, n_rows)], buf.at[slot], sem.at[slot]).wait()


def _gather_pipeline(idx_ref, nxt_ref, src_hbm, buf, sem, n_rows):
    i = pl.program_id(0)
    n = pl.num_programs(0)
    slot = i % 2

    @pl.when(i == 0)
    def _():
        _start_rows(idx_ref, src_hbm, buf, 0, sem, n_rows)

    @pl.when(i + 1 < n)
    def _():
        _start_rows(nxt_ref, src_hbm, buf, 1 - slot, sem, n_rows)

    _wait_rows(src_hbm, buf, slot, sem, n_rows)
    return slot


def _pack_bf16_pairs(h):
    half = h.shape[1] // 2
    bits = pltpu.bitcast(h.astype(BF16).astype(F32), jnp.uint32)
    return (bits[:, :half] >> 16) | bits[:, half:]


def _unpack_bf16_pairs(u):
    lo = pltpu.bitcast(u << 16, F32).astype(BF16)
    hi = pltpu.bitcast(u & jnp.uint32(0xFFFF0000), F32).astype(BF16)
    return lo, hi


def _scatter_norm_kernel(pe_ref, pd_ref, dest_ref, x_ref, g_ref, xs_hbm, buf, zbuf, sem, zsem, *, tm, n_experts):
    i = pl.program_id(0)
    n = pl.num_programs(0)
    tb = x_ref.shape[0]
    slot = i % 2

    def row_copy(s, t, row):
        return pltpu.make_async_copy(buf.at[s, pl.ds(t, 1)], xs_hbm.at[pl.ds(row, 1)], sem.at[s])

    def wait_slot(s):
        for _ in range(TOP_K):
            pltpu.make_async_copy(buf.at[s], xs_hbm.at[pl.ds(0, tb)], sem.at[s]).wait()

    def zero_group(first_row):
        return pltpu.make_async_copy(zbuf, xs_hbm.at[pl.ds(pl.multiple_of(first_row, tm), tm)], zsem)

    @pl.when(i == 0)
    def _():
        zbuf[...] = jnp.zeros_like(zbuf)
        n_groups = xs_hbm.shape[0] // tm
        used = pe_ref[n_experts - 1] // tm
        for e in range(n_experts):
            @pl.when(pd_ref[e] > 0)
            def _(e=e):
                zero_group(pe_ref[e] - tm).start()
        lax.fori_loop(used, n_groups, lambda gi, c: (zero_group(gi * tm).start(), c)[1], 0)
        for e in range(n_experts):
            @pl.when(pd_ref[e] > 0)
            def _(e=e):
                zero_group(pe_ref[e] - tm).wait()
        lax.fori_loop(used, n_groups, lambda gi, c: (zero_group(gi * tm).wait(), c)[1], 0)

    @pl.when(i >= 2)
    def _():
        wait_slot(slot)

    buf[slot] = _pack_bf16_pairs(_rms(x_ref[...], g_ref[...]))

    def body(t, _):
        for k in range(TOP_K):
            row_copy(slot, t, dest_ref[0, 0, t * TOP_K + k]).start()
        return 0
    lax.fori_loop(0, tb, body, 0, unroll=4)

    @pl.when(i == n - 1)
    def _():
        @pl.when(n >= 2)
        def _():
            wait_slot(1 - slot)
        wait_slot(slot)


def _scatter_norm(x, g, dest, pad_ends, padded, n_rows, *, tm, tb=128):
    T, D = x.shape
    E = pad_ends.shape[0]
    tb = _tile(T, tb)
    nb = T // tb
    grid_spec = pltpu.PrefetchScalarGridSpec(
        num_scalar_prefetch=2, grid=(nb,),
        in_specs=[pl.BlockSpec((1, 1, TOP_K * tb), lambda i, pe, pd: (i, 0, 0), memory_space=pltpu.SMEM),
                  pl.BlockSpec((tb, D), lambda i, pe, pd: (i, 0)),
                  pl.BlockSpec((1, D), lambda i, pe, pd: (0, 0))],
        out_specs=pl.BlockSpec(memory_space=pl.ANY),
        scratch_shapes=[pltpu.VMEM((2, tb, D // 2), jnp.uint32), pltpu.VMEM((tm, D // 2), jnp.uint32),
                        pltpu.SemaphoreType.DMA((2,)), pltpu.SemaphoreType.DMA(())])
    return pl.pallas_call(
        functools.partial(_scatter_norm_kernel, tm=tm, n_experts=E), grid_spec=grid_spec,
        out_shape=jax.ShapeDtypeStruct((n_rows, D // 2), jnp.uint32),
        compiler_params=_params("arbitrary"),
    )(pad_ends, padded, dest.reshape(nb, 1, TOP_K * tb), x, g.reshape(1, D))


def _combine_kernel(idx_ref, nxt_ref, x_ref, gate_ref, g_ref, rows_hbm, o_ref, buf, sem, *, final_norm):
    tb = x_ref.shape[0]
    slot = _gather_pipeline(idx_ref, nxt_ref, rows_hbm, buf, sem, TOP_K * tb)
    y = x_ref[...]
    gates = gate_ref[...]
    for k in range(TOP_K):
        y = y + gates[:, k:k + 1] * buf[slot, pl.ds(k * tb, tb), :]
    o_ref[...] = _rms(y, g_ref[...]) if final_norm else y


def _combine(x, rows, dest, gates_slab, g, final_norm, *, tb=64):
    T, D = x.shape
    tb = _tile(T, tb)
    nb = T // tb
    idx3 = dest.reshape(nb, tb, TOP_K).transpose(0, 2, 1).reshape(nb, 1, TOP_K * tb)
    return pl.pallas_call(
        functools.partial(_combine_kernel, final_norm=final_norm),
        grid=(nb,),
        in_specs=[pl.BlockSpec((1, 1, TOP_K * tb), lambda i: (i, 0, 0), memory_space=pltpu.SMEM),
                  pl.BlockSpec((1, 1, TOP_K * tb), lambda i: (jnp.minimum(i + 1, nb - 1), 0, 0), memory_space=pltpu.SMEM),
                  pl.BlockSpec((tb, D), lambda i: (i, 0)),
                  pl.BlockSpec((tb, LANES), lambda i: (i, 0)),
                  pl.BlockSpec((1, D), lambda i: (0, 0)),
                  pl.BlockSpec(memory_space=pl.ANY)],
        out_specs=pl.BlockSpec((tb, D), lambda i: (i, 0)),
        out_shape=jax.ShapeDtypeStruct((T, D), F32),
        scratch_shapes=[pltpu.VMEM((2, TOP_K * tb, D), F32), pltpu.SemaphoreType.DMA((2,))],
        compiler_params=_params("arbitrary"),
    )(idx3, idx3, x, gates_slab, g.reshape(1, D), rows)


def _new_expert(be_ref, i):
    return jnp.logical_or(i == 0, be_ref[i] != be_ref[jnp.maximum(i - 1, 0)])


def _moe_up_kernel(be_ref, nu_ref, xs_ref, w_ref, b_ref, o_ref, wb_ref):
    i = pl.program_id(1)
    half = xs_ref.shape[1]
    pair = 2 * LANES

    @pl.when(_new_expert(be_ref, i))
    def _():
        wb_ref[...] = w_ref[...].astype(BF16)

    @pl.when(i < nu_ref[0])
    def _():
        lo, hi = _unpack_bf16_pairs(xs_ref[...])
        row = lax.broadcasted_iota(jnp.int32, (pair, LANES), 0)
        col = lax.broadcasted_iota(jnp.int32, (pair, LANES), 1)
        pick_even = (row == 2 * col).astype(BF16)
        even = lax.broadcasted_iota(jnp.int32, (1, pair), 1) % 2 == 0
        tn = wb_ref.shape[1]
        wide = min(tn, 2 * pair)
        for c in range(tn // pair):
            if (c * pair) % wide == 0:
                cols = slice(c * pair, c * pair + wide)
                gu = (jnp.dot(lo, wb_ref[:half, cols], preferred_element_type=F32)
                      + jnp.dot(hi, wb_ref[half:, cols], preferred_element_type=F32)) + b_ref[:, cols]
            blk = gu[:, (c * pair) % wide:(c * pair) % wide + pair]
            nxt = pltpu.roll(blk, pair - 1, axis=1)
            gate = jnp.minimum(blk, SWIGLU_LIMIT)
            up = jnp.clip(nxt, -SWIGLU_LIMIT, SWIGLU_LIMIT)
            act = (up + 1.0) * gate * jax.nn.sigmoid(SWIGLU_ALPHA * gate)
            act = jnp.where(even, act, 0.0).astype(BF16)
            o_ref[:, c * LANES:(c + 1) * LANES] = jnp.dot(act, pick_even, preferred_element_type=F32).astype(o_ref.dtype)

    @pl.when(i >= nu_ref[0])
    def _():
        o_ref[...] = jnp.zeros_like(o_ref)


def _moe_up(xs, w_gu, b_gu, block_e, n_used, *, tm, tn=1024):
    R, half = xs.shape
    E, D, F2 = w_gu.shape
    tn = _tile(F2, tn)
    assert tn % (2 * LANES) == 0 and D == 2 * half
    nb = R // tm
    grid_spec = pltpu.PrefetchScalarGridSpec(
        num_scalar_prefetch=2, grid=(F2 // tn, nb),
        in_specs=[pl.BlockSpec((tm, half), lambda j, i, be, nu: (jnp.minimum(i, nu[0] - 1), 0)),
                  pl.BlockSpec((None, D, tn), lambda j, i, be, nu: (be[i], 0, j)),
                  pl.BlockSpec((None, 1, tn), lambda j, i, be, nu: (be[i], 0, j))],
        out_specs=pl.BlockSpec((tm, tn // 2), lambda j, i, be, nu: (i, j)),
        scratch_shapes=[pltpu.VMEM((D, tn), BF16)])
    return pl.pallas_call(
        _moe_up_kernel, grid_spec=grid_spec,
        out_shape=jax.ShapeDtypeStruct((R, F2 // 2), BF16),
        compiler_params=_params("parallel", "arbitrary"),
    )(block_e, n_used, xs, w_gu, b_gu.reshape(E, 1, F2))


def _moe_down_kernel(be_ref, nu_ref, a_ref, w_ref, b_ref, o_ref, wb_ref):
    i = pl.program_id(1)

    @pl.when(_new_expert(be_ref, i))
    def _():
        wb_ref[...] = w_ref[...].astype(BF16)

    @pl.when(i < nu_ref[0])
    def _():
        o_ref[...] = jnp.dot(a_ref[...], wb_ref[...], preferred_element_type=F32) + b_ref[...]

    @pl.when(i >= nu_ref[0])
    def _():
        o_ref[...] = jnp.zeros_like(o_ref)


def _moe_down(act, w_d, b_d, block_e, n_used, *, tm, tn=2048):
    R, F = act.shape
    E, _, D = w_d.shape
    tn = _tile(D, tn)
    nb = R // tm
    grid_spec = pltpu.PrefetchScalarGridSpec(
        num_scalar_prefetch=2, grid=(D // tn, nb),
        in_specs=[pl.BlockSpec((tm, F), lambda j, i, be, nu: (i, 0)),
                  pl.BlockSpec((None, F, tn), lambda j, i, be, nu: (be[i], 0, j)),
                  pl.BlockSpec((None, 1, tn), lambda j, i, be, nu: (be[i], 0, j))],
        out_specs=pl.BlockSpec((tm, tn), lambda j, i, be, nu: (i, j)),
        scratch_shapes=[pltpu.VMEM((F, tn), BF16)])
    return pl.pallas_call(
        _moe_down_kernel, grid_spec=grid_spec,
        out_shape=jax.ShapeDtypeStruct((R, D), F32),
        compiler_params=_params("parallel", "arbitrary"),
    )(block_e, n_used, act, w_d, b_d.reshape(E, 1, D))


def _moe(x, g_moe, w_router, b_router, w_gate_up, b_gate_up, w_down, b_down, g_final, final_norm, *, tm=512):
    T, D = x.shape
    E = w_router.shape[1]
    A = T * TOP_K
    tm = min(tm, A)
    slab_i, slab_g, cnt = _router(x, g_moe, w_router, b_router)
    idx = slab_i[:, :TOP_K]
    rank = slab_i[:, TOP_K:2 * TOP_K]
    counts = cnt[0, :E].astype(jnp.int32)
    padded = (counts + tm - 1) // tm * tm
    pad_ends = jnp.cumsum(padded)
    pad_starts = pad_ends - padded
    dest = pad_starts[idx] + rank
    R = (A + tm - 1) // tm * tm + E * tm
    nb = R // tm
    block_start = jnp.arange(nb, dtype=jnp.int32) * tm
    block_e = jnp.minimum(jnp.sum(block_start[:, None] >= pad_ends[None, :], axis=1), E - 1).astype(jnp.int32)
    n_used = (pad_ends[-1:] // tm).astype(jnp.int32)

    xs = _scatter_norm(x, g_moe, dest, pad_ends, padded, R, tm=tm)
    act = _moe_up(xs, w_gate_up, b_gate_up, block_e, n_used, tm=tm)
    rows = _moe_down(act, w_down, b_down, block_e, n_used, tm=tm)
    return _combine(x, rows, dest, slab_g, g_final, final_norm)


def kernel(x, mem, g_mix, w_in, b_in, w_pool, pool_scale, w_fox_o, w_out, g_mem_q, g_mem_kv, w_mem_q, w_mem_kv,
           w_mem_o, g_moe, w_router, b_router, w_gate_up, b_gate_up, w_down, b_down, g_final):
    B, S, D = x.shape
    T = B * S
    n_mem = mem.shape[1]
    depth, G, C, Do = w_pool.shape
    pool_w = G * C
    fox_w = w_fox_o.shape[1]
    H = w_in.shape[2] - pool_w - 3 * fox_w - 2 * D
    dh = fox_w // H
    off_q, off_f = pool_w, pool_w + 3 * fox_w
    off_gate = off_f + H
    xt = x.reshape(T, D)
    mt = mem.reshape(B * n_mem, D)
    for l in range(depth):
        wl, bl = w_in[l], b_in[l]
        h, lf_t = _norm(xt, g_mix[l], wl[:, off_f:off_gate].T.astype(BF16), bl[off_f:off_gate])
        u = _proj(h, wl, bl, col_off=0, n_cols=off_q, out_dtype=F32)
        qkv = _proj(h, wl, bl, col_off=off_q, n_cols=off_f - off_q, out_dtype=BF16)
        gates = _proj(h, wl[:, off_gate:], bl[off_gate:], mode="sigmoid", out_dtype=BF16)
        c_t = _forget_cumsum(lf_t, B, S)
        att = _fox_attention(qkv, c_t.transpose(0, 2, 1), c_t, B, S, H, dh)
        pp = _pool_mixer(u, w_pool[l], pool_scale[l], gates, 0, S)
        merged = _proj(att, w_fox_o[l], None, (pp, gates), (0, D), mode="merge", out_dtype=BF16)
        xt = _proj(merged, w_out[l], None, (xt,), (0,), mode="residual", out_dtype=F32)
        kv = _proj(_norm(mt, g_mem_kv[l]), w_mem_kv[l], out_dtype=BF16)
        qm = _proj(_norm(xt, g_mem_q[l]), w_mem_q[l], out_dtype=BF16)
        om = _mem_attention(qm, kv, B, S, n_mem)
        xt = _proj(om, w_mem_o[l], None, (xt,), (0,), mode="residual", out_dtype=F32)
        xt = _moe(xt, g_moe[l], w_router[l], b_router[l], w_gate_up[l], b_gate_up[l], w_down[l], b_down[l],
                  g_final, final_norm=l == depth - 1)
    return xt.reshape(B, S, D)
```

```python
import functools

import jax
import jax.numpy as jnp
from jax import lax
from jax.experimental import pallas as pl
from jax.experimental.pallas import tpu as pltpu

F32 = jnp.float32
BF16 = jnp.bfloat16

EPS = 1e-5
POOL_WINDOWS = (2, 4, 8, 16)
POOL_HALO = 16
MEM_HEADS = 4
TOP_K = 4
SWIGLU_LIMIT = 7.0
SWIGLU_ALPHA = 1.702
NEG_BIG = -1e30

LANES = 128
VMEM_LIMIT_BYTES = 56 * 1024 * 1024


def _tile(dim, pref):
    t = pref
    while t >= 8:
        if dim % t == 0:
            return t
        t //= 2
    return dim


def _params(*sem):
    return pltpu.CompilerParams(dimension_semantics=sem, vmem_limit_bytes=VMEM_LIMIT_BYTES)


def _rms(x, g):
    ms = jnp.mean(x * x, axis=-1, keepdims=True)
    return x * lax.rsqrt(ms + EPS) * g


def _split3(x):
    hi = x.astype(BF16)
    r1 = x - hi.astype(F32)
    mid = r1.astype(BF16)
    lo = (r1 - mid.astype(F32)).astype(BF16)
    return hi, mid, lo


def _norm_kernel(x_ref, g_ref, *refs, with_f):
    h = _rms(x_ref[...], g_ref[...]).astype(BF16)
    if with_f:
        wf_ref, bf_ref, h_ref, lf_ref = refs
        f = lax.dot_general(wf_ref[...], h, (((1,), (1,)), ((), ())), preferred_element_type=F32) + bf_ref[...]
        lf_ref[...] = jnp.minimum(f, 0.0) - jnp.log1p(jnp.exp(-jnp.abs(f)))
    else:
        h_ref, = refs
    h_ref[...] = h


def _norm(x, g, wf_t=None, bf_t=None, *, tm=512):
    M, K = x.shape
    tm = _tile(M, tm)
    with_f = wf_t is not None
    in_specs = [pl.BlockSpec((tm, K), lambda i: (i, 0)), pl.BlockSpec((1, K), lambda i: (0, 0))]
    args = [x, g.reshape(1, K)]
    out_shape = [jax.ShapeDtypeStruct((M, K), BF16)]
    out_specs = [pl.BlockSpec((tm, K), lambda i: (i, 0))]
    if with_f:
        H = wf_t.shape[0]
        in_specs += [pl.BlockSpec((H, K), lambda i: (0, 0)), pl.BlockSpec((H, 1), lambda i: (0, 0))]
        args += [wf_t, bf_t.reshape(H, 1)]
        out_shape.append(jax.ShapeDtypeStruct((H, M), F32))
        out_specs.append(pl.BlockSpec((H, tm), lambda i: (0, i)))
    res = pl.pallas_call(
        functools.partial(_norm_kernel, with_f=with_f),
        grid=(M // tm,),
        in_specs=in_specs, out_specs=out_specs, out_shape=out_shape,
        compiler_params=_params("parallel"),
    )(*args)
    return res if with_f else res[0]


def _lane_window(main_ref, tail_ref, shift):
    if shift == 0:
        return main_ref[...].astype(F32)
    g = jnp.concatenate([main_ref[...], tail_ref[...]], axis=1).astype(F32)
    return pltpu.roll(g, g.shape[1] - shift, axis=1)[:, :main_ref.shape[1]]


def _window_specs(first_col, rows, width, row_of, col_block_of):
    base, shift = first_col // LANES * LANES, first_col % LANES
    assert base % width == 0
    specs = [pl.BlockSpec((rows, width), lambda *g: (row_of(*g), col_block_of(*g) + base // width))]
    if shift:
        specs.append(pl.BlockSpec(
            (rows, LANES), lambda *g: (row_of(*g), (base + (col_block_of(*g) + 1) * width) // LANES)))
    return specs, shift


def _proj_kernel(x_ref, w_ref, *refs, mode, has_bias, gate_shift):
    refs = list(refs)
    wb_ref = refs.pop()
    o_ref = refs.pop()
    b_ref = refs.pop(0) if has_bias else None

    @pl.when(pl.program_id(1) == 0)
    def _():
        wb_ref[...] = w_ref[...].astype(BF16)

    acc = jnp.dot(x_ref[...], wb_ref[...], preferred_element_type=F32)
    if has_bias:
        acc = acc + b_ref[...]
    if mode == "sigmoid":
        acc = jax.nn.sigmoid(acc)
    elif mode == "merge":
        pp_ref, g_ref = refs[0], refs[1]
        acc = pp_ref[...] + _lane_window(g_ref, refs[2] if gate_shift else None, gate_shift) * acc
    elif mode == "residual":
        acc = refs[0][...] + acc
    o_ref[...] = acc.astype(o_ref.dtype)


def _proj(x, w, b=None, extra=None, gate=None, *, col_off=0, n_cols=None, mode="plain", out_dtype,
          tm=1024, tn=512):
    M, K = x.shape
    N = w.shape[1] - col_off if n_cols is None else n_cols
    tm = _tile(M, tm)
    tn = min(tn, N)
    while col_off % tn:
        tn //= 2
    assert tn % LANES == 0
    in_specs = [pl.BlockSpec((tm, K), lambda j, i: (i, 0)),
                pl.BlockSpec((K, tn), lambda j, i, o=col_off // tn: (0, j + o))]
    args = [x, w]
    if b is not None:
        in_specs.append(pl.BlockSpec((1, tn), lambda j, i, o=col_off // tn: (0, j + o)))
        args.append(b.reshape(1, -1))
    if extra is not None:
        in_specs.append(pl.BlockSpec((tm, tn), lambda j, i: (i, j)))
        args.append(extra)
    gate_shift = 0
    if gate is not None:
        specs, gate_shift = _window_specs(gate[1], tm, tn, lambda j, i: i, lambda j, i: j)
        in_specs += specs
        args += [gate[0]] * len(specs)
    return pl.pallas_call(
        functools.partial(_proj_kernel, mode=mode, has_bias=b is not None, gate_shift=gate_shift),
        grid=(pl.cdiv(N, tn), M // tm),
        in_specs=in_specs,
        out_specs=pl.BlockSpec((tm, tn), lambda j, i: (i, j)),
        out_shape=jax.ShapeDtypeStruct((M, N), out_dtype),
        scratch_shapes=[pltpu.VMEM((K, tn), BF16)],
        compiler_params=_params("parallel", "arbitrary"),
    )(*args)


def _cumsum_kernel(lf_ref, c_ref):
    S = lf_ref.shape[1]
    row = lax.broadcasted_iota(jnp.int32, (S, S), 0)
    col = lax.broadcasted_iota(jnp.int32, (S, S), 1)
    upper = (row <= col).astype(BF16)
    c = jnp.zeros(lf_ref.shape, F32)
    for part in _split3(lf_ref[...]):
        c = c + jnp.dot(part, upper, preferred_element_type=F32)
    c_ref[...] = c


def _forget_cumsum(lf_t, B, S):
    H = lf_t.shape[0]
    return pl.pallas_call(
        _cumsum_kernel,
        grid=(B,),
        in_specs=[pl.BlockSpec((H, S), lambda b: (0, b))],
        out_specs=pl.BlockSpec((None, H, S), lambda b: (b, 0, 0)),
        out_shape=jax.ShapeDtypeStruct((B, H, S), F32),
        compiler_params=_params("parallel"),
    )(lf_t)


def _pool_kernel(u_ref, halo_ref, w_ref, sc_ref, *refs, seq_len, gate_shift):
    g0_ref, g0_tail_ref = (refs[0], refs[1]) if gate_shift else (refs[0], None)
    o_ref, ext_ref, pooled_ref = refs[-3:]
    g = pl.program_id(0)
    i = pl.program_id(1)
    tp = u_ref.shape[0]
    pos0 = (i * tp) % seq_len
    u = u_ref[...]
    ext_ref[pl.ds(POOL_HALO, tp), :] = u
    ext_ref[pl.ds(0, POOL_HALO), :] = jnp.where(pos0 == 0, 0.0, halo_ref[...])
    pos = pos0 + lax.broadcasted_iota(jnp.int32, (tp, 1), 0)
    for gi, win in enumerate(POOL_WINDOWS):
        @pl.when(g == gi)
        def _(win=win):
            acc = u
            for k in range(1, win):
                acc = acc + ext_ref[pl.ds(POOL_HALO - k, tp), :]
            cnt = jnp.minimum(pos + 1, win).astype(F32)
            pooled_ref[...] = (acc / cnt - u).astype(BF16)
    y = jnp.dot(pooled_ref[...], w_ref[...].astype(BF16), preferred_element_type=F32)
    o_ref[...] = _lane_window(g0_ref, g0_tail_ref, gate_shift) * (y * sc_ref[...])


def _pool_mixer(u, w_pool, scale, gates, gate0_col, seq_len, *, tp=512):
    T = u.shape[0]
    G, C, Do = w_pool.shape
    tp = _tile(seq_len, tp)
    assert tp % POOL_HALO == 0
    hb = tp // POOL_HALO
    gate_specs, gate_shift = _window_specs(gate0_col, tp, Do, lambda g, i: i, lambda g, i: g)
    return pl.pallas_call(
        functools.partial(_pool_kernel, seq_len=seq_len, gate_shift=gate_shift),
        grid=(G, T // tp),
        in_specs=[pl.BlockSpec((tp, C), lambda g, i: (i, g)),
                  pl.BlockSpec((POOL_HALO, C), lambda g, i: (jnp.maximum(i * hb - 1, 0), g)),
                  pl.BlockSpec((None, C, Do), lambda g, i: (g, 0, 0)),
                  pl.BlockSpec((1, Do), lambda g, i: (0, g))] + gate_specs,
        out_specs=pl.BlockSpec((tp, Do), lambda g, i: (i, g)),
        out_shape=jax.ShapeDtypeStruct((T, G * Do), F32),
        scratch_shapes=[pltpu.VMEM((tp + POOL_HALO, C), F32), pltpu.VMEM((tp, C), BF16)],
        compiler_params=_params("parallel", "parallel"),
    )(u, u, w_pool, scale.reshape(1, G * Do), *([gates] * len(gate_specs)))


def _fox_kernel(q_ref, k_ref, v_ref, cq_ref, ck_ref, o_ref, *, scale, dh):
    hg = pl.program_id(1)
    qi = pl.program_id(2)
    tq = q_ref.shape[0]
    n_heads = q_ref.shape[1] // dh
    lane = lax.broadcasted_iota(jnp.int32, cq_ref.shape, 1)
    cq_all = cq_ref[...]
    q = [q_ref[:, n * dh:(n + 1) * dh] for n in range(n_heads)]
    cq = [jnp.sum(jnp.where(lane == hg * n_heads + n, cq_all, 0.0), axis=1, keepdims=True) for n in range(n_heads)]

    def scores(n, j):
        rows = pl.ds(pl.multiple_of(j * tq, tq), tq)
        s = lax.dot_general(q[n], k_ref[rows, n * dh:(n + 1) * dh], (((1,), (1,)), ((), ())),
                            preferred_element_type=F32) * scale
        return s + (cq[n] - ck_ref[n:n + 1, rows])

    def update(n, j, s, carry):
        m, l, acc = carry
        m_new = jnp.maximum(m, jnp.max(s, axis=1, keepdims=True))
        alpha = jnp.exp(m - m_new)
        p = jnp.exp(s - m_new)
        v = v_ref[pl.ds(pl.multiple_of(j * tq, tq), tq), n * dh:(n + 1) * dh]
        acc = alpha * acc + jnp.dot(p.astype(BF16), v, preferred_element_type=F32)
        return m_new, alpha * l + jnp.sum(p, axis=1, keepdims=True), acc

    init = (jnp.full((tq, 1), NEG_BIG, F32), jnp.zeros((tq, 1), F32), jnp.zeros((tq, dh), F32))
    carries = lax.fori_loop(
        0, qi, lambda j, cs: tuple(update(n, j, scores(n, j), cs[n]) for n in range(n_heads)), (init,) * n_heads)
    row = lax.broadcasted_iota(jnp.int32, (tq, tq), 0)
    col = lax.broadcasted_iota(jnp.int32, (tq, tq), 1)
    for n in range(n_heads):
        s = jnp.where(col <= row, scores(n, qi), NEG_BIG)
        m, l, acc = update(n, qi, s, carries[n])
        o_ref[:, n * dh:(n + 1) * dh] = (acc / l).astype(o_ref.dtype)


def _fox_attention(qkv, c, c_t, B, S, H, dh, *, tq=512, heads_per_step=2):
    T = B * S
    tq = _tile(S, tq)
    nq = S // tq
    hp = heads_per_step if H % heads_per_step == 0 else 1
    ng = H // hp
    w = hp * dh
    return pl.pallas_call(
        functools.partial(_fox_kernel, scale=dh ** -0.5, dh=dh),
        grid=(B, ng, nq),
        in_specs=[pl.BlockSpec((tq, w), lambda b, h, i: (b * nq + i, h)),
                  pl.BlockSpec((S, w), lambda b, h, i: (b, ng + h)),
                  pl.BlockSpec((S, w), lambda b, h, i: (b, 2 * ng + h)),
                  pl.BlockSpec((None, tq, H), lambda b, h, i: (b, i, 0)),
                  pl.BlockSpec((None, None, hp, S), lambda b, h, i: (b, h, 0, 0))],
        out_specs=pl.BlockSpec((tq, w), lambda b, h, i: (b * nq + i, h)),
        out_shape=jax.ShapeDtypeStruct((T, H * dh), BF16),
        compiler_params=_params("parallel", "parallel", "parallel"),
    )(qkv, qkv, qkv, c, c_t.reshape(B, ng, hp, S))


def _mem_attn_kernel(q_ref, kv_ref, o_ref, *, heads):
    width = q_ref.shape[1]
    dh = width // heads
    scale = dh ** -0.5
    for hd in range(heads):
        q = q_ref[:, hd * dh:(hd + 1) * dh]
        k = kv_ref[:, hd * dh:(hd + 1) * dh]
        v = kv_ref[:, width + hd * dh:width + (hd + 1) * dh]
        s = lax.dot_general(q, k, (((1,), (1,)), ((), ())), preferred_element_type=F32) * scale
        p = jnp.exp(s - jnp.max(s, axis=1, keepdims=True))
        p = p / jnp.sum(p, axis=1, keepdims=True)
        o_ref[:, hd * dh:(hd + 1) * dh] = jnp.dot(p.astype(BF16), v, preferred_element_type=F32).astype(o_ref.dtype)


def _mem_attention(q, kv, B, S, n_mem, *, tq=512):
    T, width = q.shape
    tq = _tile(S, tq)
    nq = S // tq
    return pl.pallas_call(
        functools.partial(_mem_attn_kernel, heads=MEM_HEADS),
        grid=(B, nq),
        in_specs=[pl.BlockSpec((tq, width), lambda b, i: (b * nq + i, 0)),
                  pl.BlockSpec((n_mem, 2 * width), lambda b, i: (b, 0))],
        out_specs=pl.BlockSpec((tq, width), lambda b, i: (b * nq + i, 0)),
        out_shape=jax.ShapeDtypeStruct((T, width), BF16),
        compiler_params=_params("parallel", "parallel"),
    )(q, kv)


def _router_kernel(x_ref, g_ref, whi_ref, wlo_ref, b_ref, oi_ref, og_ref, cnt_ref, carry_ref, *, n_experts):
    i = pl.program_id(0)
    tm = x_ref.shape[0]

    @pl.when(i == 0)
    def _():
        carry_ref[...] = jnp.zeros_like(carry_ref)

    h = _rms(x_ref[...], g_ref[...])
    h_hi = h.astype(BF16)
    h_lo = (h - h_hi.astype(F32)).astype(BF16)
    logits = (jnp.dot(h_hi, whi_ref[...], preferred_element_type=F32)
              + jnp.dot(h_hi, wlo_ref[...], preferred_element_type=F32)
              + jnp.dot(h_lo, whi_ref[...], preferred_element_type=F32)) + b_ref[...]
    lane = lax.broadcasted_iota(jnp.int32, (tm, LANES), 1).astype(F32)
    work = jnp.where(lane < n_experts, logits, -jnp.inf)
    vals, idxs = [], []
    for _ in range(TOP_K):
        m = jnp.max(work, axis=1, keepdims=True)
        idx = jnp.min(jnp.where(work == m, lane, float(LANES)), axis=1, keepdims=True)
        vals.append(m)
        idxs.append(idx)
        work = jnp.where(lane == idx, -jnp.inf, work)
    exps = [jnp.exp(v - vals[0]) for v in vals]
    denom = exps[0] + exps[1] + exps[2] + exps[3]
    onehots = [(lane == idx).astype(F32) for idx in idxs]
    chosen = onehots[0] + onehots[1] + onehots[2] + onehots[3]
    row = lax.broadcasted_iota(jnp.int32, (tm, tm), 0)
    col = lax.broadcasted_iota(jnp.int32, (tm, tm), 1)
    before = jnp.dot((col < row).astype(BF16), chosen.astype(BF16), preferred_element_type=F32) + carry_ref[...]
    out_i = jnp.zeros((tm, LANES), F32)
    out_g = jnp.zeros((tm, LANES), F32)
    for k in range(TOP_K):
        rank = jnp.sum(onehots[k] * before, axis=1, keepdims=True)
        out_i = jnp.where(lane == k, idxs[k], out_i)
        out_i = jnp.where(lane == TOP_K + k, rank, out_i)
        out_g = jnp.where(lane == k, exps[k] / denom, out_g)
    oi_ref[...] = out_i.astype(jnp.int32)
    og_ref[...] = out_g
    carry_ref[...] = carry_ref[...] + jnp.sum(chosen, axis=0, keepdims=True)
    cnt_ref[...] = carry_ref[...]


def _router(x, g, w_router, b_router, *, tm=512):
    T, D = x.shape
    E = w_router.shape[1]
    tm = _tile(T, tm)
    w_pad = jnp.zeros((D, LANES), F32).at[:, :E].set(w_router)
    w_hi = w_pad.astype(BF16)
    w_lo = (w_pad - w_hi.astype(F32)).astype(BF16)
    b_pad = jnp.zeros((1, LANES), F32).at[0, :E].set(b_router)
    full = lambda i: (0, 0)
    return pl.pallas_call(
        functools.partial(_router_kernel, n_experts=E),
        grid=(T // tm,),
        in_specs=[pl.BlockSpec((tm, D), lambda i: (i, 0)), pl.BlockSpec((1, D), full),
                  pl.BlockSpec((D, LANES), full), pl.BlockSpec((D, LANES), full), pl.BlockSpec((1, LANES), full)],
        out_specs=[pl.BlockSpec((tm, LANES), lambda i: (i, 0)), pl.BlockSpec((tm, LANES), lambda i: (i, 0)),
                   pl.BlockSpec((1, LANES), full)],
        out_shape=[jax.ShapeDtypeStruct((T, LANES), jnp.int32), jax.ShapeDtypeStruct((T, LANES), F32),
                   jax.ShapeDtypeStruct((1, LANES), F32)],
        scratch_shapes=[pltpu.VMEM((1, LANES), F32)],
        compiler_params=_params("arbitrary"),
    )(x, g.reshape(1, D), w_hi, w_lo, b_pad)


def _row_copy(src_hbm, row, buf, slot, r, sem):
    return pltpu.make_async_copy(src_hbm.at[pl.ds(row, 1)], buf.at[slot, pl.ds(r, 1)], sem.at[slot])


def _start_rows(idx_ref, src_hbm, buf, slot, sem, n_rows):
    def body(r, _):
        _row_copy(src_hbm, idx_ref[0, 0, r], buf, slot, r, sem).start()
        return 0
    lax.fori_loop(0, n_rows, body, 0, unroll=8)


def _wait_rows(src_hbm, buf, slot, sem, n_rows):
    pltpu.make_async_copy(src_hbm.at[pl.ds(0, n_rows)], buf.at[slot], sem.at[slot]).wait()


def _gather_pipeline(idx_ref, nxt_ref, src_hbm, buf, sem, n_rows):
    i = pl.program_id(0)
    n = pl.num_programs(0)
    slot = i % 2

    @pl.when(i == 0)
    def _():
        _start_rows(idx_ref, src_hbm, buf, 0, sem, n_rows)

    @pl.when(i + 1 < n)
    def _():
        _start_rows(nxt_ref, src_hbm, buf, 1 - slot, sem, n_rows)

    _wait_rows(src_hbm, buf, slot, sem, n_rows)
    return slot


def _pack_bf16_pairs(h):
    half = h.shape[1] // 2
    bits = pltpu.bitcast(h.astype(BF16).astype(F32), jnp.uint32)
    return (bits[:, :half] >> 16) | bits[:, half:]


def _unpack_bf16_pairs(u):
    lo = pltpu.bitcast(u << 16, F32).astype(BF16)
    hi = pltpu.bitcast(u & jnp.uint32(0xFFFF0000), F32).astype(BF16)
    return lo, hi


def _scatter_norm_kernel(pe_ref, pd_ref, dest_ref, x_ref, g_ref, xs_hbm, buf, zbuf, sem, zsem, *, tm, n_experts):
    i = pl.program_id(0)
    n = pl.num_programs(0)
    tb = x_ref.shape[0]
    slot = i % 2

    def row_copy(s, t, row):
        return pltpu.make_async_copy(buf.at[s, pl.ds(t, 1)], xs_hbm.at[pl.ds(row, 1)], sem.at[s])

    def wait_slot(s):
        for _ in range(TOP_K):
            pltpu.make_async_copy(buf.at[s], xs_hbm.at[pl.ds(0, tb)], sem.at[s]).wait()

    def zero_group(first_row):
        return pltpu.make_async_copy(zbuf, xs_hbm.at[pl.ds(pl.multiple_of(first_row, tm), tm)], zsem)

    @pl.when(i == 0)
    def _():
        zbuf[...] = jnp.zeros_like(zbuf)
        n_groups = xs_hbm.shape[0] // tm
        used = pe_ref[n_experts - 1] // tm
        for e in range(n_experts):
            @pl.when(pd_ref[e] > 0)
            def _(e=e):
                zero_group(pe_ref[e] - tm).start()
        lax.fori_loop(used, n_groups, lambda gi, c: (zero_group(gi * tm).start(), c)[1], 0)
        for e in range(n_experts):
            @pl.when(pd_ref[e] > 0)
            def _(e=e):
                zero_group(pe_ref[e] - tm).wait()
        lax.fori_loop(used, n_groups, lambda gi, c: (zero_group(gi * tm).wait(), c)[1], 0)

    @pl.when(i >= 2)
    def _():
        wait_slot(slot)

    buf[slot] = _pack_bf16_pairs(_rms(x_ref[...], g_ref[...]))

    def body(t, _):
        for k in range(TOP_K):
            row_copy(slot, t, dest_ref[0, 0, t * TOP_K + k]).start()
        return 0
    lax.fori_loop(0, tb, body, 0, unroll=4)

    @pl.when(i == n - 1)
    def _():
        @pl.when(n >= 2)
        def _():
            wait_slot(1 - slot)
        wait_slot(slot)


def _scatter_norm(x, g, dest, pad_ends, padded, n_rows, *, tm, tb=128):
    T, D = x.shape
    E = pad_ends.shape[0]
    tb = _tile(T, tb)
    nb = T // tb
    grid_spec = pltpu.PrefetchScalarGridSpec(
        num_scalar_prefetch=2, grid=(nb,),
        in_specs=[pl.BlockSpec((1, 1, TOP_K * tb), lambda i, pe, pd: (i, 0, 0), memory_space=pltpu.SMEM),
                  pl.BlockSpec((tb, D), lambda i, pe, pd: (i, 0)),
                  pl.BlockSpec((1, D), lambda i, pe, pd: (0, 0))],
        out_specs=pl.BlockSpec(memory_space=pl.ANY),
        scratch_shapes=[pltpu.VMEM((2, tb, D // 2), jnp.uint32), pltpu.VMEM((tm, D // 2), jnp.uint32),
                        pltpu.SemaphoreType.DMA((2,)), pltpu.SemaphoreType.DMA(())])
    return pl.pallas_call(
        functools.partial(_scatter_norm_kernel, tm=tm, n_experts=E), grid_spec=grid_spec,
        out_shape=jax.ShapeDtypeStruct((n_rows, D // 2), jnp.uint32),
        compiler_params=_params("arbitrary"),
    )(pad_ends, padded, dest.reshape(nb, 1, TOP_K * tb), x, g.reshape(1, D))


def _combine_kernel(idx_ref, nxt_ref, x_ref, gate_ref, g_ref, rows_hbm, o_ref, buf, sem, *, final_norm):
    tb = x_ref.shape[0]
    slot = _gather_pipeline(idx_ref, nxt_ref, rows_hbm, buf, sem, TOP_K * tb)
    y = x_ref[...]
    gates = gate_ref[...]
    for k in range(TOP_K):
        y = y + gates[:, k:k + 1] * buf[slot, pl.ds(k * tb, tb), :]
    o_ref[...] = _rms(y, g_ref[...]) if final_norm else y


def _combine(x, rows, dest, gates_slab, g, final_norm, *, tb=64):
    T, D = x.shape
    tb = _tile(T, tb)
    nb = T // tb
    idx3 = dest.reshape(nb, tb, TOP_K).transpose(0, 2, 1).reshape(nb, 1, TOP_K * tb)
    return pl.pallas_call(
        functools.partial(_combine_kernel, final_norm=final_norm),
        grid=(nb,),
        in_specs=[pl.BlockSpec((1, 1, TOP_K * tb), lambda i: (i, 0, 0), memory_space=pltpu.SMEM),
                  pl.BlockSpec((1, 1, TOP_K * tb), lambda i: (jnp.minimum(i + 1, nb - 1), 0, 0), memory_space=pltpu.SMEM),
                  pl.BlockSpec((tb, D), lambda i: (i, 0)),
                  pl.BlockSpec((tb, LANES), lambda i: (i, 0)),
                  pl.BlockSpec((1, D), lambda i: (0, 0)),
                  pl.BlockSpec(memory_space=pl.ANY)],
        out_specs=pl.BlockSpec((tb, D), lambda i: (i, 0)),
        out_shape=jax.ShapeDtypeStruct((T, D), F32),
        scratch_shapes=[pltpu.VMEM((2, TOP_K * tb, D), F32), pltpu.SemaphoreType.DMA((2,))],
        compiler_params=_params("arbitrary"),
    )(idx3, idx3, x, gates_slab, g.reshape(1, D), rows)


def _new_expert(be_ref, i):
    return jnp.logical_or(i == 0, be_ref[i] != be_ref[jnp.maximum(i - 1, 0)])


def _moe_up_kernel(be_ref, nu_ref, xs_ref, w_ref, b_ref, o_ref, wb_ref):
    i = pl.program_id(1)
    half = xs_ref.shape[1]
    pair = 2 * LANES

    @pl.when(_new_expert(be_ref, i))
    def _():
        wb_ref[...] = w_ref[...].astype(BF16)

    @pl.when(i < nu_ref[0])
    def _():
        lo, hi = _unpack_bf16_pairs(xs_ref[...])
        row = lax.broadcasted_iota(jnp.int32, (pair, LANES), 0)
        col = lax.broadcasted_iota(jnp.int32, (pair, LANES), 1)
        pick_even = (row == 2 * col).astype(BF16)
        even = lax.broadcasted_iota(jnp.int32, (1, pair), 1) % 2 == 0
        tn = wb_ref.shape[1]
        wide = min(tn, 2 * pair)
        for c in range(tn // pair):
            if (c * pair) % wide == 0:
                cols = slice(c * pair, c * pair + wide)
                gu = (jnp.dot(lo, wb_ref[:half, cols], preferred_element_type=F32)
                      + jnp.dot(hi, wb_ref[half:, cols], preferred_element_type=F32)) + b_ref[:, cols]
            blk = gu[:, (c * pair) % wide:(c * pair) % wide + pair]
            nxt = pltpu.roll(blk, pair - 1, axis=1)
            gate = jnp.minimum(blk, SWIGLU_LIMIT)
            up = jnp.clip(nxt, -SWIGLU_LIMIT, SWIGLU_LIMIT)
            act = (up + 1.0) * gate * jax.nn.sigmoid(SWIGLU_ALPHA * gate)
            act = jnp.where(even, act, 0.0).astype(BF16)
            o_ref[:, c * LANES:(c + 1) * LANES] = jnp.dot(act, pick_even, preferred_element_type=F32).astype(o_ref.dtype)

    @pl.when(i >= nu_ref[0])
    def _():
        o_ref[...] = jnp.zeros_like(o_ref)


def _moe_up(xs, w_gu, b_gu, block_e, n_used, *, tm, tn=1024):
    R, half = xs.shape
    E, D, F2 = w_gu.shape
    tn = _tile(F2, tn)
    assert tn % (2 * LANES) == 0 and D == 2 * half
    nb = R // tm
    grid_spec = pltpu.PrefetchScalarGridSpec(
        num_scalar_prefetch=2, grid=(F2 // tn, nb),
        in_specs=[pl.BlockSpec((tm, half), lambda j, i, be, nu: (jnp.minimum(i, nu[0] - 1), 0)),
                  pl.BlockSpec((None, D, tn), lambda j, i, be, nu: (be[i], 0, j)),
                  pl.BlockSpec((None, 1, tn), lambda j, i, be, nu: (be[i], 0, j))],
        out_specs=pl.BlockSpec((tm, tn // 2), lambda j, i, be, nu: (i, j)),
        scratch_shapes=[pltpu.VMEM((D, tn), BF16)])
    return pl.pallas_call(
        _moe_up_kernel, grid_spec=grid_spec,
        out_shape=jax.ShapeDtypeStruct((R, F2 // 2), BF16),
        compiler_params=_params("parallel", "arbitrary"),
    )(block_e, n_used, xs, w_gu, b_gu.reshape(E, 1, F2))


def _moe_down_kernel(be_ref, nu_ref, a_ref, w_ref, b_ref, o_ref, wb_ref):
    i = pl.program_id(1)

    @pl.when(_new_expert(be_ref, i))
    def _():
        wb_ref[...] = w_ref[...].astype(BF16)

    @pl.when(i < nu_ref[0])
    def _():
        o_ref[...] = jnp.dot(a_ref[...], wb_ref[...], preferred_element_type=F32) + b_ref[...]

    @pl.when(i >= nu_ref[0])
    def _():
        o_ref[...] = jnp.zeros_like(o_ref)


def _moe_down(act, w_d, b_d, block_e, n_used, *, tm, tn=2048):
    R, F = act.shape
    E, _, D = w_d.shape
    tn = _tile(D, tn)
    nb = R // tm
    grid_spec = pltpu.PrefetchScalarGridSpec(
        num_scalar_prefetch=2, grid=(D // tn, nb),
        in_specs=[pl.BlockSpec((tm, F), lambda j, i, be, nu: (i, 0)),
                  pl.BlockSpec((None, F, tn), lambda j, i, be, nu: (be[i], 0, j)),
                  pl.BlockSpec((None, 1, tn), lambda j, i, be, nu: (be[i], 0, j))],
        out_specs=pl.BlockSpec((tm, tn), lambda j, i, be, nu: (i, j)),
        scratch_shapes=[pltpu.VMEM((F, tn), BF16)])
    return pl.pallas_call(
        _moe_down_kernel, grid_spec=grid_spec,
        out_shape=jax.ShapeDtypeStruct((R, D), F32),
        compiler_params=_params("parallel", "arbitrary"),
    )(block_e, n_used, act, w_d, b_d.reshape(E, 1, D))


def _moe(x, g_moe, w_router, b_router, w_gate_up, b_gate_up, w_down, b_down, g_final, final_norm, *, tm=512):
    T, D = x.shape
    E = w_router.shape[1]
    A = T * TOP_K
    tm = min(tm, A)
    slab_i, slab_g, cnt = _router(x, g_moe, w_router, b_router)
    idx = slab_i[:, :TOP_K]
    rank = slab_i[:, TOP_K:2 * TOP_K]
    counts = cnt[0, :E].astype(jnp.int32)
    padded = (counts + tm - 1) // tm * tm
    pad_ends = jnp.cumsum(padded)
    pad_starts = pad_ends - padded
    dest = pad_starts[idx] + rank
    R = (A + tm - 1) // tm * tm + E * tm
    nb = R // tm
    block_start = jnp.arange(nb, dtype=jnp.int32) * tm
    block_e = jnp.minimum(jnp.sum(block_start[:, None] >= pad_ends[None, :], axis=1), E - 1).astype(jnp.int32)
    n_used = (pad_ends[-1:] // tm).astype(jnp.int32)

    xs = _scatter_norm(x, g_moe, dest, pad_ends, padded, R, tm=tm)
    act = _moe_up(xs, w_gate_up, b_gate_up, block_e, n_used, tm=tm)
    rows = _moe_down(act, w_down, b_down, block_e, n_used, tm=tm)
    return _combine(x, rows, dest, slab_g, g_final, final_norm)


def kernel(x, mem, g_mix, w_in, b_in, w_pool, pool_scale, w_fox_o, w_out, g_mem_q, g_mem_kv, w_mem_q, w_mem_kv,
           w_mem_o, g_moe, w_router, b_router, w_gate_up, b_gate_up, w_down, b_down, g_final):
    B, S, D = x.shape
    T = B * S
    n_mem = mem.shape[1]
    depth, G, C, Do = w_pool.shape
    pool_w = G * C
    fox_w = w_fox_o.shape[1]
    H = w_in.shape[2] - pool_w - 3 * fox_w - 2 * D
    dh = fox_w // H
    off_q, off_f = pool_w, pool_w + 3 * fox_w
    off_gate = off_f + H
    xt = x.reshape(T, D)
    mt = mem.reshape(B * n_mem, D)
    for l in range(depth):
        wl, bl = w_in[l], b_in[l]
        h, lf_t = _norm(xt, g_mix[l], wl[:, off_f:off_gate].T.astype(BF16), bl[off_f:off_gate])
        u = _proj(h, wl, bl, col_off=0, n_cols=off_q, out_dtype=F32)
        qkv = _proj(h, wl, bl, col_off=off_q, n_cols=off_f - off_q, out_dtype=BF16)
        gate_base = off_gate // LANES * LANES
        gate0 = off_gate - gate_base
        gates = _proj(h, wl, bl, col_off=gate_base, mode="sigmoid", out_dtype=BF16)
        c_t = _forget_cumsum(lf_t, B, S)
        att = _fox_attention(qkv, c_t.transpose(0, 2, 1), c_t, B, S, H, dh)
        pp = _pool_mixer(u, w_pool[l], pool_scale[l], gates, gate0, S)
        merged = _proj(att, w_fox_o[l], None, pp, (gates, gate0 + D), mode="merge", out_dtype=BF16)
        xt = _proj(merged, w_out[l], None, xt, mode="residual", out_dtype=F32)
        kv = _proj(_norm(mt, g_mem_kv[l]), w_mem_kv[l], out_dtype=BF16)
        qm = _proj(_norm(xt, g_mem_q[l]), w_mem_q[l], out_dtype=BF16)
        om = _mem_attention(qm, kv, B, S, n_mem)
        xt = _proj(om, w_mem_o[l], None, xt, mode="residual", out_dtype=F32)
        xt = _moe(xt, g_moe[l], w_router[l], b_router[l], w_gate_up[l], b_gate_up[l], w_down[l], b_down[l],
                  g_final, final_norm=l == depth - 1)
    return xt.reshape(B, S, D)
```

```python
import functools

import jax
import jax.numpy as jnp
from jax import lax
from jax.experimental import pallas as pl
from jax.experimental.pallas import tpu as pltpu

F32 = jnp.float32
BF16 = jnp.bfloat16

EPS = 1e-5
POOL_WINDOWS = (2, 4, 8, 16)
POOL_HALO = 16
MEM_HEADS = 4
TOP_K = 4
SWIGLU_LIMIT = 7.0
SWIGLU_ALPHA = 1.702
NEG_BIG = -1e30
LOG2E = 1.4426950408889634

LANES = 128
VMEM_LIMIT_BYTES = 56 * 1024 * 1024


def _tile(dim, pref):
    t = pref
    while t >= 8:
        if dim % t == 0:
            return t
        t //= 2
    return dim


def _params(*sem):
    return pltpu.CompilerParams(dimension_semantics=sem, vmem_limit_bytes=VMEM_LIMIT_BYTES)


def _rms(x, g):
    ms = jnp.mean(x * x, axis=-1, keepdims=True)
    return x * lax.rsqrt(ms + EPS) * g


def _split3(x):
    hi = x.astype(BF16)
    r1 = x - hi.astype(F32)
    mid = r1.astype(BF16)
    lo = (r1 - mid.astype(F32)).astype(BF16)
    return hi, mid, lo


def _norm_kernel(x_ref, g_ref, *refs, with_f):
    h = _rms(x_ref[...], g_ref[...]).astype(BF16)
    if with_f:
        wf_ref, bf_ref, h_ref, lf_ref = refs
        f = lax.dot_general(wf_ref[...], h, (((1,), (1,)), ((), ())), preferred_element_type=F32) + bf_ref[...]
        lf_ref[...] = jnp.minimum(f, 0.0) - jnp.log1p(jnp.exp(-jnp.abs(f)))
    else:
        h_ref, = refs
    h_ref[...] = h


def _norm(x, g, wf_t=None, bf_t=None, *, tm=512):
    M, K = x.shape
    tm = _tile(M, tm)
    with_f = wf_t is not None
    in_specs = [pl.BlockSpec((tm, K), lambda i: (i, 0)), pl.BlockSpec((1, K), lambda i: (0, 0))]
    args = [x, g.reshape(1, K)]
    out_shape = [jax.ShapeDtypeStruct((M, K), BF16)]
    out_specs = [pl.BlockSpec((tm, K), lambda i: (i, 0))]
    if with_f:
        H = wf_t.shape[0]
        in_specs += [pl.BlockSpec((H, K), lambda i: (0, 0)), pl.BlockSpec((H, 1), lambda i: (0, 0))]
        args += [wf_t, bf_t.reshape(H, 1)]
        out_shape.append(jax.ShapeDtypeStruct((H, M), F32))
        out_specs.append(pl.BlockSpec((H, tm), lambda i: (0, i)))
    res = pl.pallas_call(
        functools.partial(_norm_kernel, with_f=with_f),
        grid=(M // tm,),
        in_specs=in_specs, out_specs=out_specs, out_shape=out_shape,
        compiler_params=_params("parallel"),
    )(*args)
    return res if with_f else res[0]


def _lane_window(main_ref, tail_ref, shift):
    if shift == 0:
        return main_ref[...].astype(F32)
    g = jnp.concatenate([main_ref[...], tail_ref[...]], axis=1).astype(F32)
    return pltpu.roll(g, g.shape[1] - shift, axis=1)[:, :main_ref.shape[1]]


def _window_specs(first_col, rows, width, row_of, col_block_of):
    base, shift = first_col // LANES * LANES, first_col % LANES
    assert base % width == 0
    specs = [pl.BlockSpec((rows, width), lambda *g: (row_of(*g), col_block_of(*g) + base // width))]
    if shift:
        specs.append(pl.BlockSpec(
            (rows, LANES), lambda *g: (row_of(*g), (base + (col_block_of(*g) + 1) * width) // LANES)))
    return specs, shift


def _proj_kernel(x_ref, w_ref, *refs, mode, has_bias, gate_shift, w_rows_are_outputs):
    refs = list(refs)
    wb_ref = refs.pop()
    o_ref = refs.pop()
    b_ref = refs.pop(0) if has_bias else None

    @pl.when(pl.program_id(1) == 0)
    def _():
        w = w_ref[...]
        wb_ref[...] = (w.T if w_rows_are_outputs else w).astype(BF16)

    acc = jnp.dot(x_ref[...], wb_ref[...], preferred_element_type=F32)
    if has_bias:
        acc = acc + b_ref[...]
    if mode == "sigmoid":
        acc = jax.nn.sigmoid(acc)
    elif mode == "merge":
        pp_ref, g_ref = refs[0], refs[1]
        acc = pp_ref[...] + _lane_window(g_ref, refs[2] if gate_shift else None, gate_shift) * acc
    elif mode == "residual":
        acc = refs[0][...] + acc
    o_ref[...] = acc.astype(o_ref.dtype)


def _proj(x, w, b=None, extra=None, gate=None, *, w_t=False, col_off=0, n_cols=None, mode="plain", out_dtype,
          tm=1024, tn=512):
    M, K = x.shape
    N = w.shape[0 if w_t else 1] - col_off if n_cols is None else n_cols
    tm = _tile(M, tm)
    tn = min(tn, N)
    while col_off % tn:
        tn //= 2
    assert tn % LANES == 0
    if w_t:
        w_spec = pl.BlockSpec((tn, K), lambda j, i, o=col_off // tn: (j + o, 0))
    else:
        w_spec = pl.BlockSpec((K, tn), lambda j, i, o=col_off // tn: (0, j + o))
    in_specs = [pl.BlockSpec((tm, K), lambda j, i: (i, 0)), w_spec]
    args = [x, w]
    if b is not None:
        in_specs.append(pl.BlockSpec((1, tn), lambda j, i, o=col_off // tn: (0, j + o)))
        args.append(b.reshape(1, -1))
    if extra is not None:
        in_specs.append(pl.BlockSpec((tm, tn), lambda j, i: (i, j)))
        args.append(extra)
    gate_shift = 0
    if gate is not None:
        specs, gate_shift = _window_specs(gate[1], tm, tn, lambda j, i: i, lambda j, i: j)
        in_specs += specs
        args += [gate[0]] * len(specs)
    return pl.pallas_call(
        functools.partial(_proj_kernel, mode=mode, has_bias=b is not None, gate_shift=gate_shift,
                          w_rows_are_outputs=w_t),
        grid=(pl.cdiv(N, tn), M // tm),
        in_specs=in_specs,
        out_specs=pl.BlockSpec((tm, tn), lambda j, i: (i, j)),
        out_shape=jax.ShapeDtypeStruct((M, N), out_dtype),
        scratch_shapes=[pltpu.VMEM((K, tn), BF16)],
        compiler_params=_params("parallel", "arbitrary"),
    )(*args)


def _cumsum_kernel(lf_ref, c_ref):
    S = lf_ref.shape[1]
    row = lax.broadcasted_iota(jnp.int32, (S, S), 0)
    col = lax.broadcasted_iota(jnp.int32, (S, S), 1)
    upper = (row <= col).astype(BF16)
    c = jnp.zeros(lf_ref.shape, F32)
    for part in _split3(lf_ref[...]):
        c = c + jnp.dot(part, upper, preferred_element_type=F32)
    c_ref[...] = c


def _forget_cumsum(lf_t, B, S):
    H = lf_t.shape[0]
    return pl.pallas_call(
        _cumsum_kernel,
        grid=(B,),
        in_specs=[pl.BlockSpec((H, S), lambda b: (0, b))],
        out_specs=pl.BlockSpec((None, H, S), lambda b: (b, 0, 0)),
        out_shape=jax.ShapeDtypeStruct((B, H, S), F32),
        compiler_params=_params("parallel"),
    )(lf_t)


def _pool_kernel(u_ref, halo_ref, w_ref, sc_ref, *refs, seq_len, gate_shift):
    g0_ref, g0_tail_ref = (refs[0], refs[1]) if gate_shift else (refs[0], None)
    o_ref, ext_ref, pooled_ref = refs[-3:]
    g = pl.program_id(0)
    i = pl.program_id(1)
    tp = u_ref.shape[0]
    pos0 = (i * tp) % seq_len
    u = u_ref[...]
    ext_ref[pl.ds(POOL_HALO, tp), :] = u
    ext_ref[pl.ds(0, POOL_HALO), :] = jnp.where(pos0 == 0, 0.0, halo_ref[...])
    pos = pos0 + lax.broadcasted_iota(jnp.int32, (tp, 1), 0)
    for gi, win in enumerate(POOL_WINDOWS):
        @pl.when(g == gi)
        def _(win=win):
            acc = u
            for k in range(1, win):
                acc = acc + ext_ref[pl.ds(POOL_HALO - k, tp), :]
            cnt = jnp.minimum(pos + 1, win).astype(F32)
            pooled_ref[...] = (acc / cnt - u).astype(BF16)
    y = jnp.dot(pooled_ref[...], w_ref[...].astype(BF16), preferred_element_type=F32)
    o_ref[...] = _lane_window(g0_ref, g0_tail_ref, gate_shift) * (y * sc_ref[...])


def _pool_mixer(u, w_pool, scale, gates, gate0_col, seq_len, *, tp=512):
    T = u.shape[0]
    G, C, Do = w_pool.shape
    tp = _tile(seq_len, tp)
    assert tp % POOL_HALO == 0
    hb = tp // POOL_HALO
    gate_specs, gate_shift = _window_specs(gate0_col, tp, Do, lambda g, i: i, lambda g, i: g)
    return pl.pallas_call(
        functools.partial(_pool_kernel, seq_len=seq_len, gate_shift=gate_shift),
        grid=(G, T // tp),
        in_specs=[pl.BlockSpec((tp, C), lambda g, i: (i, g)),
                  pl.BlockSpec((POOL_HALO, C), lambda g, i: (jnp.maximum(i * hb - 1, 0), g)),
                  pl.BlockSpec((None, C, Do), lambda g, i: (g, 0, 0)),
                  pl.BlockSpec((1, Do), lambda g, i: (0, g))] + gate_specs,
        out_specs=pl.BlockSpec((tp, Do), lambda g, i: (i, g)),
        out_shape=jax.ShapeDtypeStruct((T, G * Do), F32),
        scratch_shapes=[pltpu.VMEM((tp + POOL_HALO, C), F32), pltpu.VMEM((tp, C), BF16)],
        compiler_params=_params("parallel", "parallel"),
    )(u, u, w_pool, scale.reshape(1, G * Do), *([gates] * len(gate_specs)))


def _fox_kernel(q_ref, k_ref, v_ref, cq_ref, ck_ref, o_ref, *, scale, dh):
    hg = pl.program_id(1)
    qi = pl.program_id(2)
    tq = q_ref.shape[0]
    n_heads = q_ref.shape[1] // dh
    lane = lax.broadcasted_iota(jnp.int32, cq_ref.shape, 1)
    cq_all = cq_ref[...]
    q = [q_ref[:, n * dh:(n + 1) * dh] for n in range(n_heads)]
    cq2 = [LOG2E * jnp.sum(jnp.where(lane == hg * n_heads + n, cq_all, 0.0), axis=1, keepdims=True)
           for n in range(n_heads)]

    def scores(n, j):
        rows = pl.ds(pl.multiple_of(j * tq, tq), tq)
        s = lax.dot_general(q[n], k_ref[rows, n * dh:(n + 1) * dh], (((1,), (1,)), ((), ())),
                            preferred_element_type=F32) * (scale * LOG2E)
        return s - LOG2E * ck_ref[n:n + 1, rows]

    def update(n, j, s, carry):
        m, l, acc = carry
        m_new = jnp.maximum(m, jnp.max(s, axis=1, keepdims=True) + cq2[n])
        alpha = jnp.exp2(m - m_new)
        p = jnp.exp2(s - (m_new - cq2[n]))
        v = v_ref[pl.ds(pl.multiple_of(j * tq, tq), tq), n * dh:(n + 1) * dh]
        acc = alpha * acc + jnp.dot(p.astype(BF16), v, preferred_element_type=F32)
        return m_new, alpha * l + jnp.sum(p, axis=1, keepdims=True), acc

    init = (jnp.full((tq, 1), NEG_BIG, F32), jnp.zeros((tq, 1), F32), jnp.zeros((tq, dh), F32))
    carries = lax.fori_loop(
        0, qi, lambda j, cs: tuple(update(n, j, scores(n, j), cs[n]) for n in range(n_heads)), (init,) * n_heads)
    row = lax.broadcasted_iota(jnp.int32, (tq, tq), 0)
    col = lax.broadcasted_iota(jnp.int32, (tq, tq), 1)
    for n in range(n_heads):
        s = jnp.where(col <= row, scores(n, qi), NEG_BIG)
        m, l, acc = update(n, qi, s, carries[n])
        o_ref[:, n * dh:(n + 1) * dh] = (acc / l).astype(o_ref.dtype)


def _fox_attention(qkv, c, c_t, B, S, H, dh, *, tq=512, heads_per_step=2):
    T = B * S
    tq = _tile(S, tq)
    nq = S // tq
    hp = heads_per_step if H % heads_per_step == 0 else 1
    ng = H // hp
    w = hp * dh
    return pl.pallas_call(
        functools.partial(_fox_kernel, scale=dh ** -0.5, dh=dh),
        grid=(B, ng, nq),
        in_specs=[pl.BlockSpec((tq, w), lambda b, h, i: (b * nq + i, h)),
                  pl.BlockSpec((S, w), lambda b, h, i: (b, ng + h)),
                  pl.BlockSpec((S, w), lambda b, h, i: (b, 2 * ng + h)),
                  pl.BlockSpec((None, tq, H), lambda b, h, i: (b, i, 0)),
                  pl.BlockSpec((None, None, hp, S), lambda b, h, i: (b, h, 0, 0))],
        out_specs=pl.BlockSpec((tq, w), lambda b, h, i: (b * nq + i, h)),
        out_shape=jax.ShapeDtypeStruct((T, H * dh), BF16),
        compiler_params=_params("parallel", "parallel", "parallel"),
    )(qkv, qkv, qkv, c, c_t.reshape(B, ng, hp, S))


def _mem_attn_kernel(q_ref, kv_ref, o_ref, *, heads):
    width = q_ref.shape[1]
    dh = width // heads
    scale = dh ** -0.5
    for hd in range(heads):
        q = q_ref[:, hd * dh:(hd + 1) * dh]
        k = kv_ref[:, hd * dh:(hd + 1) * dh]
        v = kv_ref[:, width + hd * dh:width + (hd + 1) * dh]
        s = lax.dot_general(q, k, (((1,), (1,)), ((), ())), preferred_element_type=F32) * scale
        p = jnp.exp(s - jnp.max(s, axis=1, keepdims=True))
        p = p / jnp.sum(p, axis=1, keepdims=True)
        o_ref[:, hd * dh:(hd + 1) * dh] = jnp.dot(p.astype(BF16), v, preferred_element_type=F32).astype(o_ref.dtype)


def _mem_attention(q, kv, B, S, n_mem, *, tq=512):
    T, width = q.shape
    tq = _tile(S, tq)
    nq = S // tq
    return pl.pallas_call(
        functools.partial(_mem_attn_kernel, heads=MEM_HEADS),
        grid=(B, nq),
        in_specs=[pl.BlockSpec((tq, width), lambda b, i: (b * nq + i, 0)),
                  pl.BlockSpec((n_mem, 2 * width), lambda b, i: (b, 0))],
        out_specs=pl.BlockSpec((tq, width), lambda b, i: (b * nq + i, 0)),
        out_shape=jax.ShapeDtypeStruct((T, width), BF16),
        compiler_params=_params("parallel", "parallel"),
    )(q, kv)


def _router_kernel(x_ref, g_ref, whi_ref, wlo_ref, b_ref, oi_ref, og_ref, cnt_ref, carry_ref, *, n_experts):
    i = pl.program_id(0)
    tm = x_ref.shape[0]

    @pl.when(i == 0)
    def _():
        carry_ref[...] = jnp.zeros_like(carry_ref)

    h = _rms(x_ref[...], g_ref[...])
    h_hi = h.astype(BF16)
    h_lo = (h - h_hi.astype(F32)).astype(BF16)
    logits = (jnp.dot(h_hi, whi_ref[...], preferred_element_type=F32)
              + jnp.dot(h_hi, wlo_ref[...], preferred_element_type=F32)
              + jnp.dot(h_lo, whi_ref[...], preferred_element_type=F32)) + b_ref[...]
    lane = lax.broadcasted_iota(jnp.int32, (tm, LANES), 1).astype(F32)
    work = jnp.where(lane < n_experts, logits, -jnp.inf)
    vals, idxs = [], []
    for _ in range(TOP_K):
        m = jnp.max(work, axis=1, keepdims=True)
        idx = jnp.min(jnp.where(work == m, lane, float(LANES)), axis=1, keepdims=True)
        vals.append(m)
        idxs.append(idx)
        work = jnp.where(lane == idx, -jnp.inf, work)
    exps = [jnp.exp(v - vals[0]) for v in vals]
    denom = exps[0] + exps[1] + exps[2] + exps[3]
    onehots = [(lane == idx).astype(F32) for idx in idxs]
    chosen = onehots[0] + onehots[1] + onehots[2] + onehots[3]
    row = lax.broadcasted_iota(jnp.int32, (tm, tm), 0)
    col = lax.broadcasted_iota(jnp.int32, (tm, tm), 1)
    before = jnp.dot((col < row).astype(BF16), chosen.astype(BF16), preferred_element_type=F32) + carry_ref[...]
    out_i = jnp.zeros((tm, LANES), F32)
    out_g = jnp.zeros((tm, LANES), F32)
    for k in range(TOP_K):
        rank = jnp.sum(onehots[k] * before, axis=1, keepdims=True)
        out_i = jnp.where(lane == k, idxs[k], out_i)
        out_i = jnp.where(lane == TOP_K + k, rank, out_i)
        out_g = jnp.where(lane == k, exps[k] / denom, out_g)
    oi_ref[...] = out_i.astype(jnp.int32)
    og_ref[...] = out_g
    carry_ref[...] = carry_ref[...] + jnp.sum(chosen, axis=0, keepdims=True)
    cnt_ref[...] = carry_ref[...]


def _router(x, g, w_router, b_router, *, tm=512):
    T, D = x.shape
    E = w_router.shape[1]
    tm = _tile(T, tm)
    w_pad = jnp.zeros((D, LANES), F32).at[:, :E].set(w_router)
    w_hi = w_pad.astype(BF16)
    w_lo = (w_pad - w_hi.astype(F32)).astype(BF16)
    b_pad = jnp.zeros((1, LANES), F32).at[0, :E].set(b_router)
    full = lambda i: (0, 0)
    return pl.pallas_call(
        functools.partial(_router_kernel, n_experts=E),
        grid=(T // tm,),
        in_specs=[pl.BlockSpec((tm, D), lambda i: (i, 0)), pl.BlockSpec((1, D), full),
                  pl.BlockSpec((D, LANES), full), pl.BlockSpec((D, LANES), full), pl.BlockSpec((1, LANES), full)],
        out_specs=[pl.BlockSpec((tm, LANES), lambda i: (i, 0)), pl.BlockSpec((tm, LANES), lambda i: (i, 0)),
                   pl.BlockSpec((1, LANES), full)],
        out_shape=[jax.ShapeDtypeStruct((T, LANES), jnp.int32), jax.ShapeDtypeStruct((T, LANES), F32),
                   jax.ShapeDtypeStruct((1, LANES), F32)],
        scratch_shapes=[pltpu.VMEM((1, LANES), F32)],
        compiler_params=_params("arbitrary"),
    )(x, g.reshape(1, D), w_hi, w_lo, b_pad)


def _row_copy(src_hbm, row, buf, slot, r, sem):
    return pltpu.make_async_copy(src_hbm.at[pl.ds(row, 1)], buf.at[slot, pl.ds(r, 1)], sem.at[slot])


def _start_rows(idx_ref, src_hbm, buf, slot, sem, n_rows):
    def body(r, _):
        _row_copy(src_hbm, idx_ref[0, 0, r], buf, slot, r, sem).start()
        return 0
    lax.fori_loop(0, n_rows, body, 0, unroll=8)


def _wait_rows(src_hbm, buf, slot, sem, n_rows):
    pltpu.make_async_copy(src_hbm.at[pl.ds(0, n_rows)], buf.at[slot], sem.at[slot]).wait()


def _gather_pipeline(idx_ref, nxt_ref, src_hbm, buf, sem, n_rows):
    i = pl.program_id(0)
    n = pl.num_programs(0)
    slot = i % 2

    @pl.when(i == 0)
    def _():
        _start_rows(idx_ref, src_hbm, buf, 0, sem, n_rows)

    @pl.when(i + 1 < n)
    def _():
        _start_rows(nxt_ref, src_hbm, buf, 1 - slot, sem, n_rows)

    _wait_rows(src_hbm, buf, slot, sem, n_rows)
    return slot


def _pack_bf16_pairs(h):
    half = h.shape[1] // 2
    bits = pltpu.bitcast(h.astype(BF16).astype(F32), jnp.uint32)
    return (bits[:, :half] >> 16) | bits[:, half:]


def _unpack_bf16_pairs(u):
    lo = pltpu.bitcast(u << 16, F32).astype(BF16)
    hi = pltpu.bitcast(u & jnp.uint32(0xFFFF0000), F32).astype(BF16)
    return lo, hi


def _scatter_norm_kernel(pe_ref, pd_ref, dest_ref, x_ref, g_ref, xs_hbm, buf, zbuf, sem, zsem, *, tm, n_experts):
    i = pl.program_id(0)
    n = pl.num_programs(0)
    tb = x_ref.shape[0]
    slot = i % 2

    def row_copy(s, t, row):
        return pltpu.make_async_copy(buf.at[s, pl.ds(t, 1)], xs_hbm.at[pl.ds(row, 1)], sem.at[s])

    def wait_slot(s):
        for _ in range(TOP_K):
            pltpu.make_async_copy(buf.at[s], xs_hbm.at[pl.ds(0, tb)], sem.at[s]).wait()

    def zero_group(first_row):
        return pltpu.make_async_copy(zbuf, xs_hbm.at[pl.ds(pl.multiple_of(first_row, tm), tm)], zsem)

    @pl.when(i == 0)
    def _():
        zbuf[...] = jnp.zeros_like(zbuf)
        n_groups = xs_hbm.shape[0] // tm
        used = pe_ref[n_experts - 1] // tm
        for e in range(n_experts):
            @pl.when(pd_ref[e] > 0)
            def _(e=e):
                zero_group(pe_ref[e] - tm).start()
        lax.fori_loop(used, n_groups, lambda gi, c: (zero_group(gi * tm).start(), c)[1], 0)
        for e in range(n_experts):
            @pl.when(pd_ref[e] > 0)
            def _(e=e):
                zero_group(pe_ref[e] - tm).wait()
        lax.fori_loop(used, n_groups, lambda gi, c: (zero_group(gi * tm).wait(), c)[1], 0)

    @pl.when(i >= 2)
    def _():
        wait_slot(slot)

    buf[slot] = _pack_bf16_pairs(_rms(x_ref[...], g_ref[...]))

    def body(t, _):
        for k in range(TOP_K):
            row_copy(slot, t, dest_ref[0, 0, t * TOP_K + k]).start()
        return 0
    lax.fori_loop(0, tb, body, 0, unroll=4)

    @pl.when(i == n - 1)
    def _():
        @pl.when(n >= 2)
        def _():
            wait_slot(1 - slot)
        wait_slot(slot)


def _scatter_norm(x, g, dest, pad_ends, padded, n_rows, *, tm, tb=128):
    T, D = x.shape
    E = pad_ends.shape[0]
    tb = _tile(T, tb)
    nb = T // tb
    grid_spec = pltpu.PrefetchScalarGridSpec(
        num_scalar_prefetch=2, grid=(nb,),
        in_specs=[pl.BlockSpec((1, 1, TOP_K * tb), lambda i, pe, pd: (i, 0, 0), memory_space=pltpu.SMEM),
                  pl.BlockSpec((tb, D), lambda i, pe, pd: (i, 0)),
                  pl.BlockSpec((1, D), lambda i, pe, pd: (0, 0))],
        out_specs=pl.BlockSpec(memory_space=pl.ANY),
        scratch_shapes=[pltpu.VMEM((2, tb, D // 2), jnp.uint32), pltpu.VMEM((tm, D // 2), jnp.uint32),
                        pltpu.SemaphoreType.DMA((2,)), pltpu.SemaphoreType.DMA(())])
    return pl.pallas_call(
        functools.partial(_scatter_norm_kernel, tm=tm, n_experts=E), grid_spec=grid_spec,
        out_shape=jax.ShapeDtypeStruct((n_rows, D // 2), jnp.uint32),
        compiler_params=_params("arbitrary"),
    )(pad_ends, padded, dest.reshape(nb, 1, TOP_K * tb), x, g.reshape(1, D))


def _combine_kernel(idx_ref, nxt_ref, x_ref, gate_ref, g_ref, rows_hbm, o_ref, buf, sem, *, final_norm):
    tb = x_ref.shape[0]
    slot = _gather_pipeline(idx_ref, nxt_ref, rows_hbm, buf, sem, TOP_K * tb)
    y = x_ref[...]
    gates = gate_ref[...]
    for k in range(TOP_K):
        y = y + gates[:, k:k + 1] * buf[slot, pl.ds(k * tb, tb), :]
    o_ref[...] = _rms(y, g_ref[...]) if final_norm else y


def _combine(x, rows, dest, gates_slab, g, final_norm, *, tb=64):
    T, D = x.shape
    tb = _tile(T, tb)
    nb = T // tb
    idx3 = dest.reshape(nb, tb, TOP_K).transpose(0, 2, 1).reshape(nb, 1, TOP_K * tb)
    return pl.pallas_call(
        functools.partial(_combine_kernel, final_norm=final_norm),
        grid=(nb,),
        in_specs=[pl.BlockSpec((1, 1, TOP_K * tb), lambda i: (i, 0, 0), memory_space=pltpu.SMEM),
                  pl.BlockSpec((1, 1, TOP_K * tb), lambda i: (jnp.minimum(i + 1, nb - 1), 0, 0), memory_space=pltpu.SMEM),
                  pl.BlockSpec((tb, D), lambda i: (i, 0)),
                  pl.BlockSpec((tb, LANES), lambda i: (i, 0)),
                  pl.BlockSpec((1, D), lambda i: (0, 0)),
                  pl.BlockSpec(memory_space=pl.ANY)],
        out_specs=pl.BlockSpec((tb, D), lambda i: (i, 0)),
        out_shape=jax.ShapeDtypeStruct((T, D), F32),
        scratch_shapes=[pltpu.VMEM((2, TOP_K * tb, D), F32), pltpu.SemaphoreType.DMA((2,))],
        compiler_params=_params("arbitrary"),
    )(idx3, idx3, x, gates_slab, g.reshape(1, D), rows)


def _new_expert(be_ref, i):
    return jnp.logical_or(i == 0, be_ref[i] != be_ref[jnp.maximum(i - 1, 0)])


def _moe_up_kernel(be_ref, nu_ref, xs_ref, w_ref, b_ref, o_ref, wb_ref):
    i = pl.program_id(1)
    half = xs_ref.shape[1]
    pair = 2 * LANES

    @pl.when(_new_expert(be_ref, i))
    def _():
        wb_ref[...] = w_ref[...].astype(BF16)

    @pl.when(i < nu_ref[0])
    def _():
        lo, hi = _unpack_bf16_pairs(xs_ref[...])
        row = lax.broadcasted_iota(jnp.int32, (pair, LANES), 0)
        col = lax.broadcasted_iota(jnp.int32, (pair, LANES), 1)
        pick_even = (row == 2 * col).astype(BF16)
        even = lax.broadcasted_iota(jnp.int32, (1, pair), 1) % 2 == 0
        tn = wb_ref.shape[1]
        wide = min(tn, 2 * pair)
        for c in range(tn // pair):
            if (c * pair) % wide == 0:
                cols = slice(c * pair, c * pair + wide)
                gu = (jnp.dot(lo, wb_ref[:half, cols], preferred_element_type=F32)
                      + jnp.dot(hi, wb_ref[half:, cols], preferred_element_type=F32)) + b_ref[:, cols]
            blk = gu[:, (c * pair) % wide:(c * pair) % wide + pair]
            nxt = pltpu.roll(blk, pair - 1, axis=1)
            gate = jnp.minimum(blk, SWIGLU_LIMIT)
            up = jnp.clip(nxt, -SWIGLU_LIMIT, SWIGLU_LIMIT)
            act = (up + 1.0) * gate * jax.nn.sigmoid(SWIGLU_ALPHA * gate)
            act = jnp.where(even, act, 0.0).astype(BF16)
            o_ref[:, c * LANES:(c + 1) * LANES] = jnp.dot(act, pick_even, preferred_element_type=F32).astype(o_ref.dtype)

    @pl.when(i >= nu_ref[0])
    def _():
        o_ref[...] = jnp.zeros_like(o_ref)


def _moe_up(xs, w_gu, b_gu, block_e, n_used, *, tm, tn=1024):
    R, half = xs.shape
    E, D, F2 = w_gu.shape
    tn = _tile(F2, tn)
    assert tn % (2 * LANES) == 0 and D == 2 * half
    nb = R // tm
    grid_spec = pltpu.PrefetchScalarGridSpec(
        num_scalar_prefetch=2, grid=(F2 // tn, nb),
        in_specs=[pl.BlockSpec((tm, half), lambda j, i, be, nu: (jnp.minimum(i, nu[0] - 1), 0)),
                  pl.BlockSpec((None, D, tn), lambda j, i, be, nu: (be[i], 0, j)),
                  pl.BlockSpec((None, 1, tn), lambda j, i, be, nu: (be[i], 0, j))],
        out_specs=pl.BlockSpec((tm, tn // 2), lambda j, i, be, nu: (i, j)),
        scratch_shapes=[pltpu.VMEM((D, tn), BF16)])
    return pl.pallas_call(
        _moe_up_kernel, grid_spec=grid_spec,
        out_shape=jax.ShapeDtypeStruct((R, F2 // 2), BF16),
        compiler_params=_params("parallel", "arbitrary"),
    )(block_e, n_used, xs, w_gu, b_gu.reshape(E, 1, F2))


def _moe_down_kernel(be_ref, nu_ref, a_ref, w_ref, b_ref, o_ref, wb_ref):
    i = pl.program_id(1)

    @pl.when(_new_expert(be_ref, i))
    def _():
        wb_ref[...] = w_ref[...].astype(BF16)

    @pl.when(i < nu_ref[0])
    def _():
        o_ref[...] = jnp.dot(a_ref[...], wb_ref[...], preferred_element_type=F32) + b_ref[...]

    @pl.when(i >= nu_ref[0])
    def _():
        o_ref[...] = jnp.zeros_like(o_ref)


def _moe_down(act, w_d, b_d, block_e, n_used, *, tm, tn=2048):
    R, F = act.shape
    E, _, D = w_d.shape
    tn = _tile(D, tn)
    nb = R // tm
    grid_spec = pltpu.PrefetchScalarGridSpec(
        num_scalar_prefetch=2, grid=(D // tn, nb),
        in_specs=[pl.BlockSpec((tm, F), lambda j, i, be, nu: (i, 0)),
                  pl.BlockSpec((None, F, tn), lambda j, i, be, nu: (be[i], 0, j)),
                  pl.BlockSpec((None, 1, tn), lambda j, i, be, nu: (be[i], 0, j))],
        out_specs=pl.BlockSpec((tm, tn), lambda j, i, be, nu: (i, j)),
        scratch_shapes=[pltpu.VMEM((F, tn), BF16)])
    return pl.pallas_call(
        _moe_down_kernel, grid_spec=grid_spec,
        out_shape=jax.ShapeDtypeStruct((R, D), F32),
        compiler_params=_params("parallel", "arbitrary"),
    )(block_e, n_used, act, w_d, b_d.reshape(E, 1, D))


def _moe(x, g_moe, w_router, b_router, w_gate_up, b_gate_up, w_down, b_down, g_final, final_norm, *, tm=512):
    T, D = x.shape
    E = w_router.shape[1]
    A = T * TOP_K
    tm = min(tm, A)
    slab_i, slab_g, cnt = _router(x, g_moe, w_router, b_router)
    idx = slab_i[:, :TOP_K]
    rank = slab_i[:, TOP_K:2 * TOP_K]
    counts = cnt[0, :E].astype(jnp.int32)
    padded = (counts + tm - 1) // tm * tm
    pad_ends = jnp.cumsum(padded)
    pad_starts = pad_ends - padded
    dest = pad_starts[idx] + rank
    R = (A + tm - 1) // tm * tm + E * tm
    nb = R // tm
    block_start = jnp.arange(nb, dtype=jnp.int32) * tm
    block_e = jnp.minimum(jnp.sum(block_start[:, None] >= pad_ends[None, :], axis=1), E - 1).astype(jnp.int32)
    n_used = (pad_ends[-1:] // tm).astype(jnp.int32)

    xs = _scatter_norm(x, g_moe, dest, pad_ends, padded, R, tm=tm)
    act = _moe_up(xs, w_gate_up, b_gate_up, block_e, n_used, tm=tm)
    rows = _moe_down(act, w_down, b_down, block_e, n_used, tm=tm)
    return _combine(x, rows, dest, slab_g, g_final, final_norm)


def kernel(x, mem, g_mix, w_in, b_in, w_pool, pool_scale, w_fox_o, w_out, g_mem_q, g_mem_kv, w_mem_q, w_mem_kv,
           w_mem_o, g_moe, w_router, b_router, w_gate_up, b_gate_up, w_down, b_down, g_final):
    B, S, D = x.shape
    T = B * S
    n_mem = mem.shape[1]
    depth, G, C, Do = w_pool.shape
    pool_w = G * C
    fox_w = w_fox_o.shape[1]
    H = w_in.shape[2] - pool_w - 3 * fox_w - 2 * D
    dh = fox_w // H
    off_q, off_f = pool_w, pool_w + 3 * fox_w
    off_gate = off_f + H
    xt = x.reshape(T, D)
    mt = mem.reshape(B * n_mem, D)
    for l in range(depth):
        wl, bl = jnp.swapaxes(w_in[l], 0, 1), b_in[l]
        h, lf_t = _norm(xt, g_mix[l], wl[off_f:off_gate].astype(BF16), bl[off_f:off_gate])
        u = _proj(h, wl, bl, w_t=True, col_off=0, n_cols=off_q, out_dtype=F32)
        qkv = _proj(h, wl, bl, w_t=True, col_off=off_q, n_cols=off_f - off_q, out_dtype=BF16)
        gate_base = off_gate // LANES * LANES
        gate0 = off_gate - gate_base
        gates = _proj(h, wl, bl, w_t=True, col_off=gate_base, mode="sigmoid", out_dtype=BF16)
        c_t = _forget_cumsum(lf_t, B, S)
        att = _fox_attention(qkv, c_t.transpose(0, 2, 1), c_t, B, S, H, dh)
        pp = _pool_mixer(u, w_pool[l], pool_scale[l], gates, gate0, S)
        merged = _proj(att, w_fox_o[l], None, pp, (gates, gate0 + D), mode="merge", out_dtype=BF16)
        xt = _proj(merged, w_out[l], None, xt, mode="residual", out_dtype=F32)
        kv = _proj(_norm(mt, g_mem_kv[l]), w_mem_kv[l], out_dtype=BF16)
        qm = _proj(_norm(xt, g_mem_q[l]), w_mem_q[l], out_dtype=BF16)
        om = _mem_attention(qm, kv, B, S, n_mem)
        xt = _proj(om, w_mem_o[l], None, xt, mode="residual", out_dtype=F32)
        xt = _moe(xt, g_moe[l], w_router[l], b_router[l], w_gate_up[l], b_gate_up[l], w_down[l], b_down[l],
                  g_final, final_norm=l == depth - 1)
    return xt.reshape(B, S, D)
```

```python
import functools

import jax
import jax.numpy as jnp
from jax import lax
from jax.experimental import pallas as pl
from jax.experimental.pallas import tpu as pltpu

F32 = jnp.float32
BF16 = jnp.bfloat16

EPS = 1e-5
POOL_WINDOWS = (2, 4, 8, 16)
POOL_HALO = 16
MEM_HEADS = 4
TOP_K = 4
SWIGLU_LIMIT = 7.0
SWIGLU_ALPHA = 1.702
NEG_BIG = -1e30
LOG2E = 1.4426950408889634

LANES = 128
VMEM_LIMIT_BYTES = 56 * 1024 * 1024


def _tile(dim, pref):
    t = pref
    while t >= 8:
        if dim % t == 0:
            return t
        t //= 2
    return dim


def _params(*sem):
    return pltpu.CompilerParams(dimension_semantics=sem, vmem_limit_bytes=VMEM_LIMIT_BYTES)


def _rms(x, g):
    ms = jnp.mean(x * x, axis=-1, keepdims=True)
    return x * lax.rsqrt(ms + EPS) * g


def _split3(x):
    hi = x.astype(BF16)
    r1 = x - hi.astype(F32)
    mid = r1.astype(BF16)
    lo = (r1 - mid.astype(F32)).astype(BF16)
    return hi, mid, lo


def _norm_kernel(x_ref, g_ref, *refs, with_f):
    h = _rms(x_ref[...], g_ref[...]).astype(BF16)
    if with_f:
        wf_ref, bf_ref, h_ref, lf_ref = refs
        f = lax.dot_general(wf_ref[...], h, (((1,), (1,)), ((), ())), preferred_element_type=F32) + bf_ref[...]
        lf_ref[...] = jnp.minimum(f, 0.0) - jnp.log1p(jnp.exp(-jnp.abs(f)))
    else:
        h_ref, = refs
    h_ref[...] = h


def _norm(x, g, wf_t=None, bf_t=None, *, tm=512):
    M, K = x.shape
    tm = _tile(M, tm)
    with_f = wf_t is not None
    in_specs = [pl.BlockSpec((tm, K), lambda i: (i, 0)), pl.BlockSpec((1, K), lambda i: (0, 0))]
    args = [x, g.reshape(1, K)]
    out_shape = [jax.ShapeDtypeStruct((M, K), BF16)]
    out_specs = [pl.BlockSpec((tm, K), lambda i: (i, 0))]
    if with_f:
        H = wf_t.shape[0]
        in_specs += [pl.BlockSpec((H, K), lambda i: (0, 0)), pl.BlockSpec((H, 1), lambda i: (0, 0))]
        args += [wf_t, bf_t.reshape(H, 1)]
        out_shape.append(jax.ShapeDtypeStruct((H, M), F32))
        out_specs.append(pl.BlockSpec((H, tm), lambda i: (0, i)))
    res = pl.pallas_call(
        functools.partial(_norm_kernel, with_f=with_f),
        grid=(M // tm,),
        in_specs=in_specs, out_specs=out_specs, out_shape=out_shape,
        compiler_params=_params("parallel"),
    )(*args)
    return res if with_f else res[0]


def _lane_window(main_ref, tail_ref, shift):
    if shift == 0:
        return main_ref[...].astype(F32)
    g = jnp.concatenate([main_ref[...], tail_ref[...]], axis=1).astype(F32)
    return pltpu.roll(g, g.shape[1] - shift, axis=1)[:, :main_ref.shape[1]]


def _window_specs(first_col, rows, width, row_of, col_block_of):
    base, shift = first_col // LANES * LANES, first_col % LANES
    assert base % width == 0
    specs = [pl.BlockSpec((rows, width), lambda *g: (row_of(*g), col_block_of(*g) + base // width))]
    if shift:
        specs.append(pl.BlockSpec(
            (rows, LANES), lambda *g: (row_of(*g), (base + (col_block_of(*g) + 1) * width) // LANES)))
    return specs, shift


def _proj_kernel(x_ref, w_ref, *refs, mode, has_bias, gate_shift, w_rows_are_outputs):
    refs = list(refs)
    wb_ref = refs.pop()
    o_ref = refs.pop()
    b_ref = refs.pop(0) if has_bias else None

    @pl.when(pl.program_id(1) == 0)
    def _():
        w = w_ref[...]
        wb_ref[...] = (w.T if w_rows_are_outputs else w).astype(BF16)

    acc = jnp.dot(x_ref[...], wb_ref[...], preferred_element_type=F32)
    if has_bias:
        acc = acc + b_ref[...]
    if mode == "sigmoid":
        acc = jax.nn.sigmoid(acc)
    elif mode == "merge":
        pp_ref, g_ref = refs[0], refs[1]
        acc = pp_ref[...] + _lane_window(g_ref, refs[2] if gate_shift else None, gate_shift) * acc
    elif mode == "residual":
        acc = refs[0][...] + acc
    o_ref[...] = acc.astype(o_ref.dtype)


def _proj(x, w, b=None, extra=None, gate=None, *, w_t=False, col_off=0, n_cols=None, mode="plain", out_dtype,
          tm=1024, tn=512):
    M, K = x.shape
    N = w.shape[0 if w_t else 1] - col_off if n_cols is None else n_cols
    tm = _tile(M, tm)
    tn = min(tn, N)
    while col_off % tn:
        tn //= 2
    assert tn % LANES == 0
    if w_t:
        w_spec = pl.BlockSpec((tn, K), lambda j, i, o=col_off // tn: (j + o, 0))
    else:
        w_spec = pl.BlockSpec((K, tn), lambda j, i, o=col_off // tn: (0, j + o))
    in_specs = [pl.BlockSpec((tm, K), lambda j, i: (i, 0)), w_spec]
    args = [x, w]
    if b is not None:
        in_specs.append(pl.BlockSpec((1, tn), lambda j, i, o=col_off // tn: (0, j + o)))
        args.append(b.reshape(1, -1))
    if extra is not None:
        in_specs.append(pl.BlockSpec((tm, tn), lambda j, i: (i, j)))
        args.append(extra)
    gate_shift = 0
    if gate is not None:
        specs, gate_shift = _window_specs(gate[1], tm, tn, lambda j, i: i, lambda j, i: j)
        in_specs += specs
        args += [gate[0]] * len(specs)
    return pl.pallas_call(
        functools.partial(_proj_kernel, mode=mode, has_bias=b is not None, gate_shift=gate_shift,
                          w_rows_are_outputs=w_t),
        grid=(pl.cdiv(N, tn), M // tm),
        in_specs=in_specs,
        out_specs=pl.BlockSpec((tm, tn), lambda j, i: (i, j)),
        out_shape=jax.ShapeDtypeStruct((M, N), out_dtype),
        scratch_shapes=[pltpu.VMEM((K, tn), BF16)],
        compiler_params=_params("parallel", "arbitrary"),
    )(*args)


def _cumsum_kernel(lf_ref, c_ref):
    S = lf_ref.shape[1]
    row = lax.broadcasted_iota(jnp.int32, (S, S), 0)
    col = lax.broadcasted_iota(jnp.int32, (S, S), 1)
    upper = (row <= col).astype(BF16)
    c = jnp.zeros(lf_ref.shape, F32)
    for part in _split3(lf_ref[...]):
        c = c + jnp.dot(part, upper, preferred_element_type=F32)
    c_ref[...] = c


def _forget_cumsum(lf_t, B, S):
    H = lf_t.shape[0]
    return pl.pallas_call(
        _cumsum_kernel,
        grid=(B,),
        in_specs=[pl.BlockSpec((H, S), lambda b: (0, b))],
        out_specs=pl.BlockSpec((None, H, S), lambda b: (b, 0, 0)),
        out_shape=jax.ShapeDtypeStruct((B, H, S), F32),
        compiler_params=_params("parallel"),
    )(lf_t)


def _pool_kernel(u_ref, halo_ref, w_ref, sc_ref, *refs, seq_len, gate_shift):
    g0_ref, g0_tail_ref = (refs[0], refs[1]) if gate_shift else (refs[0], None)
    o_ref, ext_ref, pooled_ref = refs[-3:]
    g = pl.program_id(0)
    i = pl.program_id(1)
    tp = u_ref.shape[0]
    pos0 = (i * tp) % seq_len
    u = u_ref[...]
    ext_ref[pl.ds(POOL_HALO, tp), :] = u
    ext_ref[pl.ds(0, POOL_HALO), :] = jnp.where(pos0 == 0, 0.0, halo_ref[...])
    pos = pos0 + lax.broadcasted_iota(jnp.int32, (tp, 1), 0)
    for gi, win in enumerate(POOL_WINDOWS):
        @pl.when(g == gi)
        def _(win=win):
            acc = u
            for k in range(1, win):
                acc = acc + ext_ref[pl.ds(POOL_HALO - k, tp), :]
            cnt = jnp.minimum(pos + 1, win).astype(F32)
            pooled_ref[...] = (acc / cnt - u).astype(BF16)
    y = jnp.dot(pooled_ref[...], w_ref[...].astype(BF16), preferred_element_type=F32)
    o_ref[...] = _lane_window(g0_ref, g0_tail_ref, gate_shift) * (y * sc_ref[...])


def _pool_mixer(u, w_pool, scale, gates, gate0_col, seq_len, *, tp=512):
    T = u.shape[0]
    G, C, Do = w_pool.shape
    tp = _tile(seq_len, tp)
    assert tp % POOL_HALO == 0
    hb = tp // POOL_HALO
    gate_specs, gate_shift = _window_specs(gate0_col, tp, Do, lambda g, i: i, lambda g, i: g)
    return pl.pallas_call(
        functools.partial(_pool_kernel, seq_len=seq_len, gate_shift=gate_shift),
        grid=(G, T // tp),
        in_specs=[pl.BlockSpec((tp, C), lambda g, i: (i, g)),
                  pl.BlockSpec((POOL_HALO, C), lambda g, i: (jnp.maximum(i * hb - 1, 0), g)),
                  pl.BlockSpec((None, C, Do), lambda g, i: (g, 0, 0)),
                  pl.BlockSpec((1, Do), lambda g, i: (0, g))] + gate_specs,
        out_specs=pl.BlockSpec((tp, Do), lambda g, i: (i, g)),
        out_shape=jax.ShapeDtypeStruct((T, G * Do), F32),
        scratch_shapes=[pltpu.VMEM((tp + POOL_HALO, C), F32), pltpu.VMEM((tp, C), BF16)],
        compiler_params=_params("parallel", "parallel"),
    )(u, u, w_pool, scale.reshape(1, G * Do), *([gates] * len(gate_specs)))


def _fox_kernel(q_ref, k_ref, v_ref, cq_ref, ck_ref, o_ref, *, scale, dh):
    hg = pl.program_id(1)
    qi = pl.program_id(2)
    tq = q_ref.shape[0]
    n_heads = q_ref.shape[1] // dh
    lane = lax.broadcasted_iota(jnp.int32, cq_ref.shape, 1)
    cq_all = cq_ref[...]
    q = [q_ref[:, n * dh:(n + 1) * dh] for n in range(n_heads)]
    cq2 = [LOG2E * jnp.sum(jnp.where(lane == hg * n_heads + n, cq_all, 0.0), axis=1, keepdims=True)
           for n in range(n_heads)]

    def scores(n, j):
        rows = pl.ds(pl.multiple_of(j * tq, tq), tq)
        s = lax.dot_general(q[n], k_ref[rows, n * dh:(n + 1) * dh], (((1,), (1,)), ((), ())),
                            preferred_element_type=F32) * (scale * LOG2E)
        return s - LOG2E * ck_ref[n:n + 1, rows]

    def update(n, j, s, carry):
        m, l, acc = carry
        m_new = jnp.maximum(m, jnp.max(s, axis=1, keepdims=True) + cq2[n])
        alpha = jnp.exp2(m - m_new)
        p = jnp.exp2(s - (m_new - cq2[n]))
        v = v_ref[pl.ds(pl.multiple_of(j * tq, tq), tq), n * dh:(n + 1) * dh]
        acc = alpha * acc + jnp.dot(p.astype(BF16), v, preferred_element_type=F32)
        return m_new, alpha * l + jnp.sum(p, axis=1, keepdims=True), acc

    init = (jnp.full((tq, 1), NEG_BIG, F32), jnp.zeros((tq, 1), F32), jnp.zeros((tq, dh), F32))
    carries = lax.fori_loop(
        0, qi, lambda j, cs: tuple(update(n, j, scores(n, j), cs[n]) for n in range(n_heads)), (init,) * n_heads)
    row = lax.broadcasted_iota(jnp.int32, (tq, tq), 0)
    col = lax.broadcasted_iota(jnp.int32, (tq, tq), 1)
    for n in range(n_heads):
        s = jnp.where(col <= row, scores(n, qi), NEG_BIG)
        m, l, acc = update(n, qi, s, carries[n])
        o_ref[:, n * dh:(n + 1) * dh] = (acc / l).astype(o_ref.dtype)


def _fox_attention(qkv, c, c_t, B, S, H, dh, *, tq=512, heads_per_step=2):
    T = B * S
    tq = _tile(S, tq)
    nq = S // tq
    hp = heads_per_step if H % heads_per_step == 0 else 1
    ng = H // hp
    w = hp * dh
    return pl.pallas_call(
        functools.partial(_fox_kernel, scale=dh ** -0.5, dh=dh),
        grid=(B, ng, nq),
        in_specs=[pl.BlockSpec((tq, w), lambda b, h, i: (b * nq + i, h)),
                  pl.BlockSpec((S, w), lambda b, h, i: (b, ng + h)),
                  pl.BlockSpec((S, w), lambda b, h, i: (b, 2 * ng + h)),
                  pl.BlockSpec((None, tq, H), lambda b, h, i: (b, i, 0)),
                  pl.BlockSpec((None, None, hp, S), lambda b, h, i: (b, h, 0, 0))],
        out_specs=pl.BlockSpec((tq, w), lambda b, h, i: (b * nq + i, h)),
        out_shape=jax.ShapeDtypeStruct((T, H * dh), BF16),
        compiler_params=_params("parallel", "parallel", "parallel"),
    )(qkv, qkv, qkv, c, c_t.reshape(B, ng, hp, S))


def _mem_attn_kernel(q_ref, kv_ref, o_ref, *, heads):
    width = q_ref.shape[1]
    dh = width // heads
    scale = dh ** -0.5
    for hd in range(heads):
        q = q_ref[:, hd * dh:(hd + 1) * dh]
        k = kv_ref[:, hd * dh:(hd + 1) * dh]
        v = kv_ref[:, width + hd * dh:width + (hd + 1) * dh]
        s = lax.dot_general(q, k, (((1,), (1,)), ((), ())), preferred_element_type=F32) * scale
        p = jnp.exp(s - jnp.max(s, axis=1, keepdims=True))
        p = p / jnp.sum(p, axis=1, keepdims=True)
        o_ref[:, hd * dh:(hd + 1) * dh] = jnp.dot(p.astype(BF16), v, preferred_element_type=F32).astype(o_ref.dtype)


def _mem_attention(q, kv, B, S, n_mem, *, tq=512):
    T, width = q.shape
    tq = _tile(S, tq)
    nq = S // tq
    return pl.pallas_call(
        functools.partial(_mem_attn_kernel, heads=MEM_HEADS),
        grid=(B, nq),
        in_specs=[pl.BlockSpec((tq, width), lambda b, i: (b * nq + i, 0)),
                  pl.BlockSpec((n_mem, 2 * width), lambda b, i: (b, 0))],
        out_specs=pl.BlockSpec((tq, width), lambda b, i: (b * nq + i, 0)),
        out_shape=jax.ShapeDtypeStruct((T, width), BF16),
        compiler_params=_params("parallel", "parallel"),
    )(q, kv)


def _router_kernel(x_ref, g_ref, whi_ref, wlo_ref, b_ref, oi_ref, og_ref, cnt_ref, carry_ref, *, n_experts):
    i = pl.program_id(0)
    tm = x_ref.shape[0]

    @pl.when(i == 0)
    def _():
        carry_ref[...] = jnp.zeros_like(carry_ref)

    h = _rms(x_ref[...], g_ref[...])
    h_hi = h.astype(BF16)
    h_lo = (h - h_hi.astype(F32)).astype(BF16)
    logits = (jnp.dot(h_hi, whi_ref[...], preferred_element_type=F32)
              + jnp.dot(h_hi, wlo_ref[...], preferred_element_type=F32)
              + jnp.dot(h_lo, whi_ref[...], preferred_element_type=F32)) + b_ref[...]
    lane = lax.broadcasted_iota(jnp.int32, (tm, LANES), 1).astype(F32)
    work = jnp.where(lane < n_experts, logits, -jnp.inf)
    vals, idxs = [], []
    for _ in range(TOP_K):
        m = jnp.max(work, axis=1, keepdims=True)
        idx = jnp.min(jnp.where(work == m, lane, float(LANES)), axis=1, keepdims=True)
        vals.append(m)
        idxs.append(idx)
        work = jnp.where(lane == idx, -jnp.inf, work)
    exps = [jnp.exp(v - vals[0]) for v in vals]
    denom = exps[0] + exps[1] + exps[2] + exps[3]
    onehots = [(lane == idx).astype(F32) for idx in idxs]
    chosen = onehots[0] + onehots[1] + onehots[2] + onehots[3]
    row = lax.broadcasted_iota(jnp.int32, (tm, tm), 0)
    col = lax.broadcasted_iota(jnp.int32, (tm, tm), 1)
    before = jnp.dot((col < row).astype(BF16), chosen.astype(BF16), preferred_element_type=F32) + carry_ref[...]
    out_i = jnp.zeros((tm, LANES), F32)
    out_g = jnp.zeros((tm, LANES), F32)
    for k in range(TOP_K):
        rank = jnp.sum(onehots[k] * before, axis=1, keepdims=True)
        out_i = jnp.where(lane == k, idxs[k], out_i)
        out_i = jnp.where(lane == TOP_K + k, rank, out_i)
        out_g = jnp.where(lane == k, exps[k] / denom, out_g)
    oi_ref[...] = out_i.astype(jnp.int32)
    og_ref[...] = out_g
    carry_ref[...] = carry_ref[...] + jnp.sum(chosen, axis=0, keepdims=True)
    cnt_ref[...] = carry_ref[...]


def _router(x, g, w_router, b_router, *, tm=512):
    T, D = x.shape
    E = w_router.shape[1]
    tm = _tile(T, tm)
    w_pad = jnp.zeros((D, LANES), F32).at[:, :E].set(w_router)
    w_hi = w_pad.astype(BF16)
    w_lo = (w_pad - w_hi.astype(F32)).astype(BF16)
    b_pad = jnp.zeros((1, LANES), F32).at[0, :E].set(b_router)
    full = lambda i: (0, 0)
    return pl.pallas_call(
        functools.partial(_router_kernel, n_experts=E),
        grid=(T // tm,),
        in_specs=[pl.BlockSpec((tm, D), lambda i: (i, 0)), pl.BlockSpec((1, D), full),
                  pl.BlockSpec((D, LANES), full), pl.BlockSpec((D, LANES), full), pl.BlockSpec((1, LANES), full)],
        out_specs=[pl.BlockSpec((tm, LANES), lambda i: (i, 0)), pl.BlockSpec((tm, LANES), lambda i: (i, 0)),
                   pl.BlockSpec((1, LANES), full)],
        out_shape=[jax.ShapeDtypeStruct((T, LANES), jnp.int32), jax.ShapeDtypeStruct((T, LANES), F32),
                   jax.ShapeDtypeStruct((1, LANES), F32)],
        scratch_shapes=[pltpu.VMEM((1, LANES), F32)],
        compiler_params=_params("arbitrary"),
    )(x, g.reshape(1, D), w_hi, w_lo, b_pad)


def _row_copy(src_hbm, row, buf, slot, r, sem):
    return pltpu.make_async_copy(src_hbm.at[pl.ds(row, 1)], buf.at[slot, pl.ds(r, 1)], sem.at[slot])


def _start_rows(idx_ref, src_hbm, buf, slot, sem, n_rows):
    def body(r, _):
        _row_copy(src_hbm, idx_ref[0, 0, r], buf, slot, r, sem).start()
        return 0
    lax.fori_loop(0, n_rows, body, 0, unroll=8)


def _wait_rows(src_hbm, buf, slot, sem, n_rows):
    pltpu.make_async_copy(src_hbm.at[pl.ds(0, n_rows)], buf.at[slot], sem.at[slot]).wait()


def _gather_pipeline(idx_ref, nxt_ref, src_hbm, buf, sem, n_rows):
    i = pl.program_id(0)
    n = pl.num_programs(0)
    slot = i % 2

    @pl.when(i == 0)
    def _():
        _start_rows(idx_ref, src_hbm, buf, 0, sem, n_rows)

    @pl.when(i + 1 < n)
    def _():
        _start_rows(nxt_ref, src_hbm, buf, 1 - slot, sem, n_rows)

    _wait_rows(src_hbm, buf, slot, sem, n_rows)
    return slot


def _pack_bf16_pairs(h):
    half = h.shape[1] // 2
    bits = pltpu.bitcast(h.astype(BF16).astype(F32), jnp.uint32)
    return (bits[:, :half] >> 16) | bits[:, half:]


def _unpack_bf16_pairs(u):
    lo = pltpu.bitcast(u << 16, F32).astype(BF16)
    hi = pltpu.bitcast(u & jnp.uint32(0xFFFF0000), F32).astype(BF16)
    return lo, hi


def _scatter_norm_kernel(pe_ref, pd_ref, dest_ref, x_ref, g_ref, xs_hbm, buf, zbuf, sem, zsem, *, tm, n_experts):
    i = pl.program_id(0)
    n = pl.num_programs(0)
    tb = x_ref.shape[0]
    slot = i % 2

    def row_copy(s, t, row):
        return pltpu.make_async_copy(buf.at[s, pl.ds(t, 1)], xs_hbm.at[pl.ds(row, 1)], sem.at[s])

    def wait_slot(s):
        for _ in range(TOP_K):
            pltpu.make_async_copy(buf.at[s], xs_hbm.at[pl.ds(0, tb)], sem.at[s]).wait()

    def zero_group(first_row):
        return pltpu.make_async_copy(zbuf, xs_hbm.at[pl.ds(pl.multiple_of(first_row, tm), tm)], zsem)

    @pl.when(i == 0)
    def _():
        zbuf[...] = jnp.zeros_like(zbuf)
        n_groups = xs_hbm.shape[0] // tm
        used = pe_ref[n_experts - 1] // tm
        for e in range(n_experts):
            @pl.when(pd_ref[e] > 0)
            def _(e=e):
                zero_group(pe_ref[e] - tm).start()
        lax.fori_loop(used, n_groups, lambda gi, c: (zero_group(gi * tm).start(), c)[1], 0)
        for e in range(n_experts):
            @pl.when(pd_ref[e] > 0)
            def _(e=e):
                zero_group(pe_ref[e] - tm).wait()
        lax.fori_loop(used, n_groups, lambda gi, c: (zero_group(gi * tm).wait(), c)[1], 0)

    @pl.when(i >= 2)
    def _():
        wait_slot(slot)

    buf[slot] = _pack_bf16_pairs(_rms(x_ref[...], g_ref[...]))

    def body(t, _):
        for k in range(TOP_K):
            row_copy(slot, t, dest_ref[0, 0, t * TOP_K + k]).start()
        return 0
    lax.fori_loop(0, tb, body, 0, unroll=4)

    @pl.when(i == n - 1)
    def _():
        @pl.when(n >= 2)
        def _():
            wait_slot(1 - slot)
        wait_slot(slot)


def _scatter_norm(x, g, dest, pad_ends, padded, n_rows, *, tm, tb=128):
    T, D = x.shape
    E = pad_ends.shape[0]
    tb = _tile(T, tb)
    nb = T // tb
    grid_spec = pltpu.PrefetchScalarGridSpec(
        num_scalar_prefetch=2, grid=(nb,),
        in_specs=[pl.BlockSpec((1, 1, TOP_K * tb), lambda i, pe, pd: (i, 0, 0), memory_space=pltpu.SMEM),
                  pl.BlockSpec((tb, D), lambda i, pe, pd: (i, 0)),
                  pl.BlockSpec((1, D), lambda i, pe, pd: (0, 0))],
        out_specs=pl.BlockSpec(memory_space=pl.ANY),
        scratch_shapes=[pltpu.VMEM((2, tb, D // 2), jnp.uint32), pltpu.VMEM((tm, D // 2), jnp.uint32),
                        pltpu.SemaphoreType.DMA((2,)), pltpu.SemaphoreType.DMA(())])
    return pl.pallas_call(
        functools.partial(_scatter_norm_kernel, tm=tm, n_experts=E), grid_spec=grid_spec,
        out_shape=jax.ShapeDtypeStruct((n_rows, D // 2), jnp.uint32),
        compiler_params=_params("arbitrary"),
    )(pad_ends, padded, dest.reshape(nb, 1, TOP_K * tb), x, g.reshape(1, D))


def _combine_kernel(idx_ref, nxt_ref, x_ref, gate_ref, g_ref, rows_hbm, o_ref, buf, sem, *, final_norm):
    tb = x_ref.shape[0]
    slot = _gather_pipeline(idx_ref, nxt_ref, rows_hbm, buf, sem, TOP_K * tb)
    y = x_ref[...]
    gates = gate_ref[...]
    for k in range(TOP_K):
        y = y + gates[:, k:k + 1] * buf[slot, pl.ds(k * tb, tb), :]
    o_ref[...] = _rms(y, g_ref[...]) if final_norm else y


def _combine(x, rows, dest, gates_slab, g, final_norm, *, tb=64):
    T, D = x.shape
    tb = _tile(T, tb)
    nb = T // tb
    idx3 = dest.reshape(nb, tb, TOP_K).transpose(0, 2, 1).reshape(nb, 1, TOP_K * tb)
    return pl.pallas_call(
        functools.partial(_combine_kernel, final_norm=final_norm),
        grid=(nb,),
        in_specs=[pl.BlockSpec((1, 1, TOP_K * tb), lambda i: (i, 0, 0), memory_space=pltpu.SMEM),
                  pl.BlockSpec((1, 1, TOP_K * tb), lambda i: (jnp.minimum(i + 1, nb - 1), 0, 0), memory_space=pltpu.SMEM),
                  pl.BlockSpec((tb, D), lambda i: (i, 0)),
                  pl.BlockSpec((tb, LANES), lambda i: (i, 0)),
                  pl.BlockSpec((1, D), lambda i: (0, 0)),
                  pl.BlockSpec(memory_space=pl.ANY)],
        out_specs=pl.BlockSpec((tb, D), lambda i: (i, 0)),
        out_shape=jax.ShapeDtypeStruct((T, D), F32),
        scratch_shapes=[pltpu.VMEM((2, TOP_K * tb, D), F32), pltpu.SemaphoreType.DMA((2,))],
        compiler_params=_params("arbitrary"),
    )(idx3, idx3, x, gates_slab, g.reshape(1, D), rows)


MOE_GROUP_ROWS = 256
MOE_GROUPS_PER_STEP = 5
MOE_UP_COLS = 512
MOE_DOWN_COLS = 512


def _swiglu_tile(lo, hi, wb_ref, b):
    half = lo.shape[1]
    pair = 2 * LANES
    row = lax.broadcasted_iota(jnp.int32, (pair, LANES), 0)
    col = lax.broadcasted_iota(jnp.int32, (pair, LANES), 1)
    pick_even = (row == 2 * col).astype(BF16)
    even = lax.broadcasted_iota(jnp.int32, (1, pair), 1) % 2 == 0
    gu = (jnp.dot(lo, wb_ref[:half, :], preferred_element_type=F32)
          + jnp.dot(hi, wb_ref[half:, :], preferred_element_type=F32)) + b
    outs = []
    for c in range(gu.shape[1] // pair):
        blk = gu[:, c * pair:(c + 1) * pair]
        nxt = pltpu.roll(blk, pair - 1, axis=1)
        gate = jnp.minimum(blk, SWIGLU_LIMIT)
        up = jnp.clip(nxt, -SWIGLU_LIMIT, SWIGLU_LIMIT)
        act = (up + 1.0) * gate * jax.nn.sigmoid(SWIGLU_ALPHA * gate)
        act = jnp.where(even, act, 0.0).astype(BF16)
        outs.append(jnp.dot(act, pick_even, preferred_element_type=F32).astype(BF16))
    return outs[0] if len(outs) == 1 else jnp.concatenate(outs, axis=1)


def _experts_kernel(se_ref, sr_ref, sn_ref, cnt_ref, bgu_ref, bd_ref, xs_hbm, wgu_hbm, wd_hbm, rows_hbm,
                    xs_v, act_v, wu_f, wu_b, wd_f, wd_b, out_v, xs_sem, wu_sem, wd_sem, out_sem, *, tm):
    s = pl.program_id(0)
    n_active = cnt_ref[0]
    tnu, tnd = wu_b.shape[1], wd_b.shape[1]
    n_up, n_down = wgu_hbm.shape[2] // tnu, wd_hbm.shape[2] // tnd

    def wu_copy(step, c, slot):
        return pltpu.make_async_copy(wgu_hbm.at[se_ref[step], :, pl.ds(c * tnu, tnu)], wu_f.at[slot], wu_sem.at[slot])

    def wd_copy(step, c, slot):
        return pltpu.make_async_copy(wd_hbm.at[se_ref[step], :, pl.ds(c * tnd, tnd)], wd_f.at[slot], wd_sem.at[slot])

    def xs_copy(step, g):
        return pltpu.make_async_copy(xs_hbm.at[pl.ds(pl.multiple_of(sr_ref[step] + g * tm, tm), tm)],
                                     xs_v.at[pl.ds(pl.multiple_of(g * tm, tm), tm)], xs_sem)

    def out_copy(slot, row, c):
        return pltpu.make_async_copy(
            out_v.at[slot], rows_hbm.at[pl.ds(pl.multiple_of(row, tm), tm), pl.ds(c * tnd, tnd)], out_sem.at[slot])

    def start_rows(step):
        lax.fori_loop(0, sn_ref[step], lambda g, carry: (xs_copy(step, g).start(), carry)[1], 0)

    @pl.when(s < n_active)
    def _():
        ng = sn_ref[s]
        row0 = sr_ref[s]

        @pl.when(s == 0)
        def _():
            start_rows(0)
            wu_copy(0, 0, 0).start()

        lax.fori_loop(0, ng, lambda g, carry: (xs_copy(s, g).wait(), carry)[1], 0)

        for c in range(n_up):
            slot = c % 2
            wu_copy(s, c, slot).wait()
            if c + 1 < n_up:
                wu_copy(s, c + 1, 1 - slot).start()
            else:
                wd_copy(s, 0, 0).start()
            wu_b[...] = wu_f[slot].astype(BF16)

            def up_group(g, carry, c=c):
                rows = pl.ds(pl.multiple_of(g * tm, tm), tm)
                lo, hi = _unpack_bf16_pairs(xs_v[rows, :])
                act_v[rows, c * (tnu // 2):(c + 1) * (tnu // 2)] = _swiglu_tile(
                    lo, hi, wu_b, bgu_ref[:, c * tnu:(c + 1) * tnu])
                return carry
            lax.fori_loop(0, ng, up_group, 0)

        @pl.when(s + 1 < n_active)
        def _():
            start_rows(s + 1)

        for c in range(n_down):
            slot = c % 2
            wd_copy(s, c, slot).wait()
            if c + 1 < n_down:
                wd_copy(s, c + 1, 1 - slot).start()
            else:
                @pl.when(s + 1 < n_active)
                def _():
                    wu_copy(s + 1, 0, 0).start()
            wd_b[...] = wd_f[slot].astype(BF16)

            def down_group(g, carry, c=c):
                q = c * ng + g
                oslot = q % 2

                @pl.when(q >= 2)
                def _():
                    out_copy(oslot, 0, 0).wait()

                rows = pl.ds(pl.multiple_of(g * tm, tm), tm)
                out_v[oslot] = (jnp.dot(act_v[rows, :], wd_b[...], preferred_element_type=F32)
                                + bd_ref[:, c * tnd:(c + 1) * tnd])
                out_copy(oslot, row0 + g * tm, c).start()
                return carry
            lax.fori_loop(0, ng, down_group, 0)

        out_copy(0, 0, 0).wait()
        out_copy(1, 0, 0).wait()

    @pl.when(s == pl.num_programs(0) - 1)
    def _():
        out_v[0] = jnp.zeros(out_v.shape[1:], out_v.dtype)
        first, last = cnt_ref[1], rows_hbm.shape[0] // tm

        def fill(gi, carry, wait):
            for c in range(n_down):
                cp = out_copy(0, gi * tm, c)
                cp.wait() if wait else cp.start()
            return carry
        lax.fori_loop(first, last, functools.partial(fill, wait=False), 0)
        lax.fori_loop(first, last, functools.partial(fill, wait=True), 0)


def _experts(xs, w_gu, b_gu, w_d, b_d, step_expert, step_row, step_groups, counts, *, tm):
    R, half = xs.shape
    E, D, F2 = w_gu.shape
    F = F2 // 2
    tnu, tnd = min(MOE_UP_COLS, F2), min(MOE_DOWN_COLS, D)
    assert D == 2 * half and F2 % tnu == 0 and D % tnd == 0 and tnu % (2 * LANES) == 0 and D // tnd >= 2
    cap = MOE_GROUPS_PER_STEP * tm
    n_steps = step_expert.shape[0]
    expert_of = lambda s, se, sr, sn, cnt: (se[jnp.minimum(s, cnt[0] - 1)], 0, 0)
    grid_spec = pltpu.PrefetchScalarGridSpec(
        num_scalar_prefetch=4, grid=(n_steps,),
        in_specs=[pl.BlockSpec((None, 1, F2), expert_of), pl.BlockSpec((None, 1, D), expert_of),
                  pl.BlockSpec(memory_space=pl.ANY), pl.BlockSpec(memory_space=pl.ANY),
                  pl.BlockSpec(memory_space=pl.ANY)],
        out_specs=pl.BlockSpec(memory_space=pl.ANY),
        scratch_shapes=[pltpu.VMEM((cap, half), jnp.uint32), pltpu.VMEM((cap, F), BF16),
                        pltpu.VMEM((2, D, tnu), F32), pltpu.VMEM((D, tnu), BF16),
                        pltpu.VMEM((2, F, tnd), F32), pltpu.VMEM((F, tnd), BF16),
                        pltpu.VMEM((2, tm, tnd), F32),
                        pltpu.SemaphoreType.DMA(()), pltpu.SemaphoreType.DMA((2,)),
                        pltpu.SemaphoreType.DMA((2,)), pltpu.SemaphoreType.DMA((2,))])
    return pl.pallas_call(
        functools.partial(_experts_kernel, tm=tm), grid_spec=grid_spec,
        out_shape=jax.ShapeDtypeStruct((R, D), F32),
        compiler_params=_params("arbitrary"),
    )(step_expert, step_row, step_groups, counts, b_gu.reshape(E, 1, F2), b_d.reshape(E, 1, D), xs, w_gu, w_d)


def _moe(x, g_moe, w_router, b_router, w_gate_up, b_gate_up, w_down, b_down, g_final, final_norm):
    T, D = x.shape
    E = w_router.shape[1]
    A = T * TOP_K
    tm = min(MOE_GROUP_ROWS, A)
    slab_i, slab_g, cnt = _router(x, g_moe, w_router, b_router)
    idx = slab_i[:, :TOP_K]
    rank = slab_i[:, TOP_K:2 * TOP_K]
    counts = cnt[0, :E].astype(jnp.int32)
    groups = (counts + tm - 1) // tm
    pad_ends = jnp.cumsum(groups) * tm
    pad_starts = pad_ends - groups * tm
    dest = pad_starts[idx] + rank
    n_groups = (A + tm - 1) // tm + E
    R = n_groups * tm
    per = MOE_GROUPS_PER_STEP
    runs = (groups + per - 1) // per
    run_ends = jnp.cumsum(runs)
    n_steps = (n_groups + per - 1) // per + E
    step = jnp.arange(n_steps, dtype=jnp.int32)
    step_expert = jnp.minimum(jnp.sum(step[:, None] >= run_ends[None, :], axis=1), E - 1).astype(jnp.int32)
    local = step - (run_ends - runs)[step_expert]
    step_row = (pad_starts[step_expert] + local * per * tm).astype(jnp.int32)
    step_groups = jnp.clip(groups[step_expert] - local * per, 0, per).astype(jnp.int32)
    step_counts = jnp.stack([run_ends[-1], pad_ends[-1] // tm]).astype(jnp.int32)

    xs = _scatter_norm(x, g_moe, dest, pad_ends, groups * tm, R, tm=tm)
    rows = _experts(xs, w_gate_up, b_gate_up, w_down, b_down, step_expert, step_row, step_groups, step_counts, tm=tm)
    return _combine(x, rows, dest, slab_g, g_final, final_norm)


def kernel(x, mem, g_mix, w_in, b_in, w_pool, pool_scale, w_fox_o, w_out, g_mem_q, g_mem_kv, w_mem_q, w_mem_kv,
           w_mem_o, g_moe, w_router, b_router, w_gate_up, b_gate_up, w_down, b_down, g_final):
    B, S, D = x.shape
    T = B * S
    n_mem = mem.shape[1]
    depth, G, C, Do = w_pool.shape
    pool_w = G * C
    fox_w = w_fox_o.shape[1]
    H = w_in.shape[2] - pool_w - 3 * fox_w - 2 * D
    dh = fox_w // H
    off_q, off_f = pool_w, pool_w + 3 * fox_w
    off_gate = off_f + H
    xt = x.reshape(T, D)
    mt = mem.reshape(B * n_mem, D)
    for l in range(depth):
        wl, bl = jnp.swapaxes(w_in[l], 0, 1), b_in[l]
        h, lf_t = _norm(xt, g_mix[l], wl[off_f:off_gate].astype(BF16), bl[off_f:off_gate])
        u = _proj(h, wl, bl, w_t=True, col_off=0, n_cols=off_q, out_dtype=F32)
        qkv = _proj(h, wl, bl, w_t=True, col_off=off_q, n_cols=off_f - off_q, out_dtype=BF16)
        gate_base = off_gate // LANES * LANES
        gate0 = off_gate - gate_base
        gates = _proj(h, wl, bl, w_t=True, col_off=gate_base, mode="sigmoid", out_dtype=BF16)
        c_t = _forget_cumsum(lf_t, B, S)
        att = _fox_attention(qkv, c_t.transpose(0, 2, 1), c_t, B, S, H, dh)
        pp = _pool_mixer(u, w_pool[l], pool_scale[l], gates, gate0, S)
        merged = _proj(att, w_fox_o[l], None, pp, (gates, gate0 + D), mode="merge", out_dtype=BF16)
        xt = _proj(merged, w_out[l], None, xt, mode="residual", out_dtype=F32)
        kv = _proj(_norm(mt, g_mem_kv[l]), w_mem_kv[l], out_dtype=BF16)
        qm = _proj(_norm(xt, g_mem_q[l]), w_mem_q[l], out_dtype=BF16)
        om = _mem_attention(qm, kv, B, S, n_mem)
        xt = _proj(om, w_mem_o[l], None, xt, mode="residual", out_dtype=F32)
        xt = _moe(xt, g_moe[l], w_router[l], b_router[l], w_gate_up[l], b_gate_up[l], w_down[l], b_down[l],
                  g_final, final_norm=l == depth - 1)
    return xt.reshape(B, S, D)
```

```python
import functools

import jax
import jax.numpy as jnp
from jax import lax
from jax.experimental import pallas as pl
from jax.experimental.pallas import tpu as pltpu

F32 = jnp.float32
BF16 = jnp.bfloat16

EPS = 1e-5
POOL_WINDOWS = (2, 4, 8, 16)
POOL_HALO = 16
MEM_HEADS = 4
TOP_K = 4
SWIGLU_LIMIT = 7.0
SWIGLU_ALPHA = 1.702
NEG_BIG = -1e30
LOG2E = 1.4426950408889634

LANES = 128
VMEM_LIMIT_BYTES = 56 * 1024 * 1024


def _tile(dim, pref):
    t = pref
    while t >= 8:
        if dim % t == 0:
            return t
        t //= 2
    return dim


def _params(*sem):
    return pltpu.CompilerParams(dimension_semantics=sem, vmem_limit_bytes=VMEM_LIMIT_BYTES)


def _rms(x, g):
    ms = jnp.mean(x * x, axis=-1, keepdims=True)
    return x * lax.rsqrt(ms + EPS) * g


def _split3(x):
    hi = x.astype(BF16)
    r1 = x - hi.astype(F32)
    mid = r1.astype(BF16)
    lo = (r1 - mid.astype(F32)).astype(BF16)
    return hi, mid, lo


def _norm_kernel(x_ref, g_ref, *refs, with_f):
    h = _rms(x_ref[...], g_ref[...]).astype(BF16)
    if with_f:
        wf_ref, bf_ref, h_ref, lf_ref = refs
        f = lax.dot_general(wf_ref[...], h, (((1,), (1,)), ((), ())), preferred_element_type=F32) + bf_ref[...]
        lf_ref[...] = jnp.minimum(f, 0.0) - jnp.log1p(jnp.exp(-jnp.abs(f)))
    else:
        h_ref, = refs
    h_ref[...] = h


def _norm(x, g, wf_t=None, bf_t=None, *, tm=512):
    M, K = x.shape
    tm = _tile(M, tm)
    with_f = wf_t is not None
    in_specs = [pl.BlockSpec((tm, K), lambda i: (i, 0)), pl.BlockSpec((1, K), lambda i: (0, 0))]
    args = [x, g.reshape(1, K)]
    out_shape = [jax.ShapeDtypeStruct((M, K), BF16)]
    out_specs = [pl.BlockSpec((tm, K), lambda i: (i, 0))]
    if with_f:
        H = wf_t.shape[0]
        in_specs += [pl.BlockSpec((H, K), lambda i: (0, 0)), pl.BlockSpec((H, 1), lambda i: (0, 0))]
        args += [wf_t, bf_t.reshape(H, 1)]
        out_shape.append(jax.ShapeDtypeStruct((H, M), F32))
        out_specs.append(pl.BlockSpec((H, tm), lambda i: (0, i)))
    res = pl.pallas_call(
        functools.partial(_norm_kernel, with_f=with_f),
        grid=(M // tm,),
        in_specs=in_specs, out_specs=out_specs, out_shape=out_shape,
        compiler_params=_params("parallel"),
    )(*args)
    return res if with_f else res[0]


def _lane_window(main_ref, tail_ref, shift):
    if shift == 0:
        return main_ref[...].astype(F32)
    g = jnp.concatenate([main_ref[...], tail_ref[...]], axis=1).astype(F32)
    return pltpu.roll(g, g.shape[1] - shift, axis=1)[:, :main_ref.shape[1]]


def _window_specs(first_col, rows, width, row_of, col_block_of):
    base, shift = first_col // LANES * LANES, first_col % LANES
    assert base % width == 0
    specs = [pl.BlockSpec((rows, width), lambda *g: (row_of(*g), col_block_of(*g) + base // width))]
    if shift:
        specs.append(pl.BlockSpec(
            (rows, LANES), lambda *g: (row_of(*g), (base + (col_block_of(*g) + 1) * width) // LANES)))
    return specs, shift


def _proj_kernel(x_ref, w_ref, *refs, mode, has_bias, gate_shift, w_rows_are_outputs):
    refs = list(refs)
    wb_ref = refs.pop()
    o_ref = refs.pop()
    b_ref = refs.pop(0) if has_bias else None

    @pl.when(pl.program_id(1) == 0)
    def _():
        w = w_ref[...]
        wb_ref[...] = (w.T if w_rows_are_outputs else w).astype(BF16)

    acc = jnp.dot(x_ref[...], wb_ref[...], preferred_element_type=F32)
    if has_bias:
        acc = acc + b_ref[...]
    if mode == "sigmoid":
        acc = jax.nn.sigmoid(acc)
    elif mode == "merge":
        pp_ref, g_ref = refs[0], refs[1]
        acc = pp_ref[...] + _lane_window(g_ref, refs[2] if gate_shift else None, gate_shift) * acc
    elif mode == "residual":
        acc = refs[0][...] + acc
    o_ref[...] = acc.astype(o_ref.dtype)


def _proj(x, w, b=None, extra=None, gate=None, *, w_t=False, col_off=0, n_cols=None, mode="plain", out_dtype,
          tm=1024, tn=512):
    M, K = x.shape
    N = w.shape[0 if w_t else 1] - col_off if n_cols is None else n_cols
    tm = _tile(M, tm)
    tn = min(tn, N)
    while col_off % tn:
        tn //= 2
    assert tn % LANES == 0
    if w_t:
        w_spec = pl.BlockSpec((tn, K), lambda j, i, o=col_off // tn: (j + o, 0))
    else:
        w_spec = pl.BlockSpec((K, tn), lambda j, i, o=col_off // tn: (0, j + o))
    in_specs = [pl.BlockSpec((tm, K), lambda j, i: (i, 0)), w_spec]
    args = [x, w]
    if b is not None:
        in_specs.append(pl.BlockSpec((1, tn), lambda j, i, o=col_off // tn: (0, j + o)))
        args.append(b.reshape(1, -1))
    if extra is not None:
        in_specs.append(pl.BlockSpec((tm, tn), lambda j, i: (i, j)))
        args.append(extra)
    gate_shift = 0
    if gate is not None:
        specs, gate_shift = _window_specs(gate[1], tm, tn, lambda j, i: i, lambda j, i: j)
        in_specs += specs
        args += [gate[0]] * len(specs)
    return pl.pallas_call(
        functools.partial(_proj_kernel, mode=mode, has_bias=b is not None, gate_shift=gate_shift,
                          w_rows_are_outputs=w_t),
        grid=(pl.cdiv(N, tn), M // tm),
        in_specs=in_specs,
        out_specs=pl.BlockSpec((tm, tn), lambda j, i: (i, j)),
        out_shape=jax.ShapeDtypeStruct((M, N), out_dtype),
        scratch_shapes=[pltpu.VMEM((K, tn), BF16)],
        compiler_params=_params("parallel", "arbitrary"),
    )(*args)


def _cumsum_kernel(lf_ref, c_ref):
    S = lf_ref.shape[1]
    row = lax.broadcasted_iota(jnp.int32, (S, S), 0)
    col = lax.broadcasted_iota(jnp.int32, (S, S), 1)
    upper = (row <= col).astype(BF16)
    c = jnp.zeros(lf_ref.shape, F32)
    for part in _split3(lf_ref[...]):
        c = c + jnp.dot(part, upper, preferred_element_type=F32)
    c_ref[...] = c


def _forget_cumsum(lf_t, B, S):
    H = lf_t.shape[0]
    return pl.pallas_call(
        _cumsum_kernel,
        grid=(B,),
        in_specs=[pl.BlockSpec((H, S), lambda b: (0, b))],
        out_specs=pl.BlockSpec((None, H, S), lambda b: (b, 0, 0)),
        out_shape=jax.ShapeDtypeStruct((B, H, S), F32),
        compiler_params=_params("parallel"),
    )(lf_t)


def _pool_kernel(u_ref, halo_ref, w_ref, sc_ref, *refs, seq_len, gate_shift):
    g0_ref, g0_tail_ref = (refs[0], refs[1]) if gate_shift else (refs[0], None)
    o_ref, ext_ref, pooled_ref = refs[-3:]
    g = pl.program_id(0)
    i = pl.program_id(1)
    tp = u_ref.shape[0]
    pos0 = (i * tp) % seq_len
    u = u_ref[...]
    ext_ref[pl.ds(POOL_HALO, tp), :] = u
    ext_ref[pl.ds(0, POOL_HALO), :] = jnp.where(pos0 == 0, 0.0, halo_ref[...])
    pos = pos0 + lax.broadcasted_iota(jnp.int32, (tp, 1), 0)
    for gi, win in enumerate(POOL_WINDOWS):
        @pl.when(g == gi)
        def _(win=win):
            acc = u
            for k in range(1, win):
                acc = acc + ext_ref[pl.ds(POOL_HALO - k, tp), :]
            cnt = jnp.minimum(pos + 1, win).astype(F32)
            pooled_ref[...] = (acc / cnt - u).astype(BF16)
    y = jnp.dot(pooled_ref[...], w_ref[...].astype(BF16), preferred_element_type=F32)
    o_ref[...] = _lane_window(g0_ref, g0_tail_ref, gate_shift) * (y * sc_ref[...])


def _pool_mixer(u, w_pool, scale, gates, gate0_col, seq_len, *, tp=512):
    T = u.shape[0]
    G, C, Do = w_pool.shape
    tp = _tile(seq_len, tp)
    assert tp % POOL_HALO == 0
    hb = tp // POOL_HALO
    gate_specs, gate_shift = _window_specs(gate0_col, tp, Do, lambda g, i: i, lambda g, i: g)
    return pl.pallas_call(
        functools.partial(_pool_kernel, seq_len=seq_len, gate_shift=gate_shift),
        grid=(G, T // tp),
        in_specs=[pl.BlockSpec((tp, C), lambda g, i: (i, g)),
                  pl.BlockSpec((POOL_HALO, C), lambda g, i: (jnp.maximum(i * hb - 1, 0), g)),
                  pl.BlockSpec((None, C, Do), lambda g, i: (g, 0, 0)),
                  pl.BlockSpec((1, Do), lambda g, i: (0, g))] + gate_specs,
        out_specs=pl.BlockSpec((tp, Do), lambda g, i: (i, g)),
        out_shape=jax.ShapeDtypeStruct((T, G * Do), F32),
        scratch_shapes=[pltpu.VMEM((tp + POOL_HALO, C), F32), pltpu.VMEM((tp, C), BF16)],
        compiler_params=_params("parallel", "parallel"),
    )(u, u, w_pool, scale.reshape(1, G * Do), *([gates] * len(gate_specs)))


def _fox_kernel(q_ref, k_ref, v_ref, cq_ref, ck_ref, o_ref, *, scale, dh):
    hg = pl.program_id(1)
    qi = pl.program_id(2)
    tq = q_ref.shape[0]
    n_heads = q_ref.shape[1] // dh
    lane = lax.broadcasted_iota(jnp.int32, cq_ref.shape, 1)
    cq_all = cq_ref[...]
    q = [q_ref[:, n * dh:(n + 1) * dh] for n in range(n_heads)]
    cq2 = [LOG2E * jnp.sum(jnp.where(lane == hg * n_heads + n, cq_all, 0.0), axis=1, keepdims=True)
           for n in range(n_heads)]

    def scores(n, j):
        rows = pl.ds(pl.multiple_of(j * tq, tq), tq)
        s = lax.dot_general(q[n], k_ref[rows, n * dh:(n + 1) * dh], (((1,), (1,)), ((), ())),
                            preferred_element_type=F32) * (scale * LOG2E)
        return s - LOG2E * ck_ref[n:n + 1, rows]

    def update(n, j, s, carry):
        m, l, acc = carry
        m_new = jnp.maximum(m, jnp.max(s, axis=1, keepdims=True) + cq2[n])
        alpha = jnp.exp2(m - m_new)
        p = jnp.exp2(s - (m_new - cq2[n]))
        v = v_ref[pl.ds(pl.multiple_of(j * tq, tq), tq), n * dh:(n + 1) * dh]
        acc = alpha * acc + jnp.dot(p.astype(BF16), v, preferred_element_type=F32)
        return m_new, alpha * l + jnp.sum(p, axis=1, keepdims=True), acc

    init = (jnp.full((tq, 1), NEG_BIG, F32), jnp.zeros((tq, 1), F32), jnp.zeros((tq, dh), F32))
    carries = lax.fori_loop(
        0, qi, lambda j, cs: tuple(update(n, j, scores(n, j), cs[n]) for n in range(n_heads)), (init,) * n_heads)
    row = lax.broadcasted_iota(jnp.int32, (tq, tq), 0)
    col = lax.broadcasted_iota(jnp.int32, (tq, tq), 1)
    for n in range(n_heads):
        s = jnp.where(col <= row, scores(n, qi), NEG_BIG)
        m, l, acc = update(n, qi, s, carries[n])
        o_ref[:, n * dh:(n + 1) * dh] = (acc / l).astype(o_ref.dtype)


def _fox_attention(qkv, c, c_t, B, S, H, dh, *, tq=512, heads_per_step=2):
    T = B * S
    tq = _tile(S, tq)
    nq = S // tq
    hp = heads_per_step if H % heads_per_step == 0 else 1
    ng = H // hp
    w = hp * dh
    return pl.pallas_call(
        functools.partial(_fox_kernel, scale=dh ** -0.5, dh=dh),
        grid=(B, ng, nq),
        in_specs=[pl.BlockSpec((tq, w), lambda b, h, i: (b * nq + i, h)),
                  pl.BlockSpec((S, w), lambda b, h, i: (b, ng + h)),
                  pl.BlockSpec((S, w), lambda b, h, i: (b, 2 * ng + h)),
                  pl.BlockSpec((None, tq, H), lambda b, h, i: (b, i, 0)),
                  pl.BlockSpec((None, None, hp, S), lambda b, h, i: (b, h, 0, 0))],
        out_specs=pl.BlockSpec((tq, w), lambda b, h, i: (b * nq + i, h)),
        out_shape=jax.ShapeDtypeStruct((T, H * dh), BF16),
        compiler_params=_params("parallel", "parallel", "parallel"),
    )(qkv, qkv, qkv, c, c_t.reshape(B, ng, hp, S))


def _mem_attn_kernel(q_ref, kv_ref, o_ref, *, heads):
    width = q_ref.shape[1]
    dh = width // heads
    scale = dh ** -0.5
    for hd in range(heads):
        q = q_ref[:, hd * dh:(hd + 1) * dh]
        k = kv_ref[:, hd * dh:(hd + 1) * dh]
        v = kv_ref[:, width + hd * dh:width + (hd + 1) * dh]
        s = lax.dot_general(q, k, (((1,), (1,)), ((), ())), preferred_element_type=F32) * scale
        p = jnp.exp(s - jnp.max(s, axis=1, keepdims=True))
        p = p / jnp.sum(p, axis=1, keepdims=True)
        o_ref[:, hd * dh:(hd + 1) * dh] = jnp.dot(p.astype(BF16), v, preferred_element_type=F32).astype(o_ref.dtype)


def _mem_attention(q, kv, B, S, n_mem, *, tq=512):
    T, width = q.shape
    tq = _tile(S, tq)
    nq = S // tq
    return pl.pallas_call(
        functools.partial(_mem_attn_kernel, heads=MEM_HEADS),
        grid=(B, nq),
        in_specs=[pl.BlockSpec((tq, width), lambda b, i: (b * nq + i, 0)),
                  pl.BlockSpec((n_mem, 2 * width), lambda b, i: (b, 0))],
        out_specs=pl.BlockSpec((tq, width), lambda b, i: (b * nq + i, 0)),
        out_shape=jax.ShapeDtypeStruct((T, width), BF16),
        compiler_params=_params("parallel", "parallel"),
    )(q, kv)


def _router_kernel(x_ref, g_ref, whi_ref, wlo_ref, b_ref, oi_ref, og_ref, cnt_ref, carry_ref, *, n_experts):
    i = pl.program_id(0)
    tm = x_ref.shape[0]

    @pl.when(i == 0)
    def _():
        carry_ref[...] = jnp.zeros_like(carry_ref)

    h = _rms(x_ref[...], g_ref[...])
    h_hi = h.astype(BF16)
    h_lo = (h - h_hi.astype(F32)).astype(BF16)
    logits = (jnp.dot(h_hi, whi_ref[...], preferred_element_type=F32)
              + jnp.dot(h_hi, wlo_ref[...], preferred_element_type=F32)
              + jnp.dot(h_lo, whi_ref[...], preferred_element_type=F32)) + b_ref[...]
    lane = lax.broadcasted_iota(jnp.int32, (tm, LANES), 1).astype(F32)
    work = jnp.where(lane < n_experts, logits, -jnp.inf)
    vals, idxs = [], []
    for _ in range(TOP_K):
        m = jnp.max(work, axis=1, keepdims=True)
        idx = jnp.min(jnp.where(work == m, lane, float(LANES)), axis=1, keepdims=True)
        vals.append(m)
        idxs.append(idx)
        work = jnp.where(lane == idx, -jnp.inf, work)
    exps = [jnp.exp(v - vals[0]) for v in vals]
    denom = exps[0] + exps[1] + exps[2] + exps[3]
    onehots = [(lane == idx).astype(F32) for idx in idxs]
    chosen = onehots[0] + onehots[1] + onehots[2] + onehots[3]
    row = lax.broadcasted_iota(jnp.int32, (tm, tm), 0)
    col = lax.broadcasted_iota(jnp.int32, (tm, tm), 1)
    before = jnp.dot((col < row).astype(BF16), chosen.astype(BF16), preferred_element_type=F32) + carry_ref[...]
    out_i = jnp.zeros((tm, LANES), F32)
    out_g = jnp.zeros((tm, LANES), F32)
    for k in range(TOP_K):
        rank = jnp.sum(onehots[k] * before, axis=1, keepdims=True)
        out_i = jnp.where(lane == k, idxs[k], out_i)
        out_i = jnp.where(lane == TOP_K + k, rank, out_i)
        out_g = jnp.where(lane == k, exps[k] / denom, out_g)
    oi_ref[...] = out_i.astype(jnp.int32)
    og_ref[...] = out_g
    carry_ref[...] = carry_ref[...] + jnp.sum(chosen, axis=0, keepdims=True)
    cnt_ref[...] = carry_ref[...]


def _router(x, g, w_router, b_router, *, tm=512):
    T, D = x.shape
    E = w_router.shape[1]
    tm = _tile(T, tm)
    w_pad = jnp.zeros((D, LANES), F32).at[:, :E].set(w_router)
    w_hi = w_pad.astype(BF16)
    w_lo = (w_pad - w_hi.astype(F32)).astype(BF16)
    b_pad = jnp.zeros((1, LANES), F32).at[0, :E].set(b_router)
    full = lambda i: (0, 0)
    return pl.pallas_call(
        functools.partial(_router_kernel, n_experts=E),
        grid=(T // tm,),
        in_specs=[pl.BlockSpec((tm, D), lambda i: (i, 0)), pl.BlockSpec((1, D), full),
                  pl.BlockSpec((D, LANES), full), pl.BlockSpec((D, LANES), full), pl.BlockSpec((1, LANES), full)],
        out_specs=[pl.BlockSpec((tm, LANES), lambda i: (i, 0)), pl.BlockSpec((tm, LANES), lambda i: (i, 0)),
                   pl.BlockSpec((1, LANES), full)],
        out_shape=[jax.ShapeDtypeStruct((T, LANES), jnp.int32), jax.ShapeDtypeStruct((T, LANES), F32),
                   jax.ShapeDtypeStruct((1, LANES), F32)],
        scratch_shapes=[pltpu.VMEM((1, LANES), F32)],
        compiler_params=_params("arbitrary"),
    )(x, g.reshape(1, D), w_hi, w_lo, b_pad)


def _row_copy(src_hbm, row, buf, slot, r, sem):
    return pltpu.make_async_copy(src_hbm.at[pl.ds(row, 1)], buf.at[slot, pl.ds(r, 1)], sem.at[slot])


def _start_rows(idx_ref, src_hbm, buf, slot, sem, n_rows):
    def body(r, _):
        _row_copy(src_hbm, idx_ref[0, 0, r], buf, slot, r, sem).start()
        return 0
    lax.fori_loop(0, n_rows, body, 0, unroll=8)


def _wait_rows(src_hbm, buf, slot, sem, n_rows):
    pltpu.make_async_copy(src_hbm.at[pl.ds(0, n_rows)], buf.at[slot], sem.at[slot]).wait()


def _gather_pipeline(idx_ref, nxt_ref, src_hbm, buf, sem, n_rows):
    i = pl.program_id(0)
    n = pl.num_programs(0)
    slot = i % 2

    @pl.when(i == 0)
    def _():
        _start_rows(idx_ref, src_hbm, buf, 0, sem, n_rows)

    @pl.when(i + 1 < n)
    def _():
        _start_rows(nxt_ref, src_hbm, buf, 1 - slot, sem, n_rows)

    _wait_rows(src_hbm, buf, slot, sem, n_rows)
    return slot


def _pack_bf16_pairs(h):
    half = h.shape[1] // 2
    bits = pltpu.bitcast(h.astype(BF16).astype(F32), jnp.uint32)
    return (bits[:, :half] >> 16) | bits[:, half:]


def _unpack_bf16_pairs(u):
    lo = pltpu.bitcast(u << 16, F32).astype(BF16)
    hi = pltpu.bitcast(u & jnp.uint32(0xFFFF0000), F32).astype(BF16)
    return lo, hi


def _scatter_norm_kernel(pe_ref, pd_ref, dest_ref, x_ref, g_ref, xs_hbm, buf, zbuf, sem, zsem, *, tm, n_experts):
    i = pl.program_id(0)
    n = pl.num_programs(0)
    tb = x_ref.shape[0]
    slot = i % 2

    def row_copy(s, t, row):
        return pltpu.make_async_copy(buf.at[s, pl.ds(t, 1)], xs_hbm.at[pl.ds(row, 1)], sem.at[s])

    def wait_slot(s):
        for _ in range(TOP_K):
            pltpu.make_async_copy(buf.at[s], xs_hbm.at[pl.ds(0, tb)], sem.at[s]).wait()

    def zero_group(first_row):
        return pltpu.make_async_copy(zbuf, xs_hbm.at[pl.ds(pl.multiple_of(first_row, tm), tm)], zsem)

    @pl.when(i == 0)
    def _():
        zbuf[...] = jnp.zeros_like(zbuf)
        n_groups = xs_hbm.shape[0] // tm
        used = pe_ref[n_experts - 1] // tm
        for e in range(n_experts):
            @pl.when(pd_ref[e] > 0)
            def _(e=e):
                zero_group(pe_ref[e] - tm).start()
        lax.fori_loop(used, n_groups, lambda gi, c: (zero_group(gi * tm).start(), c)[1], 0)
        for e in range(n_experts):
            @pl.when(pd_ref[e] > 0)
            def _(e=e):
                zero_group(pe_ref[e] - tm).wait()
        lax.fori_loop(used, n_groups, lambda gi, c: (zero_group(gi * tm).wait(), c)[1], 0)

    @pl.when(i >= 2)
    def _():
        wait_slot(slot)

    buf[slot] = _pack_bf16_pairs(_rms(x_ref[...], g_ref[...]))

    def body(t, _):
        for k in range(TOP_K):
            row_copy(slot, t, dest_ref[0, 0, t * TOP_K + k]).start()
        return 0
    lax.fori_loop(0, tb, body, 0, unroll=4)

    @pl.when(i == n - 1)
    def _():
        @pl.when(n >= 2)
        def _():
            wait_slot(1 - slot)
        wait_slot(slot)


def _scatter_norm(x, g, dest, pad_ends, padded, n_rows, *, tm, tb=128):
    T, D = x.shape
    E = pad_ends.shape[0]
    tb = _tile(T, tb)
    nb = T // tb
    grid_spec = pltpu.PrefetchScalarGridSpec(
        num_scalar_prefetch=2, grid=(nb,),
        in_specs=[pl.BlockSpec((1, 1, TOP_K * tb), lambda i, pe, pd: (i, 0, 0), memory_space=pltpu.SMEM),
                  pl.BlockSpec((tb, D), lambda i, pe, pd: (i, 0)),
                  pl.BlockSpec((1, D), lambda i, pe, pd: (0, 0))],
        out_specs=pl.BlockSpec(memory_space=pl.ANY),
        scratch_shapes=[pltpu.VMEM((2, tb, D // 2), jnp.uint32), pltpu.VMEM((tm, D // 2), jnp.uint32),
                        pltpu.SemaphoreType.DMA((2,)), pltpu.SemaphoreType.DMA(())])
    return pl.pallas_call(
        functools.partial(_scatter_norm_kernel, tm=tm, n_experts=E), grid_spec=grid_spec,
        out_shape=jax.ShapeDtypeStruct((n_rows, D // 2), jnp.uint32),
        compiler_params=_params("arbitrary"),
    )(pad_ends, padded, dest.reshape(nb, 1, TOP_K * tb), x, g.reshape(1, D))


def _combine_kernel(idx_ref, nxt_ref, x_ref, gate_ref, g_ref, rows_hbm, o_ref, buf, sem, *, final_norm):
    tb = x_ref.shape[0]
    slot = _gather_pipeline(idx_ref, nxt_ref, rows_hbm, buf, sem, TOP_K * tb)
    y = x_ref[...]
    gates = gate_ref[...]
    for k in range(TOP_K):
        y = y + gates[:, k:k + 1] * buf[slot, pl.ds(k * tb, tb), :]
    o_ref[...] = _rms(y, g_ref[...]) if final_norm else y


def _combine(x, rows, dest, gates_slab, g, final_norm, *, tb=64):
    T, D = x.shape
    tb = _tile(T, tb)
    nb = T // tb
    idx3 = dest.reshape(nb, tb, TOP_K).transpose(0, 2, 1).reshape(nb, 1, TOP_K * tb)
    return pl.pallas_call(
        functools.partial(_combine_kernel, final_norm=final_norm),
        grid=(nb,),
        in_specs=[pl.BlockSpec((1, 1, TOP_K * tb), lambda i: (i, 0, 0), memory_space=pltpu.SMEM),
                  pl.BlockSpec((1, 1, TOP_K * tb), lambda i: (jnp.minimum(i + 1, nb - 1), 0, 0), memory_space=pltpu.SMEM),
                  pl.BlockSpec((tb, D), lambda i: (i, 0)),
                  pl.BlockSpec((tb, LANES), lambda i: (i, 0)),
                  pl.BlockSpec((1, D), lambda i: (0, 0)),
                  pl.BlockSpec(memory_space=pl.ANY)],
        out_specs=pl.BlockSpec((tb, D), lambda i: (i, 0)),
        out_shape=jax.ShapeDtypeStruct((T, D), F32),
        scratch_shapes=[pltpu.VMEM((2, TOP_K * tb, D), F32), pltpu.SemaphoreType.DMA((2,))],
        compiler_params=_params("arbitrary"),
    )(idx3, idx3, x, gates_slab, g.reshape(1, D), rows)


MOE_GROUP_ROWS = 256
MOE_GROUPS_PER_STEP = 5
MOE_BLOCK_GROUPS = 4
MOE_UP_COLS = 512
MOE_DOWN_COLS = 512


def _swiglu_tile(lo, hi, wb_ref, b):
    half = lo.shape[1]
    pair = 2 * LANES
    row = lax.broadcasted_iota(jnp.int32, (pair, LANES), 0)
    col = lax.broadcasted_iota(jnp.int32, (pair, LANES), 1)
    pick_even = (row == 2 * col).astype(BF16)
    even = lax.broadcasted_iota(jnp.int32, (1, pair), 1) % 2 == 0
    gu = (jnp.dot(lo, wb_ref[:half, :], preferred_element_type=F32)
          + jnp.dot(hi, wb_ref[half:, :], preferred_element_type=F32)) + b
    outs = []
    for c in range(gu.shape[1] // pair):
        blk = gu[:, c * pair:(c + 1) * pair]
        nxt = pltpu.roll(blk, pair - 1, axis=1)
        gate = jnp.minimum(blk, SWIGLU_LIMIT)
        up = jnp.clip(nxt, -SWIGLU_LIMIT, SWIGLU_LIMIT)
        act = (up + 1.0) * gate * jax.nn.sigmoid(SWIGLU_ALPHA * gate)
        act = jnp.where(even, act, 0.0).astype(BF16)
        outs.append(jnp.dot(act, pick_even, preferred_element_type=F32).astype(BF16))
    return outs[0] if len(outs) == 1 else jnp.concatenate(outs, axis=1)


def _experts_kernel(se_ref, sr_ref, sn_ref, cnt_ref, bgu_ref, bd_ref, xs_hbm, wgu_hbm, wd_hbm, rows_hbm,
                    xs_v, act_v, wu_f, wu_b, wd_f, wd_b, out_v, xs_sem, wu_sem, wd_sem, out_sem, *, tm):
    s = pl.program_id(0)
    n_active = cnt_ref[0]
    tnu, tnd = wu_b.shape[1], wd_b.shape[1]
    n_up, n_down = wgu_hbm.shape[2] // tnu, wd_hbm.shape[2] // tnd

    def wu_copy(step, c, slot):
        return pltpu.make_async_copy(wgu_hbm.at[se_ref[step], :, pl.ds(c * tnu, tnu)], wu_f.at[slot], wu_sem.at[slot])

    def wd_copy(step, c, slot):
        return pltpu.make_async_copy(wd_hbm.at[se_ref[step], :, pl.ds(c * tnd, tnd)], wd_f.at[slot], wd_sem.at[slot])

    def xs_copy(step, g):
        return pltpu.make_async_copy(xs_hbm.at[pl.ds(pl.multiple_of(sr_ref[step] + g * tm, tm), tm)],
                                     xs_v.at[pl.ds(pl.multiple_of(g * tm, tm), tm)], xs_sem)

    def out_copy(slot, row, c):
        return pltpu.make_async_copy(
            out_v.at[slot], rows_hbm.at[pl.ds(pl.multiple_of(row, tm), tm), pl.ds(c * tnd, tnd)], out_sem.at[slot])

    def start_rows(step):
        lax.fori_loop(0, sn_ref[step], lambda g, carry: (xs_copy(step, g).start(), carry)[1], 0)

    def for_row_blocks(n_groups, fn, carry):
        n_blocks = n_groups // MOE_BLOCK_GROUPS
        carry = lax.fori_loop(0, n_blocks, lambda b, cr: fn(b * MOE_BLOCK_GROUPS, MOE_BLOCK_GROUPS, cr), carry)
        return lax.fori_loop(n_blocks * MOE_BLOCK_GROUPS, n_groups, lambda g, cr: fn(g, 1, cr), carry)

    @pl.when(s < n_active)
    def _():
        ng = sn_ref[s]
        row0 = sr_ref[s]

        @pl.when(s == 0)
        def _():
            start_rows(0)
            wu_copy(0, 0, 0).start()

        lax.fori_loop(0, ng, lambda g, carry: (xs_copy(s, g).wait(), carry)[1], 0)

        for c in range(n_up):
            slot = c % 2
            wu_copy(s, c, slot).wait()
            if c + 1 < n_up:
                wu_copy(s, c + 1, 1 - slot).start()
            else:
                wd_copy(s, 0, 0).start()
            wu_b[...] = wu_f[slot].astype(BF16)

            def up_rows(g0, n_g, carry, c=c):
                rows = pl.ds(pl.multiple_of(g0 * tm, tm), n_g * tm)
                lo, hi = _unpack_bf16_pairs(xs_v[rows, :])
                act_v[rows, c * (tnu // 2):(c + 1) * (tnu // 2)] = _swiglu_tile(
                    lo, hi, wu_b, bgu_ref[:, c * tnu:(c + 1) * tnu])
                return carry
            for_row_blocks(ng, up_rows, 0)

        @pl.when(s + 1 < n_active)
        def _():
            start_rows(s + 1)

        def drain(pending):
            for k in range(MOE_BLOCK_GROUPS):
                @pl.when(pending[k] == 1)
                def _(k=k):
                    out_copy(k, 0, 0).wait()

        pending = (jnp.int32(0),) * MOE_BLOCK_GROUPS
        for c in range(n_down):
            slot = c % 2
            wd_copy(s, c, slot).wait()
            if c + 1 < n_down:
                wd_copy(s, c + 1, 1 - slot).start()
            else:
                @pl.when(s + 1 < n_active)
                def _():
                    wu_copy(s + 1, 0, 0).start()
            wd_b[...] = wd_f[slot].astype(BF16)

            def down_rows(g0, n_g, pending, c=c):
                rows = pl.ds(pl.multiple_of(g0 * tm, tm), n_g * tm)
                o = (jnp.dot(act_v[rows, :], wd_b[...], preferred_element_type=F32)
                     + bd_ref[:, c * tnd:(c + 1) * tnd])
                drain(pending)
                for k in range(n_g):
                    out_v[k] = o[k * tm:(k + 1) * tm]
                    out_copy(k, row0 + (g0 + k) * tm, c).start()
                return tuple(jnp.int32(1 if k < n_g else 0) for k in range(MOE_BLOCK_GROUPS))

            pending = for_row_blocks(ng, down_rows, pending)
        drain(pending)

    @pl.when(s == pl.num_programs(0) - 1)
    def _():
        out_v[0] = jnp.zeros(out_v.shape[1:], out_v.dtype)
        first, last = cnt_ref[1], rows_hbm.shape[0] // tm

        def fill(gi, carry, wait):
            for c in range(n_down):
                cp = out_copy(0, gi * tm, c)
                cp.wait() if wait else cp.start()
            return carry
        lax.fori_loop(first, last, functools.partial(fill, wait=False), 0)
        lax.fori_loop(first, last, functools.partial(fill, wait=True), 0)


def _experts(xs, w_gu, b_gu, w_d, b_d, step_expert, step_row, step_groups, counts, *, tm):
    R, half = xs.shape
    E, D, F2 = w_gu.shape
    F = F2 // 2
    tnu, tnd = min(MOE_UP_COLS, F2), min(MOE_DOWN_COLS, D)
    assert D == 2 * half and F2 % tnu == 0 and D % tnd == 0 and tnu % (2 * LANES) == 0 and D // tnd >= 2
    cap = MOE_GROUPS_PER_STEP * tm
    n_steps = step_expert.shape[0]
    expert_of = lambda s, se, sr, sn, cnt: (se[jnp.minimum(s, cnt[0] - 1)], 0, 0)
    grid_spec = pltpu.PrefetchScalarGridSpec(
        num_scalar_prefetch=4, grid=(n_steps,),
        in_specs=[pl.BlockSpec((None, 1, F2), expert_of), pl.BlockSpec((None, 1, D), expert_of),
                  pl.BlockSpec(memory_space=pl.ANY), pl.BlockSpec(memory_space=pl.ANY),
                  pl.BlockSpec(memory_space=pl.ANY)],
        out_specs=pl.BlockSpec(memory_space=pl.ANY),
        scratch_shapes=[pltpu.VMEM((cap, half), jnp.uint32), pltpu.VMEM((cap, F), BF16),
                        pltpu.VMEM((2, D, tnu), F32), pltpu.VMEM((D, tnu), BF16),
                        pltpu.VMEM((2, F, tnd), F32), pltpu.VMEM((F, tnd), BF16),
                        pltpu.VMEM((MOE_BLOCK_GROUPS, tm, tnd), F32),
                        pltpu.SemaphoreType.DMA(()), pltpu.SemaphoreType.DMA((2,)),
                        pltpu.SemaphoreType.DMA((2,)), pltpu.SemaphoreType.DMA((MOE_BLOCK_GROUPS,))])
    return pl.pallas_call(
        functools.partial(_experts_kernel, tm=tm), grid_spec=grid_spec,
        out_shape=jax.ShapeDtypeStruct((R, D), F32),
        compiler_params=_params("arbitrary"),
    )(step_expert, step_row, step_groups, counts, b_gu.reshape(E, 1, F2), b_d.reshape(E, 1, D), xs, w_gu, w_d)


def _moe(x, g_moe, w_router, b_router, w_gate_up, b_gate_up, w_down, b_down, g_final, final_norm):
    T, D = x.shape
    E = w_router.shape[1]
    A = T * TOP_K
    tm = min(MOE_GROUP_ROWS, A)
    slab_i, slab_g, cnt = _router(x, g_moe, w_router, b_router)
    idx = slab_i[:, :TOP_K]
    rank = slab_i[:, TOP_K:2 * TOP_K]
    counts = cnt[0, :E].astype(jnp.int32)
    groups = (counts + tm - 1) // tm
    pad_ends = jnp.cumsum(groups) * tm
    pad_starts = pad_ends - groups * tm
    dest = pad_starts[idx] + rank
    n_groups = (A + tm - 1) // tm + E
    R = n_groups * tm
    per = MOE_GROUPS_PER_STEP
    runs = (groups + per - 1) // per
    run_ends = jnp.cumsum(runs)
    n_steps = (n_groups + per - 1) // per + E
    step = jnp.arange(n_steps, dtype=jnp.int32)
    step_expert = jnp.minimum(jnp.sum(step[:, None] >= run_ends[None, :], axis=1), E - 1).astype(jnp.int32)
    local = step - (run_ends - runs)[step_expert]
    step_row = (pad_starts[step_expert] + local * per * tm).astype(jnp.int32)
    step_groups = jnp.clip(groups[step_expert] - local * per, 0, per).astype(jnp.int32)
    step_counts = jnp.stack([run_ends[-1], pad_ends[-1] // tm]).astype(jnp.int32)

    xs = _scatter_norm(x, g_moe, dest, pad_ends, groups * tm, R, tm=tm)
    rows = _experts(xs, w_gate_up, b_gate_up, w_down, b_down, step_expert, step_row, step_groups, step_counts, tm=tm)
    return _combine(x, rows, dest, slab_g, g_final, final_norm)


def kernel(x, mem, g_mix, w_in, b_in, w_pool, pool_scale, w_fox_o, w_out, g_mem_q, g_mem_kv, w_mem_q, w_mem_kv,
           w_mem_o, g_moe, w_router, b_router, w_gate_up, b_gate_up, w_down, b_down, g_final):
    B, S, D = x.shape
    T = B * S
    n_mem = mem.shape[1]
    depth, G, C, Do = w_pool.shape
    pool_w = G * C
    fox_w = w_fox_o.shape[1]
    H = w_in.shape[2] - pool_w - 3 * fox_w - 2 * D
    dh = fox_w // H
    off_q, off_f = pool_w, pool_w + 3 * fox_w
    off_gate = off_f + H
    xt = x.reshape(T, D)
    mt = mem.reshape(B * n_mem, D)
    for l in range(depth):
        wl, bl = jnp.swapaxes(w_in[l], 0, 1), b_in[l]
        h, lf_t = _norm(xt, g_mix[l], wl[off_f:off_gate].astype(BF16), bl[off_f:off_gate])
        u = _proj(h, wl, bl, w_t=True, col_off=0, n_cols=off_q, out_dtype=F32)
        qkv = _proj(h, wl, bl, w_t=True, col_off=off_q, n_cols=off_f - off_q, out_dtype=BF16)
        gate_base = off_gate // LANES * LANES
        gate0 = off_gate - gate_base
        gates = _proj(h, wl, bl, w_t=True, col_off=gate_base, mode="sigmoid", out_dtype=BF16)
        c_t = _forget_cumsum(lf_t, B, S)
        att = _fox_attention(qkv, c_t.transpose(0, 2, 1), c_t, B, S, H, dh)
        pp = _pool_mixer(u, w_pool[l], pool_scale[l], gates, gate0, S)
        merged = _proj(att, w_fox_o[l], None, pp, (gates, gate0 + D), mode="merge", out_dtype=BF16)
        xt = _proj(merged, w_out[l], None, xt, mode="residual", out_dtype=F32)
        kv = _proj(_norm(mt, g_mem_kv[l]), w_mem_kv[l], out_dtype=BF16)
        qm = _proj(_norm(xt, g_mem_q[l]), w_mem_q[l], out_dtype=BF16)
        om = _mem_attention(qm, kv, B, S, n_mem)
        xt = _proj(om, w_mem_o[l], None, xt, mode="residual", out_dtype=F32)
        xt = _moe(xt, g_moe[l], w_router[l], b_router[l], w_gate_up[l], b_gate_up[l], w_down[l], b_down[l],
                  g_final, final_norm=l == depth - 1)
    return xt.reshape(B, S, D)
```

```python
import functools

import jax
import jax.numpy as jnp
from jax import lax
from jax.experimental import pallas as pl
from jax.experimental.pallas import tpu as pltpu

F32 = jnp.float32
BF16 = jnp.bfloat16

EPS = 1e-5
POOL_WINDOWS = (2, 4, 8, 16)
POOL_HALO = 16
MEM_HEADS = 4
TOP_K = 4
SWIGLU_LIMIT = 7.0
SWIGLU_ALPHA = 1.702
NEG_BIG = -1e30
LOG2E = 1.4426950408889634

LANES = 128
VMEM_LIMIT_BYTES = 56 * 1024 * 1024


def _tile(dim, pref):
    t = pref
    while t >= 8:
        if dim % t == 0:
            return t
        t //= 2
    return dim


def _params(*sem):
    return pltpu.CompilerParams(dimension_semantics=sem, vmem_limit_bytes=VMEM_LIMIT_BYTES)


def _rms(x, g):
    ms = jnp.mean(x * x, axis=-1, keepdims=True)
    return x * lax.rsqrt(ms + EPS) * g


def _split3(x):
    hi = x.astype(BF16)
    r1 = x - hi.astype(F32)
    mid = r1.astype(BF16)
    lo = (r1 - mid.astype(F32)).astype(BF16)
    return hi, mid, lo


def _norm_kernel(x_ref, g_ref, *refs, with_f):
    h = _rms(x_ref[...], g_ref[...]).astype(BF16)
    if with_f:
        wf_ref, bf_ref, h_ref, lf_ref = refs
        f = lax.dot_general(wf_ref[...], h, (((1,), (1,)), ((), ())), preferred_element_type=F32) + bf_ref[...]
        lf_ref[...] = jnp.minimum(f, 0.0) - jnp.log1p(jnp.exp(-jnp.abs(f)))
    else:
        h_ref, = refs
    h_ref[...] = h


def _norm(x, g, wf_t=None, bf_t=None, *, tm=512):
    M, K = x.shape
    tm = _tile(M, tm)
    with_f = wf_t is not None
    in_specs = [pl.BlockSpec((tm, K), lambda i: (i, 0)), pl.BlockSpec((1, K), lambda i: (0, 0))]
    args = [x, g.reshape(1, K)]
    out_shape = [jax.ShapeDtypeStruct((M, K), BF16)]
    out_specs = [pl.BlockSpec((tm, K), lambda i: (i, 0))]
    if with_f:
        H = wf_t.shape[0]
        in_specs += [pl.BlockSpec((H, K), lambda i: (0, 0)), pl.BlockSpec((H, 1), lambda i: (0, 0))]
        args += [wf_t, bf_t.reshape(H, 1)]
        out_shape.append(jax.ShapeDtypeStruct((H, M), F32))
        out_specs.append(pl.BlockSpec((H, tm), lambda i: (0, i)))
    res = pl.pallas_call(
        functools.partial(_norm_kernel, with_f=with_f),
        grid=(M // tm,),
        in_specs=in_specs, out_specs=out_specs, out_shape=out_shape,
        compiler_params=_params("parallel"),
    )(*args)
    return res if with_f else res[0]


def _lane_window(main_ref, tail_ref, shift):
    if shift == 0:
        return main_ref[...].astype(F32)
    g = jnp.concatenate([main_ref[...], tail_ref[...]], axis=1).astype(F32)
    return pltpu.roll(g, g.shape[1] - shift, axis=1)[:, :main_ref.shape[1]]


def _window_specs(first_col, rows, width, row_of, col_block_of):
    base, shift = first_col // LANES * LANES, first_col % LANES
    assert base % width == 0
    specs = [pl.BlockSpec((rows, width), lambda *g: (row_of(*g), col_block_of(*g) + base // width))]
    if shift:
        specs.append(pl.BlockSpec(
            (rows, LANES), lambda *g: (row_of(*g), (base + (col_block_of(*g) + 1) * width) // LANES)))
    return specs, shift


def _proj_kernel(x_ref, w_ref, *refs, mode, has_bias, gate_shift, w_rows_are_outputs):
    refs = list(refs)
    wb_ref = refs.pop()
    o_ref = refs.pop()
    b_ref = refs.pop(0) if has_bias else None

    @pl.when(pl.program_id(1) == 0)
    def _():
        w = w_ref[...]
        wb_ref[...] = (w.T if w_rows_are_outputs else w).astype(BF16)

    acc = jnp.dot(x_ref[...], wb_ref[...], preferred_element_type=F32)
    if has_bias:
        acc = acc + b_ref[...]
    if mode == "sigmoid":
        acc = jax.nn.sigmoid(acc)
    elif mode == "merge":
        pp_ref, g_ref = refs[0], refs[1]
        acc = pp_ref[...] + _lane_window(g_ref, refs[2] if gate_shift else None, gate_shift) * acc
    elif mode == "residual":
        acc = refs[0][...] + acc
    o_ref[...] = acc.astype(o_ref.dtype)


def _proj(x, w, b=None, extra=None, gate=None, *, w_t=False, col_off=0, n_cols=None, mode="plain", out_dtype,
          tm=1024, tn=512):
    M, K = x.shape
    N = w.shape[0 if w_t else 1] - col_off if n_cols is None else n_cols
    tm = _tile(M, tm)
    tn = min(tn, N)
    while col_off % tn:
        tn //= 2
    assert tn % LANES == 0
    if w_t:
        w_spec = pl.BlockSpec((tn, K), lambda j, i, o=col_off // tn: (j + o, 0))
    else:
        w_spec = pl.BlockSpec((K, tn), lambda j, i, o=col_off // tn: (0, j + o))
    in_specs = [pl.BlockSpec((tm, K), lambda j, i: (i, 0)), w_spec]
    args = [x, w]
    if b is not None:
        in_specs.append(pl.BlockSpec((1, tn), lambda j, i, o=col_off // tn: (0, j + o)))
        args.append(b.reshape(1, -1))
    if extra is not None:
        in_specs.append(pl.BlockSpec((tm, tn), lambda j, i: (i, j)))
        args.append(extra)
    gate_shift = 0
    if gate is not None:
        specs, gate_shift = _window_specs(gate[1], tm, tn, lambda j, i: i, lambda j, i: j)
        in_specs += specs
        args += [gate[0]] * len(specs)
    return pl.pallas_call(
        functools.partial(_proj_kernel, mode=mode, has_bias=b is not None, gate_shift=gate_shift,
                          w_rows_are_outputs=w_t),
        grid=(pl.cdiv(N, tn), M // tm),
        in_specs=in_specs,
        out_specs=pl.BlockSpec((tm, tn), lambda j, i: (i, j)),
        out_shape=jax.ShapeDtypeStruct((M, N), out_dtype),
        scratch_shapes=[pltpu.VMEM((K, tn), BF16)],
        compiler_params=_params("parallel", "arbitrary"),
    )(*args)


def _cumsum_kernel(lf_ref, c_ref):
    S = lf_ref.shape[1]
    row = lax.broadcasted_iota(jnp.int32, (S, S), 0)
    col = lax.broadcasted_iota(jnp.int32, (S, S), 1)
    upper = (row <= col).astype(BF16)
    c = jnp.zeros(lf_ref.shape, F32)
    for part in _split3(lf_ref[...]):
        c = c + jnp.dot(part, upper, preferred_element_type=F32)
    c_ref[...] = c


def _forget_cumsum(lf_t, B, S):
    H = lf_t.shape[0]
    return pl.pallas_call(
        _cumsum_kernel,
        grid=(B,),
        in_specs=[pl.BlockSpec((H, S), lambda b: (0, b))],
        out_specs=pl.BlockSpec((None, H, S), lambda b: (b, 0, 0)),
        out_shape=jax.ShapeDtypeStruct((B, H, S), F32),
        compiler_params=_params("parallel"),
    )(lf_t)


def _pool_kernel(u_ref, halo_ref, w_ref, sc_ref, *refs, seq_len, gate_shift):
    g0_ref, g0_tail_ref = (refs[0], refs[1]) if gate_shift else (refs[0], None)
    o_ref, ext_ref, pooled_ref = refs[-3:]
    g = pl.program_id(0)
    i = pl.program_id(1)
    tp = u_ref.shape[0]
    pos0 = (i * tp) % seq_len
    u = u_ref[...]
    ext_ref[pl.ds(POOL_HALO, tp), :] = u
    ext_ref[pl.ds(0, POOL_HALO), :] = jnp.where(pos0 == 0, 0.0, halo_ref[...])
    pos = pos0 + lax.broadcasted_iota(jnp.int32, (tp, 1), 0)
    for gi, win in enumerate(POOL_WINDOWS):
        @pl.when(g == gi)
        def _(win=win):
            acc = u
            for k in range(1, win):
                acc = acc + ext_ref[pl.ds(POOL_HALO - k, tp), :]
            cnt = jnp.minimum(pos + 1, win).astype(F32)
            pooled_ref[...] = (acc / cnt - u).astype(BF16)
    y = jnp.dot(pooled_ref[...], w_ref[...].astype(BF16), preferred_element_type=F32)
    o_ref[...] = _lane_window(g0_ref, g0_tail_ref, gate_shift) * (y * sc_ref[...])


def _pool_mixer(u, w_pool, scale, gates, gate0_col, seq_len, *, tp=512):
    T = u.shape[0]
    G, C, Do = w_pool.shape
    tp = _tile(seq_len, tp)
    assert tp % POOL_HALO == 0
    hb = tp // POOL_HALO
    gate_specs, gate_shift = _window_specs(gate0_col, tp, Do, lambda g, i: i, lambda g, i: g)
    return pl.pallas_call(
        functools.partial(_pool_kernel, seq_len=seq_len, gate_shift=gate_shift),
        grid=(G, T // tp),
        in_specs=[pl.BlockSpec((tp, C), lambda g, i: (i, g)),
                  pl.BlockSpec((POOL_HALO, C), lambda g, i: (jnp.maximum(i * hb - 1, 0), g)),
                  pl.BlockSpec((None, C, Do), lambda g, i: (g, 0, 0)),
                  pl.BlockSpec((1, Do), lambda g, i: (0, g))] + gate_specs,
        out_specs=pl.BlockSpec((tp, Do), lambda g, i: (i, g)),
        out_shape=jax.ShapeDtypeStruct((T, G * Do), F32),
        scratch_shapes=[pltpu.VMEM((tp + POOL_HALO, C), F32), pltpu.VMEM((tp, C), BF16)],
        compiler_params=_params("parallel", "parallel"),
    )(u, u, w_pool, scale.reshape(1, G * Do), *([gates] * len(gate_specs)))


def _fox_kernel(q_ref, k_ref, v_ref, cq_ref, ck_ref, o_ref, *, scale, dh):
    hg = pl.program_id(1)
    qi = pl.program_id(2)
    tq = q_ref.shape[0]
    n_heads = q_ref.shape[1] // dh
    lane = lax.broadcasted_iota(jnp.int32, cq_ref.shape, 1)
    cq_all = cq_ref[...]
    q = [q_ref[:, n * dh:(n + 1) * dh] for n in range(n_heads)]
    cq2 = [LOG2E * jnp.sum(jnp.where(lane == hg * n_heads + n, cq_all, 0.0), axis=1, keepdims=True)
           for n in range(n_heads)]

    def scores(n, j):
        rows = pl.ds(pl.multiple_of(j * tq, tq), tq)
        s = lax.dot_general(q[n], k_ref[rows, n * dh:(n + 1) * dh], (((1,), (1,)), ((), ())),
                            preferred_element_type=F32) * (scale * LOG2E)
        return s - LOG2E * ck_ref[n:n + 1, rows]

    def update(n, j, s, carry):
        m, l, acc = carry
        m_new = jnp.maximum(m, jnp.max(s, axis=1, keepdims=True) + cq2[n])
        alpha = jnp.exp2(m - m_new)
        p = jnp.exp2(s - (m_new - cq2[n]))
        v = v_ref[pl.ds(pl.multiple_of(j * tq, tq), tq), n * dh:(n + 1) * dh]
        acc = alpha * acc + jnp.dot(p.astype(BF16), v, preferred_element_type=F32)
        return m_new, alpha * l + jnp.sum(p, axis=1, keepdims=True), acc

    init = (jnp.full((tq, 1), NEG_BIG, F32), jnp.zeros((tq, 1), F32), jnp.zeros((tq, dh), F32))
    carries = lax.fori_loop(
        0, qi, lambda j, cs: tuple(update(n, j, scores(n, j), cs[n]) for n in range(n_heads)), (init,) * n_heads)
    row = lax.broadcasted_iota(jnp.int32, (tq, tq), 0)
    col = lax.broadcasted_iota(jnp.int32, (tq, tq), 1)
    for n in range(n_heads):
        s = jnp.where(col <= row, scores(n, qi), NEG_BIG)
        m, l, acc = update(n, qi, s, carries[n])
        o_ref[:, n * dh:(n + 1) * dh] = (acc / l).astype(o_ref.dtype)


def _fox_attention(qkv, c, c_t, B, S, H, dh, *, tq=512, heads_per_step=2):
    T = B * S
    tq = _tile(S, tq)
    nq = S // tq
    hp = heads_per_step if H % heads_per_step == 0 else 1
    ng = H // hp
    w = hp * dh
    return pl.pallas_call(
        functools.partial(_fox_kernel, scale=dh ** -0.5, dh=dh),
        grid=(B, ng, nq),
        in_specs=[pl.BlockSpec((tq, w), lambda b, h, i: (b * nq + i, h)),
                  pl.BlockSpec((S, w), lambda b, h, i: (b, ng + h)),
                  pl.BlockSpec((S, w), lambda b, h, i: (b, 2 * ng + h)),
                  pl.BlockSpec((None, tq, H), lambda b, h, i: (b, i, 0)),
                  pl.BlockSpec((None, None, hp, S), lambda b, h, i: (b, h, 0, 0))],
        out_specs=pl.BlockSpec((tq, w), lambda b, h, i: (b * nq + i, h)),
        out_shape=jax.ShapeDtypeStruct((T, H * dh), BF16),
        compiler_params=_params("parallel", "parallel", "parallel"),
    )(qkv, qkv, qkv, c, c_t.reshape(B, ng, hp, S))


def _mem_attn_kernel(q_ref, kv_ref, o_ref, *, heads):
    width = q_ref.shape[1]
    dh = width // heads
    scale = dh ** -0.5
    for hd in range(heads):
        q = q_ref[:, hd * dh:(hd + 1) * dh]
        k = kv_ref[:, hd * dh:(hd + 1) * dh]
        v = kv_ref[:, width + hd * dh:width + (hd + 1) * dh]
        s = lax.dot_general(q, k, (((1,), (1,)), ((), ())), preferred_element_type=F32) * scale
        p = jnp.exp(s - jnp.max(s, axis=1, keepdims=True))
        p = p / jnp.sum(p, axis=1, keepdims=True)
        o_ref[:, hd * dh:(hd + 1) * dh] = jnp.dot(p.astype(BF16), v, preferred_element_type=F32).astype(o_ref.dtype)


def _mem_attention(q, kv, B, S, n_mem, *, tq=512):
    T, width = q.shape
    tq = _tile(S, tq)
    nq = S // tq
    return pl.pallas_call(
        functools.partial(_mem_attn_kernel, heads=MEM_HEADS),
        grid=(B, nq),
        in_specs=[pl.BlockSpec((tq, width), lambda b, i: (b * nq + i, 0)),
                  pl.BlockSpec((n_mem, 2 * width), lambda b, i: (b, 0))],
        out_specs=pl.BlockSpec((tq, width), lambda b, i: (b * nq + i, 0)),
        out_shape=jax.ShapeDtypeStruct((T, width), BF16),
        compiler_params=_params("parallel", "parallel"),
    )(q, kv)


def _router_kernel(x_ref, g_ref, whi_ref, wlo_ref, b_ref, oi_ref, og_ref, cnt_ref, carry_ref, *, n_experts):
    i = pl.program_id(0)
    tm = x_ref.shape[0]

    @pl.when(i == 0)
    def _():
        carry_ref[...] = jnp.zeros_like(carry_ref)

    h = _rms(x_ref[...], g_ref[...])
    h_hi = h.astype(BF16)
    h_lo = (h - h_hi.astype(F32)).astype(BF16)
    logits = (jnp.dot(h_hi, whi_ref[...], preferred_element_type=F32)
              + jnp.dot(h_hi, wlo_ref[...], preferred_element_type=F32)
              + jnp.dot(h_lo, whi_ref[...], preferred_element_type=F32)) + b_ref[...]
    lane = lax.broadcasted_iota(jnp.int32, (tm, LANES), 1).astype(F32)
    work = jnp.where(lane < n_experts, logits, -jnp.inf)
    vals, idxs = [], []
    for _ in range(TOP_K):
        m = jnp.max(work, axis=1, keepdims=True)
        idx = jnp.min(jnp.where(work == m, lane, float(LANES)), axis=1, keepdims=True)
        vals.append(m)
        idxs.append(idx)
        work = jnp.where(lane == idx, -jnp.inf, work)
    exps = [jnp.exp(v - vals[0]) for v in vals]
    denom = exps[0] + exps[1] + exps[2] + exps[3]
    onehots = [(lane == idx).astype(F32) for idx in idxs]
    chosen = onehots[0] + onehots[1] + onehots[2] + onehots[3]
    row = lax.broadcasted_iota(jnp.int32, (tm, tm), 0)
    col = lax.broadcasted_iota(jnp.int32, (tm, tm), 1)
    before = jnp.dot((col < row).astype(BF16), chosen.astype(BF16), preferred_element_type=F32) + carry_ref[...]
    out_i = jnp.zeros((tm, LANES), F32)
    out_g = jnp.zeros((tm, LANES), F32)
    for k in range(TOP_K):
        rank = jnp.sum(onehots[k] * before, axis=1, keepdims=True)
        out_i = jnp.where(lane == k, idxs[k], out_i)
        out_i = jnp.where(lane == TOP_K + k, rank, out_i)
        out_g = jnp.where(lane == k, exps[k] / denom, out_g)
    oi_ref[...] = out_i.astype(jnp.int32)
    og_ref[...] = out_g
    carry_ref[...] = carry_ref[...] + jnp.sum(chosen, axis=0, keepdims=True)
    cnt_ref[...] = carry_ref[...]


def _router(x, g, w_router, b_router, *, tm=512):
    T, D = x.shape
    E = w_router.shape[1]
    tm = _tile(T, tm)
    w_pad = jnp.zeros((D, LANES), F32).at[:, :E].set(w_router)
    w_hi = w_pad.astype(BF16)
    w_lo = (w_pad - w_hi.astype(F32)).astype(BF16)
    b_pad = jnp.zeros((1, LANES), F32).at[0, :E].set(b_router)
    full = lambda i: (0, 0)
    return pl.pallas_call(
        functools.partial(_router_kernel, n_experts=E),
        grid=(T // tm,),
        in_specs=[pl.BlockSpec((tm, D), lambda i: (i, 0)), pl.BlockSpec((1, D), full),
                  pl.BlockSpec((D, LANES), full), pl.BlockSpec((D, LANES), full), pl.BlockSpec((1, LANES), full)],
        out_specs=[pl.BlockSpec((tm, LANES), lambda i: (i, 0)), pl.BlockSpec((tm, LANES), lambda i: (i, 0)),
                   pl.BlockSpec((1, LANES), full)],
        out_shape=[jax.ShapeDtypeStruct((T, LANES), jnp.int32), jax.ShapeDtypeStruct((T, LANES), F32),
                   jax.ShapeDtypeStruct((1, LANES), F32)],
        scratch_shapes=[pltpu.VMEM((1, LANES), F32)],
        compiler_params=_params("arbitrary"),
    )(x, g.reshape(1, D), w_hi, w_lo, b_pad)


def _row_copy(src_hbm, row, buf, slot, r, sem):
    return pltpu.make_async_copy(src_hbm.at[pl.ds(row, 1)], buf.at[slot, pl.ds(r, 1)], sem.at[slot])


def _start_rows(idx_ref, src_hbm, buf, slot, sem, n_rows):
    def body(r, _):
        _row_copy(src_hbm, idx_ref[0, 0, r], buf, slot, r, sem).start()
        return 0
    lax.fori_loop(0, n_rows, body, 0, unroll=8)


def _wait_rows(src_hbm, buf, slot, sem, n_rows):
    pltpu.make_async_copy(src_hbm.at[pl.ds(0, n_rows)], buf.at[slot], sem.at[slot]).wait()


def _gather_pipeline(idx_ref, nxt_ref, src_hbm, buf, sem, n_rows):
    i = pl.program_id(0)
    n = pl.num_programs(0)
    slot = i % 2

    @pl.when(i == 0)
    def _():
        _start_rows(idx_ref, src_hbm, buf, 0, sem, n_rows)

    @pl.when(i + 1 < n)
    def _():
        _start_rows(nxt_ref, src_hbm, buf, 1 - slot, sem, n_rows)

    _wait_rows(src_hbm, buf, slot, sem, n_rows)
    return slot


def _pack_bf16_pairs(h):
    half = h.shape[1] // 2
    bits = pltpu.bitcast(h.astype(BF16).astype(F32), jnp.uint32)
    return (bits[:, :half] >> 16) | bits[:, half:]


def _unpack_bf16_pairs(u):
    lo = pltpu.bitcast(u << 16, F32).astype(BF16)
    hi = pltpu.bitcast(u & jnp.uint32(0xFFFF0000), F32).astype(BF16)
    return lo, hi


def _scatter_norm_kernel(pe_ref, pd_ref, dest_ref, x_ref, g_ref, xs_hbm, buf, zbuf, sem, zsem, *, tm, n_experts):
    i = pl.program_id(0)
    n = pl.num_programs(0)
    tb = x_ref.shape[0]
    slot = i % 2

    def row_copy(s, t, row):
        return pltpu.make_async_copy(buf.at[s, pl.ds(t, 1)], xs_hbm.at[pl.ds(row, 1)], sem.at[s])

    def wait_slot(s):
        for _ in range(TOP_K):
            pltpu.make_async_copy(buf.at[s], xs_hbm.at[pl.ds(0, tb)], sem.at[s]).wait()

    def zero_group(first_row):
        return pltpu.make_async_copy(zbuf, xs_hbm.at[pl.ds(pl.multiple_of(first_row, tm), tm)], zsem)

    @pl.when(i == 0)
    def _():
        zbuf[...] = jnp.zeros_like(zbuf)
        n_groups = xs_hbm.shape[0] // tm
        used = pe_ref[n_experts - 1] // tm
        for e in range(n_experts):
            @pl.when(pd_ref[e] > 0)
            def _(e=e):
                zero_group(pe_ref[e] - tm).start()
        lax.fori_loop(used, n_groups, lambda gi, c: (zero_group(gi * tm).start(), c)[1], 0)
        for e in range(n_experts):
            @pl.when(pd_ref[e] > 0)
            def _(e=e):
                zero_group(pe_ref[e] - tm).wait()
        lax.fori_loop(used, n_groups, lambda gi, c: (zero_group(gi * tm).wait(), c)[1], 0)

    @pl.when(i >= 2)
    def _():
        wait_slot(slot)

    buf[slot] = _pack_bf16_pairs(_rms(x_ref[...], g_ref[...]))

    def body(t, _):
        for k in range(TOP_K):
            row_copy(slot, t, dest_ref[0, 0, t * TOP_K + k]).start()
        return 0
    lax.fori_loop(0, tb, body, 0, unroll=4)

    @pl.when(i == n - 1)
    def _():
        @pl.when(n >= 2)
        def _():
            wait_slot(1 - slot)
        wait_slot(slot)


def _scatter_norm(x, g, dest, pad_ends, padded, n_rows, *, tm, tb=128):
    T, D = x.shape
    E = pad_ends.shape[0]
    tb = _tile(T, tb)
    nb = T // tb
    grid_spec = pltpu.PrefetchScalarGridSpec(
        num_scalar_prefetch=2, grid=(nb,),
        in_specs=[pl.BlockSpec((1, 1, TOP_K * tb), lambda i, pe, pd: (i, 0, 0), memory_space=pltpu.SMEM),
                  pl.BlockSpec((tb, D), lambda i, pe, pd: (i, 0)),
                  pl.BlockSpec((1, D), lambda i, pe, pd: (0, 0))],
        out_specs=pl.BlockSpec(memory_space=pl.ANY),
        scratch_shapes=[pltpu.VMEM((2, tb, D // 2), jnp.uint32), pltpu.VMEM((tm, D // 2), jnp.uint32),
                        pltpu.SemaphoreType.DMA((2,)), pltpu.SemaphoreType.DMA(())])
    return pl.pallas_call(
        functools.partial(_scatter_norm_kernel, tm=tm, n_experts=E), grid_spec=grid_spec,
        out_shape=jax.ShapeDtypeStruct((n_rows, D // 2), jnp.uint32),
        compiler_params=_params("arbitrary"),
    )(pad_ends, padded, dest.reshape(nb, 1, TOP_K * tb), x, g.reshape(1, D))


def _combine_kernel(idx_ref, nxt_ref, x_ref, gate_ref, g_ref, rows_hbm, o_ref, buf, sem, *, final_norm):
    tb = x_ref.shape[0]
    slot = _gather_pipeline(idx_ref, nxt_ref, rows_hbm, buf, sem, TOP_K * tb)
    y = x_ref[...]
    gates = gate_ref[...]
    for k in range(TOP_K):
        y = y + gates[:, k:k + 1] * buf[slot, pl.ds(k * tb, tb), :]
    o_ref[...] = _rms(y, g_ref[...]) if final_norm else y


def _combine(x, rows, dest, gates_slab, g, final_norm, *, tb=64):
    T, D = x.shape
    tb = _tile(T, tb)
    nb = T // tb
    idx3 = dest.reshape(nb, tb, TOP_K).transpose(0, 2, 1).reshape(nb, 1, TOP_K * tb)
    return pl.pallas_call(
        functools.partial(_combine_kernel, final_norm=final_norm),
        grid=(nb,),
        in_specs=[pl.BlockSpec((1, 1, TOP_K * tb), lambda i: (i, 0, 0), memory_space=pltpu.SMEM),
                  pl.BlockSpec((1, 1, TOP_K * tb), lambda i: (jnp.minimum(i + 1, nb - 1), 0, 0), memory_space=pltpu.SMEM),
                  pl.BlockSpec((tb, D), lambda i: (i, 0)),
                  pl.BlockSpec((tb, LANES), lambda i: (i, 0)),
                  pl.BlockSpec((1, D), lambda i: (0, 0)),
                  pl.BlockSpec(memory_space=pl.ANY)],
        out_specs=pl.BlockSpec((tb, D), lambda i: (i, 0)),
        out_shape=jax.ShapeDtypeStruct((T, D), F32),
        scratch_shapes=[pltpu.VMEM((2, TOP_K * tb, D), F32), pltpu.SemaphoreType.DMA((2,))],
        compiler_params=_params("arbitrary"),
    )(idx3, idx3, x, gates_slab, g.reshape(1, D), rows)


MOE_GROUP_ROWS = 256
MOE_GROUPS_PER_STEP = 5
MOE_BLOCK_GROUPS = 4
MOE_UP_COLS = 512
MOE_DOWN_COLS = 512


def _swiglu_tile(lo, hi, wb_ref, b):
    half = lo.shape[1]
    pair = 2 * LANES
    row = lax.broadcasted_iota(jnp.int32, (pair, LANES), 0)
    col = lax.broadcasted_iota(jnp.int32, (pair, LANES), 1)
    pick_even = (row == 2 * col).astype(BF16)
    even = lax.broadcasted_iota(jnp.int32, (1, pair), 1) % 2 == 0
    gu = (jnp.dot(lo, wb_ref[:half, :], preferred_element_type=F32)
          + jnp.dot(hi, wb_ref[half:, :], preferred_element_type=F32)) + b
    outs = []
    for c in range(gu.shape[1] // pair):
        blk = gu[:, c * pair:(c + 1) * pair]
        nxt = pltpu.roll(blk, pair - 1, axis=1)
        gate = jnp.minimum(blk, SWIGLU_LIMIT)
        up = jnp.clip(nxt, -SWIGLU_LIMIT, SWIGLU_LIMIT)
        act = (up + 1.0) * gate * jax.nn.sigmoid(SWIGLU_ALPHA * gate)
        act = jnp.where(even, act, 0.0).astype(BF16)
        outs.append(jnp.dot(act, pick_even, preferred_element_type=F32).astype(BF16))
    return outs[0] if len(outs) == 1 else jnp.concatenate(outs, axis=1)


def _experts_kernel(se_ref, sr_ref, sn_ref, cnt_ref, bgu_ref, bd_ref, xs_hbm, wgu_hbm, wd_hbm, rows_hbm,
                    xs_v, act_v, wu_f, wu_b, wd_f, wd_b, out_v, xs_sem, wu_sem, wd_sem, out_sem, *, tm):
    s = pl.program_id(0)
    n_active = cnt_ref[0]
    tnu, tnd = wu_b.shape[1], wd_b.shape[1]
    n_up, n_down = wgu_hbm.shape[2] // tnu, wd_hbm.shape[2] // tnd

    def wu_copy(step, c, slot):
        cols = pl.ds(pl.multiple_of(c * tnu, tnu), tnu)
        return pltpu.make_async_copy(wgu_hbm.at[se_ref[step], :, cols], wu_f.at[slot], wu_sem.at[slot])

    def wd_copy(step, c, slot):
        cols = pl.ds(pl.multiple_of(c * tnd, tnd), tnd)
        return pltpu.make_async_copy(wd_hbm.at[se_ref[step], :, cols], wd_f.at[slot], wd_sem.at[slot])

    def xs_copy(step, g):
        return pltpu.make_async_copy(xs_hbm.at[pl.ds(pl.multiple_of(sr_ref[step] + g * tm, tm), tm)],
                                     xs_v.at[pl.ds(pl.multiple_of(g * tm, tm), tm)], xs_sem)

    def out_copy(slot, row, c):
        cols = pl.ds(pl.multiple_of(c * tnd, tnd), tnd)
        return pltpu.make_async_copy(
            out_v.at[slot], rows_hbm.at[pl.ds(pl.multiple_of(row, tm), tm), cols], out_sem.at[slot])

    def start_rows(step):
        lax.fori_loop(0, sn_ref[step], lambda g, carry: (xs_copy(step, g).start(), carry)[1], 0)

    def for_row_blocks(n_groups, fn, carry):
        n_blocks = n_groups // MOE_BLOCK_GROUPS
        carry = lax.fori_loop(0, n_blocks, lambda b, cr: fn(b * MOE_BLOCK_GROUPS, MOE_BLOCK_GROUPS, cr), carry)
        return lax.fori_loop(n_blocks * MOE_BLOCK_GROUPS, n_groups, lambda g, cr: fn(g, 1, cr), carry)

    @pl.when(s < n_active)
    def _():
        ng = sn_ref[s]
        row0 = sr_ref[s]

        @pl.when(s == 0)
        def _():
            start_rows(0)
            wu_copy(0, 0, 0).start()

        lax.fori_loop(0, ng, lambda g, carry: (xs_copy(s, g).wait(), carry)[1], 0)

        def up_chunk(c, carry):
            slot = c % 2
            wu_copy(s, c, slot).wait()

            @pl.when(c + 1 < n_up)
            def _():
                wu_copy(s, c + 1, 1 - slot).start()

            @pl.when(c + 1 == n_up)
            def _():
                wd_copy(s, 0, 0).start()

            wu_b[...] = wu_f[slot].astype(BF16)
            bias = bgu_ref[c]

            def up_rows(g0, n_g, carry):
                rows = pl.ds(pl.multiple_of(g0 * tm, tm), n_g * tm)
                lo, hi = _unpack_bf16_pairs(xs_v[rows, :])
                tile = _swiglu_tile(lo, hi, wu_b, bias)
                for cc in range(n_up):
                    @pl.when(c == cc)
                    def _(cc=cc):
                        act_v[rows, cc * (tnu // 2):(cc + 1) * (tnu // 2)] = tile
                return carry
            return for_row_blocks(ng, up_rows, carry)
        lax.fori_loop(0, n_up, up_chunk, 0)

        @pl.when(s + 1 < n_active)
        def _():
            start_rows(s + 1)

        def drain(pending):
            for k in range(MOE_BLOCK_GROUPS):
                @pl.when(pending[k] == 1)
                def _(k=k):
                    out_copy(k, 0, 0).wait()

        def down_chunk(c, pending):
            slot = c % 2
            wd_copy(s, c, slot).wait()

            @pl.when(c + 1 < n_down)
            def _():
                wd_copy(s, c + 1, 1 - slot).start()

            @pl.when(jnp.logical_and(c + 1 == n_down, s + 1 < n_active))
            def _():
                wu_copy(s + 1, 0, 0).start()

            wd_b[...] = wd_f[slot].astype(BF16)
            bias = bd_ref[c]

            def down_rows(g0, n_g, pending):
                rows = pl.ds(pl.multiple_of(g0 * tm, tm), n_g * tm)
                o = jnp.dot(act_v[rows, :], wd_b[...], preferred_element_type=F32) + bias
                drain(pending)
                for k in range(n_g):
                    out_v[k] = o[k * tm:(k + 1) * tm]
                    out_copy(k, row0 + (g0 + k) * tm, c).start()
                return tuple(jnp.int32(1 if k < n_g else 0) for k in range(MOE_BLOCK_GROUPS))
            return for_row_blocks(ng, down_rows, pending)
        drain(lax.fori_loop(0, n_down, down_chunk, (jnp.int32(0),) * MOE_BLOCK_GROUPS))

    @pl.when(s == pl.num_programs(0) - 1)
    def _():
        out_v[0] = jnp.zeros(out_v.shape[1:], out_v.dtype)
        first, last = cnt_ref[1], rows_hbm.shape[0] // tm

        def fill(gi, carry, wait):
            for c in range(n_down):
                cp = out_copy(0, gi * tm, c)
                cp.wait() if wait else cp.start()
            return carry
        lax.fori_loop(first, last, functools.partial(fill, wait=False), 0)
        lax.fori_loop(first, last, functools.partial(fill, wait=True), 0)


def _experts(xs, w_gu, b_gu, w_d, b_d, step_expert, step_row, step_groups, counts, *, tm):
    R, half = xs.shape
    E, D, F2 = w_gu.shape
    F = F2 // 2
    tnu, tnd = min(MOE_UP_COLS, F2), min(MOE_DOWN_COLS, D)
    assert D == 2 * half and F2 % tnu == 0 and D % tnd == 0 and tnu % (2 * LANES) == 0 and D // tnd >= 2
    cap = MOE_GROUPS_PER_STEP * tm
    n_steps = step_expert.shape[0]
    expert_of = lambda s, se, sr, sn, cnt: (se[jnp.minimum(s, cnt[0] - 1)], 0, 0, 0)
    grid_spec = pltpu.PrefetchScalarGridSpec(
        num_scalar_prefetch=4, grid=(n_steps,),
        in_specs=[pl.BlockSpec((None, F2 // tnu, 1, tnu), expert_of),
                  pl.BlockSpec((None, D // tnd, 1, tnd), expert_of),
                  pl.BlockSpec(memory_space=pl.ANY), pl.BlockSpec(memory_space=pl.ANY),
                  pl.BlockSpec(memory_space=pl.ANY)],
        out_specs=pl.BlockSpec(memory_space=pl.ANY),
        scratch_shapes=[pltpu.VMEM((cap, half), jnp.uint32), pltpu.VMEM((cap, F), BF16),
                        pltpu.VMEM((2, D, tnu), F32), pltpu.VMEM((D, tnu), BF16),
                        pltpu.VMEM((2, F, tnd), F32), pltpu.VMEM((F, tnd), BF16),
                        pltpu.VMEM((MOE_BLOCK_GROUPS, tm, tnd), F32),
                        pltpu.SemaphoreType.DMA(()), pltpu.SemaphoreType.DMA((2,)),
                        pltpu.SemaphoreType.DMA((2,)), pltpu.SemaphoreType.DMA((MOE_BLOCK_GROUPS,))])
    return pl.pallas_call(
        functools.partial(_experts_kernel, tm=tm), grid_spec=grid_spec,
        out_shape=jax.ShapeDtypeStruct((R, D), F32),
        compiler_params=_params("arbitrary"),
    )(step_expert, step_row, step_groups, counts, b_gu.reshape(E, F2 // tnu, 1, tnu),
      b_d.reshape(E, D // tnd, 1, tnd), xs, w_gu, w_d)


def _moe(x, g_moe, w_router, b_router, w_gate_up, b_gate_up, w_down, b_down, g_final, final_norm):
    T, D = x.shape
    E = w_router.shape[1]
    A = T * TOP_K
    tm = min(MOE_GROUP_ROWS, A)
    slab_i, slab_g, cnt = _router(x, g_moe, w_router, b_router)
    idx = slab_i[:, :TOP_K]
    rank = slab_i[:, TOP_K:2 * TOP_K]
    counts = cnt[0, :E].astype(jnp.int32)
    groups = (counts + tm - 1) // tm
    pad_ends = jnp.cumsum(groups) * tm
    pad_starts = pad_ends - groups * tm
    dest = pad_starts[idx] + rank
    n_groups = (A + tm - 1) // tm + E
    R = n_groups * tm
    per = MOE_GROUPS_PER_STEP
    runs = (groups + per - 1) // per
    run_ends = jnp.cumsum(runs)
    n_steps = (n_groups + per - 1) // per + E
    step = jnp.arange(n_steps, dtype=jnp.int32)
    step_expert = jnp.minimum(jnp.sum(step[:, None] >= run_ends[None, :], axis=1), E - 1).astype(jnp.int32)
    local = step - (run_ends - runs)[step_expert]
    step_row = (pad_starts[step_expert] + local * per * tm).astype(jnp.int32)
    step_groups = jnp.clip(groups[step_expert] - local * per, 0, per).astype(jnp.int32)
    step_counts = jnp.stack([run_ends[-1], pad_ends[-1] // tm]).astype(jnp.int32)

    xs = _scatter_norm(x, g_moe, dest, pad_ends, groups * tm, R, tm=tm)
    rows = _experts(xs, w_gate_up, b_gate_up, w_down, b_down, step_expert, step_row, step_groups, step_counts, tm=tm)
    return _combine(x, rows, dest, slab_g, g_final, final_norm)


def kernel(x, mem, g_mix, w_in, b_in, w_pool, pool_scale, w_fox_o, w_out, g_mem_q, g_mem_kv, w_mem_q, w_mem_kv,
           w_mem_o, g_moe, w_router, b_router, w_gate_up, b_gate_up, w_down, b_down, g_final):
    B, S, D = x.shape
    T = B * S
    n_mem = mem.shape[1]
    depth, G, C, Do = w_pool.shape
    pool_w = G * C
    fox_w = w_fox_o.shape[1]
    H = w_in.shape[2] - pool_w - 3 * fox_w - 2 * D
    dh = fox_w // H
    off_q, off_f = pool_w, pool_w + 3 * fox_w
    off_gate = off_f + H
    xt = x.reshape(T, D)
    mt = mem.reshape(B * n_mem, D)
    for l in range(depth):
        wl, bl = jnp.swapaxes(w_in[l], 0, 1), b_in[l]
        h, lf_t = _norm(xt, g_mix[l], wl[off_f:off_gate].astype(BF16), bl[off_f:off_gate])
        u = _proj(h, wl, bl, w_t=True, col_off=0, n_cols=off_q, out_dtype=F32)
        qkv = _proj(h, wl, bl, w_t=True, col_off=off_q, n_cols=off_f - off_q, out_dtype=BF16)
        gate_base = off_gate // LANES * LANES
        gate0 = off_gate - gate_base
        gates = _proj(h, wl, bl, w_t=True, col_off=gate_base, mode="sigmoid", out_dtype=BF16)
        c_t = _forget_cumsum(lf_t, B, S)
        att = _fox_attention(qkv, c_t.transpose(0, 2, 1), c_t, B, S, H, dh)
        pp = _pool_mixer(u, w_pool[l], pool_scale[l], gates, gate0, S)
        merged = _proj(att, w_fox_o[l], None, pp, (gates, gate0 + D), mode="merge", out_dtype=BF16)
        xt = _proj(merged, w_out[l], None, xt, mode="residual", out_dtype=F32)
        kv = _proj(_norm(mt, g_mem_kv[l]), w_mem_kv[l], out_dtype=BF16)
        qm = _proj(_norm(xt, g_mem_q[l]), w_mem_q[l], out_dtype=BF16)
        om = _mem_attention(qm, kv, B, S, n_mem)
        xt = _proj(om, w_mem_o[l], None, xt, mode="residual", out_dtype=F32)
        xt = _moe(xt, g_moe[l], w_router[l], b_router[l], w_gate_up[l], b_gate_up[l], w_down[l], b_down[l],
                  g_final, final_norm=l == depth - 1)
    return xt.reshape(B, S, D)
```

```python
import functools

import jax
import jax.numpy as jnp
from jax import lax
from jax.experimental import pallas as pl
from jax.experimental.pallas import tpu as pltpu

F32 = jnp.float32
BF16 = jnp.bfloat16

EPS = 1e-5
POOL_WINDOWS = (2, 4, 8, 16)
POOL_HALO = 16
MEM_HEADS = 4
TOP_K = 4
SWIGLU_LIMIT = 7.0
SWIGLU_ALPHA = 1.702
NEG_BIG = -1e30
LOG2E = 1.4426950408889634

LANES = 128
VMEM_LIMIT_BYTES = 56 * 1024 * 1024


def _tile(dim, pref):
    t = pref
    while t >= 8:
        if dim % t == 0:
            return t
        t //= 2
    return dim


def _params(*sem):
    return pltpu.CompilerParams(dimension_semantics=sem, vmem_limit_bytes=VMEM_LIMIT_BYTES)


def _rms(x, g):
    ms = jnp.mean(x * x, axis=-1, keepdims=True)
    return x * lax.rsqrt(ms + EPS) * g


def _split3(x):
    hi = x.astype(BF16)
    r1 = x - hi.astype(F32)
    mid = r1.astype(BF16)
    lo = (r1 - mid.astype(F32)).astype(BF16)
    return hi, mid, lo


def _norm_kernel(x_ref, g_ref, *refs, with_f):
    h = _rms(x_ref[...], g_ref[...]).astype(BF16)
    if with_f:
        wf_ref, bf_ref, h_ref, lf_ref = refs
        f = lax.dot_general(wf_ref[...], h, (((1,), (1,)), ((), ())), preferred_element_type=F32) + bf_ref[...]
        lf_ref[...] = jnp.minimum(f, 0.0) - jnp.log1p(jnp.exp(-jnp.abs(f)))
    else:
        h_ref, = refs
    h_ref[...] = h


def _norm(x, g, wf_t=None, bf_t=None, *, tm=512):
    M, K = x.shape
    tm = _tile(M, tm)
    with_f = wf_t is not None
    in_specs = [pl.BlockSpec((tm, K), lambda i: (i, 0)), pl.BlockSpec((1, K), lambda i: (0, 0))]
    args = [x, g.reshape(1, K)]
    out_shape = [jax.ShapeDtypeStruct((M, K), BF16)]
    out_specs = [pl.BlockSpec((tm, K), lambda i: (i, 0))]
    if with_f:
        H = wf_t.shape[0]
        in_specs += [pl.BlockSpec((H, K), lambda i: (0, 0)), pl.BlockSpec((H, 1), lambda i: (0, 0))]
        args += [wf_t, bf_t.reshape(H, 1)]
        out_shape.append(jax.ShapeDtypeStruct((H, M), F32))
        out_specs.append(pl.BlockSpec((H, tm), lambda i: (0, i)))
    res = pl.pallas_call(
        functools.partial(_norm_kernel, with_f=with_f),
        grid=(M // tm,),
        in_specs=in_specs, out_specs=out_specs, out_shape=out_shape,
        compiler_params=_params("parallel"),
    )(*args)
    return res if with_f else res[0]


def _lane_window(main_ref, tail_ref, shift):
    if shift == 0:
        return main_ref[...].astype(F32)
    g = jnp.concatenate([main_ref[...], tail_ref[...]], axis=1).astype(F32)
    return pltpu.roll(g, g.shape[1] - shift, axis=1)[:, :main_ref.shape[1]]


def _window_specs(first_col, rows, width, row_of, col_block_of):
    base, shift = first_col // LANES * LANES, first_col % LANES
    assert base % width == 0
    specs = [pl.BlockSpec((rows, width), lambda *g: (row_of(*g), col_block_of(*g) + base // width))]
    if shift:
        specs.append(pl.BlockSpec(
            (rows, LANES), lambda *g: (row_of(*g), (base + (col_block_of(*g) + 1) * width) // LANES)))
    return specs, shift


def _proj_kernel(x_ref, w_ref, *refs, mode, has_bias, gate_shift, w_rows_are_outputs):
    refs = list(refs)
    wb_ref = refs.pop()
    o_ref = refs.pop()
    b_ref = refs.pop(0) if has_bias else None

    @pl.when(pl.program_id(1) == 0)
    def _():
        w = w_ref[...]
        wb_ref[...] = (w.T if w_rows_are_outputs else w).astype(BF16)

    acc = jnp.dot(x_ref[...], wb_ref[...], preferred_element_type=F32)
    if has_bias:
        acc = acc + b_ref[...]
    if mode == "sigmoid":
        acc = jax.nn.sigmoid(acc)
    elif mode == "merge":
        pp_ref, g_ref = refs[0], refs[1]
        acc = pp_ref[...] + _lane_window(g_ref, refs[2] if gate_shift else None, gate_shift) * acc
    elif mode == "residual":
        acc = refs[0][...] + acc
    o_ref[...] = acc.astype(o_ref.dtype)


def _proj(x, w, b=None, extra=None, gate=None, *, w_t=False, col_off=0, n_cols=None, mode="plain", out_dtype,
          tm=1024, tn=512):
    M, K = x.shape
    N = w.shape[0 if w_t else 1] - col_off if n_cols is None else n_cols
    tm = _tile(M, tm)
    tn = min(tn, N)
    while col_off % tn:
        tn //= 2
    assert tn % LANES == 0
    if w_t:
        w_spec = pl.BlockSpec((tn, K), lambda j, i, o=col_off // tn: (j + o, 0))
    else:
        w_spec = pl.BlockSpec((K, tn), lambda j, i, o=col_off // tn: (0, j + o))
    in_specs = [pl.BlockSpec((tm, K), lambda j, i: (i, 0)), w_spec]
    args = [x, w]
    if b is not None:
        in_specs.append(pl.BlockSpec((1, tn), lambda j, i, o=col_off // tn: (0, j + o)))
        args.append(b.reshape(1, -1))
    if extra is not None:
        in_specs.append(pl.BlockSpec((tm, tn), lambda j, i: (i, j)))
        args.append(extra)
    gate_shift = 0
    if gate is not None:
        specs, gate_shift = _window_specs(gate[1], tm, tn, lambda j, i: i, lambda j, i: j)
        in_specs += specs
        args += [gate[0]] * len(specs)
    return pl.pallas_call(
        functools.partial(_proj_kernel, mode=mode, has_bias=b is not None, gate_shift=gate_shift,
                          w_rows_are_outputs=w_t),
        grid=(pl.cdiv(N, tn), M // tm),
        in_specs=in_specs,
        out_specs=pl.BlockSpec((tm, tn), lambda j, i: (i, j)),
        out_shape=jax.ShapeDtypeStruct((M, N), out_dtype),
        scratch_shapes=[pltpu.VMEM((K, tn), BF16)],
        compiler_params=_params("parallel", "arbitrary"),
    )(*args)


def _cumsum_kernel(lf_ref, c_ref):
    S = lf_ref.shape[1]
    row = lax.broadcasted_iota(jnp.int32, (S, S), 0)
    col = lax.broadcasted_iota(jnp.int32, (S, S), 1)
    upper = (row <= col).astype(BF16)
    c = jnp.zeros(lf_ref.shape, F32)
    for part in _split3(lf_ref[...]):
        c = c + jnp.dot(part, upper, preferred_element_type=F32)
    c_ref[...] = c


def _forget_cumsum(lf_t, B, S):
    H = lf_t.shape[0]
    return pl.pallas_call(
        _cumsum_kernel,
        grid=(B,),
        in_specs=[pl.BlockSpec((H, S), lambda b: (0, b))],
        out_specs=pl.BlockSpec((None, H, S), lambda b: (b, 0, 0)),
        out_shape=jax.ShapeDtypeStruct((B, H, S), F32),
        compiler_params=_params("parallel"),
    )(lf_t)


def _pool_kernel(u_ref, halo_ref, w_ref, sc_ref, *refs, seq_len, gate_shift):
    g0_ref, g0_tail_ref = (refs[0], refs[1]) if gate_shift else (refs[0], None)
    o_ref, ext_ref, pooled_ref = refs[-3:]
    g = pl.program_id(0)
    i = pl.program_id(1)
    tp = u_ref.shape[0]
    pos0 = (i * tp) % seq_len
    u = u_ref[...]
    ext_ref[pl.ds(POOL_HALO, tp), :] = u
    ext_ref[pl.ds(0, POOL_HALO), :] = jnp.where(pos0 == 0, 0.0, halo_ref[...])
    pos = pos0 + lax.broadcasted_iota(jnp.int32, (tp, 1), 0)
    for gi, win in enumerate(POOL_WINDOWS):
        @pl.when(g == gi)
        def _(win=win):
            acc = u
            for k in range(1, win):
                acc = acc + ext_ref[pl.ds(POOL_HALO - k, tp), :]
            cnt = jnp.minimum(pos + 1, win).astype(F32)
            pooled_ref[...] = (acc / cnt - u).astype(BF16)
    y = jnp.dot(pooled_ref[...], w_ref[...].astype(BF16), preferred_element_type=F32)
    o_ref[...] = _lane_window(g0_ref, g0_tail_ref, gate_shift) * (y * sc_ref[...])


def _pool_mixer(u, w_pool, scale, gates, gate0_col, seq_len, *, tp=512):
    T = u.shape[0]
    G, C, Do = w_pool.shape
    tp = _tile(seq_len, tp)
    assert tp % POOL_HALO == 0
    hb = tp // POOL_HALO
    gate_specs, gate_shift = _window_specs(gate0_col, tp, Do, lambda g, i: i, lambda g, i: g)
    return pl.pallas_call(
        functools.partial(_pool_kernel, seq_len=seq_len, gate_shift=gate_shift),
        grid=(G, T // tp),
        in_specs=[pl.BlockSpec((tp, C), lambda g, i: (i, g)),
                  pl.BlockSpec((POOL_HALO, C), lambda g, i: (jnp.maximum(i * hb - 1, 0), g)),
                  pl.BlockSpec((None, C, Do), lambda g, i: (g, 0, 0)),
                  pl.BlockSpec((1, Do), lambda g, i: (0, g))] + gate_specs,
        out_specs=pl.BlockSpec((tp, Do), lambda g, i: (i, g)),
        out_shape=jax.ShapeDtypeStruct((T, G * Do), F32),
        scratch_shapes=[pltpu.VMEM((tp + POOL_HALO, C), F32), pltpu.VMEM((tp, C), BF16)],
        compiler_params=_params("parallel", "parallel"),
    )(u, u, w_pool, scale.reshape(1, G * Do), *([gates] * len(gate_specs)))


def _fox_kernel(q_ref, k_ref, v_ref, cq_ref, ck_ref, o_ref, *, scale, dh):
    hg = pl.program_id(1)
    qi = pl.program_id(2)
    tq = q_ref.shape[0]
    n_heads = q_ref.shape[1] // dh
    lane = lax.broadcasted_iota(jnp.int32, cq_ref.shape, 1)
    cq_all = cq_ref[...]
    q = [q_ref[:, n * dh:(n + 1) * dh] for n in range(n_heads)]
    cq2 = [LOG2E * jnp.sum(jnp.where(lane == hg * n_heads + n, cq_all, 0.0), axis=1, keepdims=True)
           for n in range(n_heads)]

    def scores(n, j):
        rows = pl.ds(pl.multiple_of(j * tq, tq), tq)
        s = lax.dot_general(q[n], k_ref[rows, n * dh:(n + 1) * dh], (((1,), (1,)), ((), ())),
                            preferred_element_type=F32) * (scale * LOG2E)
        return s - LOG2E * ck_ref[n:n + 1, rows]

    def update(n, j, s, carry):
        m, l, acc = carry
        m_new = jnp.maximum(m, jnp.max(s, axis=1, keepdims=True) + cq2[n])
        alpha = jnp.exp2(m - m_new)
        p = jnp.exp2(s - (m_new - cq2[n]))
        v = v_ref[pl.ds(pl.multiple_of(j * tq, tq), tq), n * dh:(n + 1) * dh]
        acc = alpha * acc + jnp.dot(p.astype(BF16), v, preferred_element_type=F32)
        return m_new, alpha * l + jnp.sum(p, axis=1, keepdims=True), acc

    init = (jnp.full((tq, 1), NEG_BIG, F32), jnp.zeros((tq, 1), F32), jnp.zeros((tq, dh), F32))
    carries = lax.fori_loop(
        0, qi, lambda j, cs: tuple(update(n, j, scores(n, j), cs[n]) for n in range(n_heads)), (init,) * n_heads)
    row = lax.broadcasted_iota(jnp.int32, (tq, tq), 0)
    col = lax.broadcasted_iota(jnp.int32, (tq, tq), 1)
    for n in range(n_heads):
        s = jnp.where(col <= row, scores(n, qi), NEG_BIG)
        m, l, acc = update(n, qi, s, carries[n])
        o_ref[:, n * dh:(n + 1) * dh] = (acc / l).astype(o_ref.dtype)


def _fox_attention(qkv, c, c_t, B, S, H, dh, *, tq=512, heads_per_step=2):
    T = B * S
    tq = _tile(S, tq)
    nq = S // tq
    hp = heads_per_step if H % heads_per_step == 0 else 1
    ng = H // hp
    w = hp * dh
    return pl.pallas_call(
        functools.partial(_fox_kernel, scale=dh ** -0.5, dh=dh),
        grid=(B, ng, nq),
        in_specs=[pl.BlockSpec((tq, w), lambda b, h, i: (b * nq + i, h)),
                  pl.BlockSpec((S, w), lambda b, h, i: (b, ng + h)),
                  pl.BlockSpec((S, w), lambda b, h, i: (b, 2 * ng + h)),
                  pl.BlockSpec((None, tq, H), lambda b, h, i: (b, i, 0)),
                  pl.BlockSpec((None, None, hp, S), lambda b, h, i: (b, h, 0, 0))],
        out_specs=pl.BlockSpec((tq, w), lambda b, h, i: (b * nq + i, h)),
        out_shape=jax.ShapeDtypeStruct((T, H * dh), BF16),
        compiler_params=_params("parallel", "parallel", "parallel"),
    )(qkv, qkv, qkv, c, c_t.reshape(B, ng, hp, S))


def _mem_attn_kernel(q_ref, kv_ref, o_ref, *, heads):
    width = q_ref.shape[1]
    dh = width // heads
    scale = dh ** -0.5
    for hd in range(heads):
        q = q_ref[:, hd * dh:(hd + 1) * dh]
        k = kv_ref[:, hd * dh:(hd + 1) * dh]
        v = kv_ref[:, width + hd * dh:width + (hd + 1) * dh]
        s = lax.dot_general(q, k, (((1,), (1,)), ((), ())), preferred_element_type=F32) * scale
        p = jnp.exp(s - jnp.max(s, axis=1, keepdims=True))
        p = p / jnp.sum(p, axis=1, keepdims=True)
        o_ref[:, hd * dh:(hd + 1) * dh] = jnp.dot(p.astype(BF16), v, preferred_element_type=F32).astype(o_ref.dtype)


def _mem_attention(q, kv, B, S, n_mem, *, tq=512):
    T, width = q.shape
    tq = _tile(S, tq)
    nq = S // tq
    return pl.pallas_call(
        functools.partial(_mem_attn_kernel, heads=MEM_HEADS),
        grid=(B, nq),
        in_specs=[pl.BlockSpec((tq, width), lambda b, i: (b * nq + i, 0)),
                  pl.BlockSpec((n_mem, 2 * width), lambda b, i: (b, 0))],
        out_specs=pl.BlockSpec((tq, width), lambda b, i: (b * nq + i, 0)),
        out_shape=jax.ShapeDtypeStruct((T, width), BF16),
        compiler_params=_params("parallel", "parallel"),
    )(q, kv)


def _router_kernel(x_ref, g_ref, whi_ref, wlo_ref, b_ref, oi_ref, og_ref, cnt_ref, carry_ref, *, n_experts):
    i = pl.program_id(0)
    tm = x_ref.shape[0]

    @pl.when(i == 0)
    def _():
        carry_ref[...] = jnp.zeros_like(carry_ref)

    h = _rms(x_ref[...], g_ref[...])
    h_hi = h.astype(BF16)
    h_lo = (h - h_hi.astype(F32)).astype(BF16)
    logits = (jnp.dot(h_hi, whi_ref[...], preferred_element_type=F32)
              + jnp.dot(h_hi, wlo_ref[...], preferred_element_type=F32)
              + jnp.dot(h_lo, whi_ref[...], preferred_element_type=F32)) + b_ref[...]
    lane = lax.broadcasted_iota(jnp.int32, (tm, LANES), 1).astype(F32)
    work = jnp.where(lane < n_experts, logits, -jnp.inf)
    vals, idxs = [], []
    for _ in range(TOP_K):
        m = jnp.max(work, axis=1, keepdims=True)
        idx = jnp.min(jnp.where(work == m, lane, float(LANES)), axis=1, keepdims=True)
        vals.append(m)
        idxs.append(idx)
        work = jnp.where(lane == idx, -jnp.inf, work)
    exps = [jnp.exp(v - vals[0]) for v in vals]
    denom = exps[0] + exps[1] + exps[2] + exps[3]
    onehots = [(lane == idx).astype(F32) for idx in idxs]
    chosen = onehots[0] + onehots[1] + onehots[2] + onehots[3]
    row = lax.broadcasted_iota(jnp.int32, (tm, tm), 0)
    col = lax.broadcasted_iota(jnp.int32, (tm, tm), 1)
    before = jnp.dot((col < row).astype(BF16), chosen.astype(BF16), preferred_element_type=F32) + carry_ref[...]
    out_i = jnp.zeros((tm, LANES), F32)
    out_g = jnp.zeros((tm, LANES), F32)
    for k in range(TOP_K):
        rank = jnp.sum(onehots[k] * before, axis=1, keepdims=True)
        out_i = jnp.where(lane == k, idxs[k], out_i)
        out_i = jnp.where(lane == TOP_K + k, rank, out_i)
        out_g = jnp.where(lane == k, exps[k] / denom, out_g)
    oi_ref[...] = out_i.astype(jnp.int32)
    og_ref[...] = out_g
    carry_ref[...] = carry_ref[...] + jnp.sum(chosen, axis=0, keepdims=True)
    cnt_ref[...] = carry_ref[...]


def _router(x, g, w_router, b_router, *, tm=512):
    T, D = x.shape
    E = w_router.shape[1]
    tm = _tile(T, tm)
    w_pad = jnp.zeros((D, LANES), F32).at[:, :E].set(w_router)
    w_hi = w_pad.astype(BF16)
    w_lo = (w_pad - w_hi.astype(F32)).astype(BF16)
    b_pad = jnp.zeros((1, LANES), F32).at[0, :E].set(b_router)
    full = lambda i: (0, 0)
    return pl.pallas_call(
        functools.partial(_router_kernel, n_experts=E),
        grid=(T // tm,),
        in_specs=[pl.BlockSpec((tm, D), lambda i: (i, 0)), pl.BlockSpec((1, D), full),
                  pl.BlockSpec((D, LANES), full), pl.BlockSpec((D, LANES), full), pl.BlockSpec((1, LANES), full)],
        out_specs=[pl.BlockSpec((tm, LANES), lambda i: (i, 0)), pl.BlockSpec((tm, LANES), lambda i: (i, 0)),
                   pl.BlockSpec((1, LANES), full)],
        out_shape=[jax.ShapeDtypeStruct((T, LANES), jnp.int32), jax.ShapeDtypeStruct((T, LANES), F32),
                   jax.ShapeDtypeStruct((1, LANES), F32)],
        scratch_shapes=[pltpu.VMEM((1, LANES), F32)],
        compiler_params=_params("arbitrary"),
    )(x, g.reshape(1, D), w_hi, w_lo, b_pad)


def _row_copy(src_hbm, row, buf, slot, r, sem):
    return pltpu.make_async_copy(src_hbm.at[pl.ds(row, 1)], buf.at[slot, pl.ds(r, 1)], sem.at[slot])


def _start_rows(idx_ref, src_hbm, buf, slot, sem, n_rows):
    def body(r, _):
        _row_copy(src_hbm, idx_ref[0, 0, r], buf, slot, r, sem).start()
        return 0
    lax.fori_loop(0, n_rows, body, 0, unroll=8)


def _wait_rows(src_hbm, buf, slot, sem, n_rows):
    pltpu.make_async_copy(src_hbm.at[pl.ds(0, n_rows)], buf.at[slot], sem.at[slot]).wait()


def _gather_pipeline(idx_ref, nxt_ref, src_hbm, buf, sem, n_rows):
    i = pl.program_id(0)
    n = pl.num_programs(0)
    slot = i % 2

    @pl.when(i == 0)
    def _():
        _start_rows(idx_ref, src_hbm, buf, 0, sem, n_rows)

    @pl.when(i + 1 < n)
    def _():
        _start_rows(nxt_ref, src_hbm, buf, 1 - slot, sem, n_rows)

    _wait_rows(src_hbm, buf, slot, sem, n_rows)
    return slot


def _pack_bf16_pairs(h):
    half = h.shape[1] // 2
    bits = pltpu.bitcast(h.astype(BF16).astype(F32), jnp.uint32)
    return (bits[:, :half] >> 16) | bits[:, half:]


def _unpack_bf16_pairs(u):
    lo = pltpu.bitcast(u << 16, F32).astype(BF16)
    hi = pltpu.bitcast(u & jnp.uint32(0xFFFF0000), F32).astype(BF16)
    return lo, hi


def _scatter_norm_kernel(pe_ref, pd_ref, dest_ref, x_ref, g_ref, xs_hbm, buf, zbuf, sem, zsem, *, tm, n_experts):
    i = pl.program_id(0)
    n = pl.num_programs(0)
    tb = x_ref.shape[0]
    slot = i % 2

    def row_copy(s, t, row):
        return pltpu.make_async_copy(buf.at[s, pl.ds(t, 1)], xs_hbm.at[pl.ds(row, 1)], sem.at[s])

    def wait_slot(s):
        for _ in range(TOP_K):
            pltpu.make_async_copy(buf.at[s], xs_hbm.at[pl.ds(0, tb)], sem.at[s]).wait()

    def zero_group(first_row):
        return pltpu.make_async_copy(zbuf, xs_hbm.at[pl.ds(pl.multiple_of(first_row, tm), tm)], zsem)

    @pl.when(i == 0)
    def _():
        zbuf[...] = jnp.zeros_like(zbuf)
        n_groups = xs_hbm.shape[0] // tm
        used = pe_ref[n_experts - 1] // tm
        for e in range(n_experts):
            @pl.when(pd_ref[e] > 0)
            def _(e=e):
                zero_group(pe_ref[e] - tm).start()
        lax.fori_loop(used, n_groups, lambda gi, c: (zero_group(gi * tm).start(), c)[1], 0)
        for e in range(n_experts):
            @pl.when(pd_ref[e] > 0)
            def _(e=e):
                zero_group(pe_ref[e] - tm).wait()
        lax.fori_loop(used, n_groups, lambda gi, c: (zero_group(gi * tm).wait(), c)[1], 0)

    @pl.when(i >= 2)
    def _():
        wait_slot(slot)

    buf[slot] = _pack_bf16_pairs(_rms(x_ref[...], g_ref[...]))

    def body(t, _):
        for k in range(TOP_K):
            row_copy(slot, t, dest_ref[0, 0, t * TOP_K + k]).start()
        return 0
    lax.fori_loop(0, tb, body, 0, unroll=4)

    @pl.when(i == n - 1)
    def _():
        @pl.when(n >= 2)
        def _():
            wait_slot(1 - slot)
        wait_slot(slot)


def _scatter_norm(x, g, dest, pad_ends, padded, n_rows, *, tm, tb=128):
    T, D = x.shape
    E = pad_ends.shape[0]
    tb = _tile(T, tb)
    nb = T // tb
    grid_spec = pltpu.PrefetchScalarGridSpec(
        num_scalar_prefetch=2, grid=(nb,),
        in_specs=[pl.BlockSpec((1, 1, TOP_K * tb), lambda i, pe, pd: (i, 0, 0), memory_space=pltpu.SMEM),
                  pl.BlockSpec((tb, D), lambda i, pe, pd: (i, 0)),
                  pl.BlockSpec((1, D), lambda i, pe, pd: (0, 0))],
        out_specs=pl.BlockSpec(memory_space=pl.ANY),
        scratch_shapes=[pltpu.VMEM((2, tb, D // 2), jnp.uint32), pltpu.VMEM((tm, D // 2), jnp.uint32),
                        pltpu.SemaphoreType.DMA((2,)), pltpu.SemaphoreType.DMA(())])
    return pl.pallas_call(
        functools.partial(_scatter_norm_kernel, tm=tm, n_experts=E), grid_spec=grid_spec,
        out_shape=jax.ShapeDtypeStruct((n_rows, D // 2), jnp.uint32),
        compiler_params=_params("arbitrary"),
    )(pad_ends, padded, dest.reshape(nb, 1, TOP_K * tb), x, g.reshape(1, D))


def _combine_kernel(idx_ref, nxt_ref, x_ref, gate_ref, g_ref, rows_hbm, o_ref, buf, sem, *, final_norm):
    tb = x_ref.shape[0]
    slot = _gather_pipeline(idx_ref, nxt_ref, rows_hbm, buf, sem, TOP_K * tb)
    y = x_ref[...]
    gates = gate_ref[...]
    for k in range(TOP_K):
        y = y + gates[:, k:k + 1] * buf[slot, pl.ds(k * tb, tb), :]
    o_ref[...] = _rms(y, g_ref[...]) if final_norm else y


def _combine(x, rows, dest, gates_slab, g, final_norm, *, tb=64):
    T, D = x.shape
    tb = _tile(T, tb)
    nb = T // tb
    idx3 = dest.reshape(nb, tb, TOP_K).transpose(0, 2, 1).reshape(nb, 1, TOP_K * tb)
    return pl.pallas_call(
        functools.partial(_combine_kernel, final_norm=final_norm),
        grid=(nb,),
        in_specs=[pl.BlockSpec((1, 1, TOP_K * tb), lambda i: (i, 0, 0), memory_space=pltpu.SMEM),
                  pl.BlockSpec((1, 1, TOP_K * tb), lambda i: (jnp.minimum(i + 1, nb - 1), 0, 0), memory_space=pltpu.SMEM),
                  pl.BlockSpec((tb, D), lambda i: (i, 0)),
                  pl.BlockSpec((tb, LANES), lambda i: (i, 0)),
                  pl.BlockSpec((1, D), lambda i: (0, 0)),
                  pl.BlockSpec(memory_space=pl.ANY)],
        out_specs=pl.BlockSpec((tb, D), lambda i: (i, 0)),
        out_shape=jax.ShapeDtypeStruct((T, D), F32),
        scratch_shapes=[pltpu.VMEM((2, TOP_K * tb, D), F32), pltpu.SemaphoreType.DMA((2,))],
        compiler_params=_params("arbitrary"),
    )(idx3, idx3, x, gates_slab, g.reshape(1, D), rows)


MOE_GROUP_ROWS = 256
MOE_GROUPS_PER_STEP = 5
MOE_BLOCK_SIZES = (5, 4, 1)
MOE_BLOCK_GROUPS = max(MOE_BLOCK_SIZES)
MOE_UP_COLS = 512
MOE_DOWN_COLS = 512


def _swiglu_tile(lo, hi, wb_ref, b):
    half = lo.shape[1]
    pair = 2 * LANES
    row = lax.broadcasted_iota(jnp.int32, (pair, LANES), 0)
    col = lax.broadcasted_iota(jnp.int32, (pair, LANES), 1)
    pick_even = (row == 2 * col).astype(BF16)
    even = lax.broadcasted_iota(jnp.int32, (1, pair), 1) % 2 == 0
    gu = (jnp.dot(lo, wb_ref[:half, :], preferred_element_type=F32)
          + jnp.dot(hi, wb_ref[half:, :], preferred_element_type=F32)) + b
    outs = []
    for c in range(gu.shape[1] // pair):
        blk = gu[:, c * pair:(c + 1) * pair]
        nxt = pltpu.roll(blk, pair - 1, axis=1)
        gate = jnp.minimum(blk, SWIGLU_LIMIT)
        up = jnp.clip(nxt, -SWIGLU_LIMIT, SWIGLU_LIMIT)
        act = (up + 1.0) * gate * jax.nn.sigmoid(SWIGLU_ALPHA * gate)
        act = jnp.where(even, act, 0.0).astype(BF16)
        outs.append(jnp.dot(act, pick_even, preferred_element_type=F32).astype(BF16))
    return outs[0] if len(outs) == 1 else jnp.concatenate(outs, axis=1)


def _experts_kernel(se_ref, sr_ref, sn_ref, cnt_ref, bgu_ref, bd_ref, xs_hbm, wgu_hbm, wd_hbm, rows_hbm,
                    xs_v, act_v, wu_f, wu_b, wd_f, wd_b, out_v, xs_sem, wu_sem, wd_sem, out_sem, *, tm):
    s = pl.program_id(0)
    n_active = cnt_ref[0]
    tnu, tnd = wu_b.shape[1], wd_b.shape[1]
    n_up, n_down = wgu_hbm.shape[2] // tnu, wd_hbm.shape[2] // tnd

    def wu_copy(step, c, slot):
        cols = pl.ds(pl.multiple_of(c * tnu, tnu), tnu)
        return pltpu.make_async_copy(wgu_hbm.at[se_ref[step], :, cols], wu_f.at[slot], wu_sem.at[slot])

    def wd_copy(step, c, slot):
        cols = pl.ds(pl.multiple_of(c * tnd, tnd), tnd)
        return pltpu.make_async_copy(wd_hbm.at[se_ref[step], :, cols], wd_f.at[slot], wd_sem.at[slot])

    def xs_copy(step, g):
        return pltpu.make_async_copy(xs_hbm.at[pl.ds(pl.multiple_of(sr_ref[step] + g * tm, tm), tm)],
                                     xs_v.at[pl.ds(pl.multiple_of(g * tm, tm), tm)], xs_sem)

    def out_copy(slot, row, c):
        cols = pl.ds(pl.multiple_of(c * tnd, tnd), tnd)
        return pltpu.make_async_copy(
            out_v.at[slot], rows_hbm.at[pl.ds(pl.multiple_of(row, tm), tm), cols], out_sem.at[slot])

    def start_rows(step):
        lax.fori_loop(0, sn_ref[step], lambda g, carry: (xs_copy(step, g).start(), carry)[1], 0)

    def for_row_blocks(n_groups, fn, carry):
        done = 0
        for size in MOE_BLOCK_SIZES:
            n_blocks = (n_groups - done) // size
            carry = lax.fori_loop(0, n_blocks, lambda b, cr, d=done, sz=size: fn(d + b * sz, sz, cr), carry)
            done = done + n_blocks * size
        return carry

    @pl.when(s < n_active)
    def _():
        ng = sn_ref[s]
        row0 = sr_ref[s]

        @pl.when(s == 0)
        def _():
            start_rows(0)
            wu_copy(0, 0, 0).start()

        lax.fori_loop(0, ng, lambda g, carry: (xs_copy(s, g).wait(), carry)[1], 0)

        def up_chunk(c, carry):
            slot = c % 2
            wu_copy(s, c, slot).wait()

            @pl.when(c + 1 < n_up)
            def _():
                wu_copy(s, c + 1, 1 - slot).start()

            @pl.when(c + 1 == n_up)
            def _():
                wd_copy(s, 0, 0).start()

            wu_b[...] = wu_f[slot].astype(BF16)
            bias = bgu_ref[c]

            def up_rows(g0, n_g, carry):
                rows = pl.ds(pl.multiple_of(g0 * tm, tm), n_g * tm)
                lo, hi = _unpack_bf16_pairs(xs_v[rows, :])
                tile = _swiglu_tile(lo, hi, wu_b, bias)
                for cc in range(n_up):
                    @pl.when(c == cc)
                    def _(cc=cc):
                        act_v[rows, cc * (tnu // 2):(cc + 1) * (tnu // 2)] = tile
                return carry
            return for_row_blocks(ng, up_rows, carry)
        lax.fori_loop(0, n_up, up_chunk, 0)

        @pl.when(s + 1 < n_active)
        def _():
            start_rows(s + 1)

        def drain(pending):
            for k in range(MOE_BLOCK_GROUPS):
                @pl.when(pending[k] == 1)
                def _(k=k):
                    out_copy(k, 0, 0).wait()

        def down_chunk(c, pending):
            slot = c % 2
            wd_copy(s, c, slot).wait()

            @pl.when(c + 1 < n_down)
            def _():
                wd_copy(s, c + 1, 1 - slot).start()

            @pl.when(jnp.logical_and(c + 1 == n_down, s + 1 < n_active))
            def _():
                wu_copy(s + 1, 0, 0).start()

            wd_b[...] = wd_f[slot].astype(BF16)
            bias = bd_ref[c]

            def down_rows(g0, n_g, pending):
                rows = pl.ds(pl.multiple_of(g0 * tm, tm), n_g * tm)
                o = jnp.dot(act_v[rows, :], wd_b[...], preferred_element_type=F32) + bias
                drain(pending)
                for k in range(n_g):
                    out_v[k] = o[k * tm:(k + 1) * tm]
                    out_copy(k, row0 + (g0 + k) * tm, c).start()
                return tuple(jnp.int32(1 if k < n_g else 0) for k in range(MOE_BLOCK_GROUPS))
            return for_row_blocks(ng, down_rows, pending)
        drain(lax.fori_loop(0, n_down, down_chunk, (jnp.int32(0),) * MOE_BLOCK_GROUPS))

    @pl.when(s == pl.num_programs(0) - 1)
    def _():
        out_v[0] = jnp.zeros(out_v.shape[1:], out_v.dtype)
        first, last = cnt_ref[1], rows_hbm.shape[0] // tm

        def fill(gi, carry, wait):
            for c in range(n_down):
                cp = out_copy(0, gi * tm, c)
                cp.wait() if wait else cp.start()
            return carry
        lax.fori_loop(first, last, functools.partial(fill, wait=False), 0)
        lax.fori_loop(first, last, functools.partial(fill, wait=True), 0)


def _experts(xs, w_gu, b_gu, w_d, b_d, step_expert, step_row, step_groups, counts, *, tm):
    R, half = xs.shape
    E, D, F2 = w_gu.shape
    F = F2 // 2
    tnu, tnd = min(MOE_UP_COLS, F2), min(MOE_DOWN_COLS, D)
    assert D == 2 * half and F2 % tnu == 0 and D % tnd == 0 and tnu % (2 * LANES) == 0 and D // tnd >= 2
    assert MOE_BLOCK_SIZES[-1] == 1 and MOE_BLOCK_GROUPS <= MOE_GROUPS_PER_STEP
    cap = MOE_GROUPS_PER_STEP * tm
    n_steps = step_expert.shape[0]
    expert_of = lambda s, se, sr, sn, cnt: (se[jnp.minimum(s, cnt[0] - 1)], 0, 0, 0)
    grid_spec = pltpu.PrefetchScalarGridSpec(
        num_scalar_prefetch=4, grid=(n_steps,),
        in_specs=[pl.BlockSpec((None, F2 // tnu, 1, tnu), expert_of),
                  pl.BlockSpec((None, D // tnd, 1, tnd), expert_of),
                  pl.BlockSpec(memory_space=pl.ANY), pl.BlockSpec(memory_space=pl.ANY),
                  pl.BlockSpec(memory_space=pl.ANY)],
        out_specs=pl.BlockSpec(memory_space=pl.ANY),
        scratch_shapes=[pltpu.VMEM((cap, half), jnp.uint32), pltpu.VMEM((cap, F), BF16),
                        pltpu.VMEM((2, D, tnu), F32), pltpu.VMEM((D, tnu), BF16),
                        pltpu.VMEM((2, F, tnd), F32), pltpu.VMEM((F, tnd), BF16),
                        pltpu.VMEM((MOE_BLOCK_GROUPS, tm, tnd), F32),
                        pltpu.SemaphoreType.DMA(()), pltpu.SemaphoreType.DMA((2,)),
                        pltpu.SemaphoreType.DMA((2,)), pltpu.SemaphoreType.DMA((MOE_BLOCK_GROUPS,))])
    return pl.pallas_call(
        functools.partial(_experts_kernel, tm=tm), grid_spec=grid_spec,
        out_shape=jax.ShapeDtypeStruct((R, D), F32),
        compiler_params=_params("arbitrary"),
    )(step_expert, step_row, step_groups, counts, b_gu.reshape(E, F2 // tnu, 1, tnu),
      b_d.reshape(E, D // tnd, 1, tnd), xs, w_gu, w_d)


def _moe(x, g_moe, w_router, b_router, w_gate_up, b_gate_up, w_down, b_down, g_final, final_norm):
    T, D = x.shape
    E = w_router.shape[1]
    A = T * TOP_K
    tm = min(MOE_GROUP_ROWS, A)
    slab_i, slab_g, cnt = _router(x, g_moe, w_router, b_router)
    idx = slab_i[:, :TOP_K]
    rank = slab_i[:, TOP_K:2 * TOP_K]
    counts = cnt[0, :E].astype(jnp.int32)
    groups = (counts + tm - 1) // tm
    pad_ends = jnp.cumsum(groups) * tm
    pad_starts = pad_ends - groups * tm
    dest = pad_starts[idx] + rank
    n_groups = (A + tm - 1) // tm + E
    R = n_groups * tm
    per = MOE_GROUPS_PER_STEP
    runs = (groups + per - 1) // per
    run_ends = jnp.cumsum(runs)
    n_steps = (n_groups + per - 1) // per + E
    step = jnp.arange(n_steps, dtype=jnp.int32)
    step_expert = jnp.minimum(jnp.sum(step[:, None] >= run_ends[None, :], axis=1), E - 1).astype(jnp.int32)
    local = step - (run_ends - runs)[step_expert]
    step_row = (pad_starts[step_expert] + local * per * tm).astype(jnp.int32)
    step_groups = jnp.clip(groups[step_expert] - local * per, 0, per).astype(jnp.int32)
    step_counts = jnp.stack([run_ends[-1], pad_ends[-1] // tm]).astype(jnp.int32)

    xs = _scatter_norm(x, g_moe, dest, pad_ends, groups * tm, R, tm=tm)
    rows = _experts(xs, w_gate_up, b_gate_up, w_down, b_down, step_expert, step_row, step_groups, step_counts, tm=tm)
    return _combine(x, rows, dest, slab_g, g_final, final_norm)


def kernel(x, mem, g_mix, w_in, b_in, w_pool, pool_scale, w_fox_o, w_out, g_mem_q, g_mem_kv, w_mem_q, w_mem_kv,
           w_mem_o, g_moe, w_router, b_router, w_gate_up, b_gate_up, w_down, b_down, g_final):
    B, S, D = x.shape
    T = B * S
    n_mem = mem.shape[1]
    depth, G, C, Do = w_pool.shape
    pool_w = G * C
    fox_w = w_fox_o.shape[1]
    H = w_in.shape[2] - pool_w - 3 * fox_w - 2 * D
    dh = fox_w // H
    off_q, off_f = pool_w, pool_w + 3 * fox_w
    off_gate = off_f + H
    xt = x.reshape(T, D)
    mt = mem.reshape(B * n_mem, D)
    for l in range(depth):
        wl, bl = jnp.swapaxes(w_in[l], 0, 1), b_in[l]
        h, lf_t = _norm(xt, g_mix[l], wl[off_f:off_gate].astype(BF16), bl[off_f:off_gate])
        u = _proj(h, wl, bl, w_t=True, col_off=0, n_cols=off_q, out_dtype=F32)
        qkv = _proj(h, wl, bl, w_t=True, col_off=off_q, n_cols=off_f - off_q, out_dtype=BF16)
        gate_base = off_gate // LANES * LANES
        gate0 = off_gate - gate_base
        gates = _proj(h, wl, bl, w_t=True, col_off=gate_base, mode="sigmoid", out_dtype=BF16)
        c_t = _forget_cumsum(lf_t, B, S)
        att = _fox_attention(qkv, c_t.transpose(0, 2, 1), c_t, B, S, H, dh)
        pp = _pool_mixer(u, w_pool[l], pool_scale[l], gates, gate0, S)
        merged = _proj(att, w_fox_o[l], None, pp, (gates, gate0 + D), mode="merge", out_dtype=BF16)
        xt = _proj(merged, w_out[l], None, xt, mode="residual", out_dtype=F32)
        kv = _proj(_norm(mt, g_mem_kv[l]), w_mem_kv[l], out_dtype=BF16)
        qm = _proj(_norm(xt, g_mem_q[l]), w_mem_q[l], out_dtype=BF16)
        om = _mem_attention(qm, kv, B, S, n_mem)
        xt = _proj(om, w_mem_o[l], None, xt, mode="residual", out_dtype=F32)
        xt = _moe(xt, g_moe[l], w_router[l], b_router[l], w_gate_up[l], b_gate_up[l], w_down[l], b_down[l],
                  g_final, final_norm=l == depth - 1)
    return xt.reshape(B, S, D)
```

```python
import functools

import jax
import jax.numpy as jnp
from jax import lax
from jax.experimental import pallas as pl
from jax.experimental.pallas import tpu as pltpu

F32 = jnp.float32
BF16 = jnp.bfloat16

EPS = 1e-5
POOL_WINDOWS = (2, 4, 8, 16)
POOL_HALO = 16
MEM_HEADS = 4
TOP_K = 4
SWIGLU_LIMIT = 7.0
SWIGLU_ALPHA = 1.702
NEG_BIG = -1e30
LOG2E = 1.4426950408889634

LANES = 128
VMEM_LIMIT_BYTES = 56 * 1024 * 1024


def _tile(dim, pref):
    t = pref
    while t >= 8:
        if dim % t == 0:
            return t
        t //= 2
    return dim


def _params(*sem):
    return pltpu.CompilerParams(dimension_semantics=sem, vmem_limit_bytes=VMEM_LIMIT_BYTES)


def _rms(x, g):
    ms = jnp.mean(x * x, axis=-1, keepdims=True)
    return x * lax.rsqrt(ms + EPS) * g


def _split3(x):
    hi = x.astype(BF16)
    r1 = x - hi.astype(F32)
    mid = r1.astype(BF16)
    lo = (r1 - mid.astype(F32)).astype(BF16)
    return hi, mid, lo


def _norm_kernel(x_ref, g_ref, *refs, with_f):
    h = _rms(x_ref[...], g_ref[...]).astype(BF16)
    if with_f:
        wf_ref, bf_ref, h_ref, lf_ref = refs
        f = lax.dot_general(wf_ref[...], h, (((1,), (1,)), ((), ())), preferred_element_type=F32) + bf_ref[...]
        lf_ref[...] = jnp.minimum(f, 0.0) - jnp.log1p(jnp.exp(-jnp.abs(f)))
    else:
        h_ref, = refs
    h_ref[...] = h


def _norm(x, g, wf_t=None, bf_t=None, *, tm=512):
    M, K = x.shape
    tm = _tile(M, tm)
    with_f = wf_t is not None
    in_specs = [pl.BlockSpec((tm, K), lambda i: (i, 0)), pl.BlockSpec((1, K), lambda i: (0, 0))]
    args = [x, g.reshape(1, K)]
    out_shape = [jax.ShapeDtypeStruct((M, K), BF16)]
    out_specs = [pl.BlockSpec((tm, K), lambda i: (i, 0))]
    if with_f:
        H = wf_t.shape[0]
        in_specs += [pl.BlockSpec((H, K), lambda i: (0, 0)), pl.BlockSpec((H, 1), lambda i: (0, 0))]
        args += [wf_t, bf_t.reshape(H, 1)]
        out_shape.append(jax.ShapeDtypeStruct((H, M), F32))
        out_specs.append(pl.BlockSpec((H, tm), lambda i: (0, i)))
    res = pl.pallas_call(
        functools.partial(_norm_kernel, with_f=with_f),
        grid=(M // tm,),
        in_specs=in_specs, out_specs=out_specs, out_shape=out_shape,
        compiler_params=_params("parallel"),
    )(*args)
    return res if with_f else res[0]


def _lane_window(main_ref, tail_ref, shift):
    if shift == 0:
        return main_ref[...].astype(F32)
    g = jnp.concatenate([main_ref[...], tail_ref[...]], axis=1).astype(F32)
    return pltpu.roll(g, g.shape[1] - shift, axis=1)[:, :main_ref.shape[1]]


def _window_specs(first_col, rows, width, row_of, col_block_of):
    base, shift = first_col // LANES * LANES, first_col % LANES
    assert base % width == 0
    specs = [pl.BlockSpec((rows, width), lambda *g: (row_of(*g), col_block_of(*g) + base // width))]
    if shift:
        specs.append(pl.BlockSpec(
            (rows, LANES), lambda *g: (row_of(*g), (base + (col_block_of(*g) + 1) * width) // LANES)))
    return specs, shift


def _proj_kernel(x_ref, w_ref, *refs, mode, has_bias, gate_shift, w_rows_are_outputs):
    refs = list(refs)
    wb_ref = refs.pop()
    o_ref = refs.pop()
    b_ref = refs.pop(0) if has_bias else None

    @pl.when(pl.program_id(1) == 0)
    def _():
        w = w_ref[...]
        wb_ref[...] = (w.T if w_rows_are_outputs else w).astype(BF16)

    acc = jnp.dot(x_ref[...], wb_ref[...], preferred_element_type=F32)
    if has_bias:
        acc = acc + b_ref[...]
    if mode == "sigmoid":
        acc = jax.nn.sigmoid(acc)
    elif mode == "merge":
        pp_ref, g_ref = refs[0], refs[1]
        acc = pp_ref[...] + _lane_window(g_ref, refs[2] if gate_shift else None, gate_shift) * acc
    elif mode == "residual":
        acc = refs[0][...] + acc
    o_ref[...] = acc.astype(o_ref.dtype)


def _proj(x, w, b=None, extra=None, gate=None, *, w_t=False, col_off=0, n_cols=None, mode="plain", out_dtype,
          tm=1024, tn=512):
    M, K = x.shape
    N = w.shape[0 if w_t else 1] - col_off if n_cols is None else n_cols
    tm = _tile(M, tm)
    tn = min(tn, N)
    while col_off % tn:
        tn //= 2
    assert tn % LANES == 0
    if w_t:
        w_spec = pl.BlockSpec((tn, K), lambda j, i, o=col_off // tn: (j + o, 0))
    else:
        w_spec = pl.BlockSpec((K, tn), lambda j, i, o=col_off // tn: (0, j + o))
    in_specs = [pl.BlockSpec((tm, K), lambda j, i: (i, 0)), w_spec]
    args = [x, w]
    if b is not None:
        in_specs.append(pl.BlockSpec((1, tn), lambda j, i, o=col_off // tn: (0, j + o)))
        args.append(b.reshape(1, -1))
    if extra is not None:
        in_specs.append(pl.BlockSpec((tm, tn), lambda j, i: (i, j)))
        args.append(extra)
    gate_shift = 0
    if gate is not None:
        specs, gate_shift = _window_specs(gate[1], tm, tn, lambda j, i: i, lambda j, i: j)
        in_specs += specs
        args += [gate[0]] * len(specs)
    return pl.pallas_call(
        functools.partial(_proj_kernel, mode=mode, has_bias=b is not None, gate_shift=gate_shift,
                          w_rows_are_outputs=w_t),
        grid=(pl.cdiv(N, tn), M // tm),
        in_specs=in_specs,
        out_specs=pl.BlockSpec((tm, tn), lambda j, i: (i, j)),
        out_shape=jax.ShapeDtypeStruct((M, N), out_dtype),
        scratch_shapes=[pltpu.VMEM((K, tn), BF16)],
        compiler_params=_params("parallel", "arbitrary"),
    )(*args)


def _cumsum_kernel(lf_ref, c_ref):
    S = lf_ref.shape[1]
    row = lax.broadcasted_iota(jnp.int32, (S, S), 0)
    col = lax.broadcasted_iota(jnp.int32, (S, S), 1)
    upper = (row <= col).astype(BF16)
    c = jnp.zeros(lf_ref.shape, F32)
    for part in _split3(lf_ref[...]):
        c = c + jnp.dot(part, upper, preferred_element_type=F32)
    c_ref[...] = c


def _forget_cumsum(lf_t, B, S):
    H = lf_t.shape[0]
    return pl.pallas_call(
        _cumsum_kernel,
        grid=(B,),
        in_specs=[pl.BlockSpec((H, S), lambda b: (0, b))],
        out_specs=pl.BlockSpec((None, H, S), lambda b: (b, 0, 0)),
        out_shape=jax.ShapeDtypeStruct((B, H, S), F32),
        compiler_params=_params("parallel"),
    )(lf_t)


def _pool_kernel(u_ref, halo_ref, w_ref, sc_ref, *refs, seq_len, gate_shift):
    g0_ref, g0_tail_ref = (refs[0], refs[1]) if gate_shift else (refs[0], None)
    o_ref, ext_ref, pooled_ref = refs[-3:]
    g = pl.program_id(0)
    i = pl.program_id(1)
    tp = u_ref.shape[0]
    pos0 = (i * tp) % seq_len
    u = u_ref[...]
    ext_ref[pl.ds(POOL_HALO, tp), :] = u
    ext_ref[pl.ds(0, POOL_HALO), :] = jnp.where(pos0 == 0, 0.0, halo_ref[...])
    pos = pos0 + lax.broadcasted_iota(jnp.int32, (tp, 1), 0)
    for gi, win in enumerate(POOL_WINDOWS):
        @pl.when(g == gi)
        def _(win=win):
            acc = u
            for k in range(1, win):
                acc = acc + ext_ref[pl.ds(POOL_HALO - k, tp), :]
            cnt = jnp.minimum(pos + 1, win).astype(F32)
            pooled_ref[...] = (acc / cnt - u).astype(BF16)
    y = jnp.dot(pooled_ref[...], w_ref[...].astype(BF16), preferred_element_type=F32)
    o_ref[...] = _lane_window(g0_ref, g0_tail_ref, gate_shift) * (y * sc_ref[...])


def _pool_mixer(u, w_pool, scale, gates, gate0_col, seq_len, *, tp=512):
    T = u.shape[0]
    G, C, Do = w_pool.shape
    tp = _tile(seq_len, tp)
    assert tp % POOL_HALO == 0
    hb = tp // POOL_HALO
    gate_specs, gate_shift = _window_specs(gate0_col, tp, Do, lambda g, i: i, lambda g, i: g)
    return pl.pallas_call(
        functools.partial(_pool_kernel, seq_len=seq_len, gate_shift=gate_shift),
        grid=(G, T // tp),
        in_specs=[pl.BlockSpec((tp, C), lambda g, i: (i, g)),
                  pl.BlockSpec((POOL_HALO, C), lambda g, i: (jnp.maximum(i * hb - 1, 0), g)),
                  pl.BlockSpec((None, C, Do), lambda g, i: (g, 0, 0)),
                  pl.BlockSpec((1, Do), lambda g, i: (0, g))] + gate_specs,
        out_specs=pl.BlockSpec((tp, Do), lambda g, i: (i, g)),
        out_shape=jax.ShapeDtypeStruct((T, G * Do), F32),
        scratch_shapes=[pltpu.VMEM((tp + POOL_HALO, C), F32), pltpu.VMEM((tp, C), BF16)],
        compiler_params=_params("parallel", "parallel"),
    )(u, u, w_pool, scale.reshape(1, G * Do), *([gates] * len(gate_specs)))


def _fox_kernel(q_ref, k_ref, v_ref, cq_ref, ck_ref, o_ref, *, scale, dh):
    hg = pl.program_id(1)
    qi = pl.program_id(2)
    tq = q_ref.shape[0]
    n_heads = q_ref.shape[1] // dh
    lane = lax.broadcasted_iota(jnp.int32, cq_ref.shape, 1)
    cq_all = cq_ref[...]
    q = [q_ref[:, n * dh:(n + 1) * dh] for n in range(n_heads)]
    cq2 = [LOG2E * jnp.sum(jnp.where(lane == hg * n_heads + n, cq_all, 0.0), axis=1, keepdims=True)
           for n in range(n_heads)]

    def scores(n, j):
        rows = pl.ds(pl.multiple_of(j * tq, tq), tq)
        s = lax.dot_general(q[n], k_ref[rows, n * dh:(n + 1) * dh], (((1,), (1,)), ((), ())),
                            preferred_element_type=F32) * (scale * LOG2E)
        return s - LOG2E * ck_ref[n:n + 1, rows]

    def update(n, j, s, carry):
        m, l, acc = carry
        m_new = jnp.maximum(m, jnp.max(s, axis=1, keepdims=True) + cq2[n])
        alpha = jnp.exp2(m - m_new)
        p = jnp.exp2(s - (m_new - cq2[n]))
        v = v_ref[pl.ds(pl.multiple_of(j * tq, tq), tq), n * dh:(n + 1) * dh]
        acc = alpha * acc + jnp.dot(p.astype(BF16), v, preferred_element_type=F32)
        return m_new, alpha * l + jnp.sum(p, axis=1, keepdims=True), acc

    init = (jnp.full((tq, 1), NEG_BIG, F32), jnp.zeros((tq, 1), F32), jnp.zeros((tq, dh), F32))
    carries = lax.fori_loop(
        0, qi, lambda j, cs: tuple(update(n, j, scores(n, j), cs[n]) for n in range(n_heads)), (init,) * n_heads)
    row = lax.broadcasted_iota(jnp.int32, (tq, tq), 0)
    col = lax.broadcasted_iota(jnp.int32, (tq, tq), 1)
    for n in range(n_heads):
        s = jnp.where(col <= row, scores(n, qi), NEG_BIG)
        m, l, acc = update(n, qi, s, carries[n])
        o_ref[:, n * dh:(n + 1) * dh] = (acc / l).astype(o_ref.dtype)


def _fox_attention(qkv, c, c_t, B, S, H, dh, *, tq=512, heads_per_step=2):
    T = B * S
    tq = _tile(S, tq)
    nq = S // tq
    hp = heads_per_step if H % heads_per_step == 0 else 1
    ng = H // hp
    w = hp * dh
    return pl.pallas_call(
        functools.partial(_fox_kernel, scale=dh ** -0.5, dh=dh),
        grid=(B, ng, nq),
        in_specs=[pl.BlockSpec((tq, w), lambda b, h, i: (b * nq + i, h)),
                  pl.BlockSpec((S, w), lambda b, h, i: (b, ng + h)),
                  pl.BlockSpec((S, w), lambda b, h, i: (b, 2 * ng + h)),
                  pl.BlockSpec((None, tq, H), lambda b, h, i: (b, i, 0)),
                  pl.BlockSpec((None, None, hp, S), lambda b, h, i: (b, h, 0, 0))],
        out_specs=pl.BlockSpec((tq, w), lambda b, h, i: (b * nq + i, h)),
        out_shape=jax.ShapeDtypeStruct((T, H * dh), BF16),
        compiler_params=_params("parallel", "parallel", "parallel"),
    )(qkv, qkv, qkv, c, c_t.reshape(B, ng, hp, S))


def _mem_attn_kernel(q_ref, kv_ref, o_ref, *, heads):
    width = q_ref.shape[1]
    dh = width // heads
    scale = dh ** -0.5
    for hd in range(heads):
        q = q_ref[:, hd * dh:(hd + 1) * dh]
        k = kv_ref[:, hd * dh:(hd + 1) * dh]
        v = kv_ref[:, width + hd * dh:width + (hd + 1) * dh]
        s = lax.dot_general(q, k, (((1,), (1,)), ((), ())), preferred_element_type=F32) * scale
        p = jnp.exp(s - jnp.max(s, axis=1, keepdims=True))
        p = p / jnp.sum(p, axis=1, keepdims=True)
        o_ref[:, hd * dh:(hd + 1) * dh] = jnp.dot(p.astype(BF16), v, preferred_element_type=F32).astype(o_ref.dtype)


def _mem_attention(q, kv, B, S, n_mem, *, tq=512):
    T, width = q.shape
    tq = _tile(S, tq)
    nq = S // tq
    return pl.pallas_call(
        functools.partial(_mem_attn_kernel, heads=MEM_HEADS),
        grid=(B, nq),
        in_specs=[pl.BlockSpec((tq, width), lambda b, i: (b * nq + i, 0)),
                  pl.BlockSpec((n_mem, 2 * width), lambda b, i: (b, 0))],
        out_specs=pl.BlockSpec((tq, width), lambda b, i: (b * nq + i, 0)),
        out_shape=jax.ShapeDtypeStruct((T, width), BF16),
        compiler_params=_params("parallel", "parallel"),
    )(q, kv)


def _router_kernel(x_ref, g_ref, whi_ref, wlo_ref, b_ref, oi_ref, og_ref, cnt_ref, carry_ref, *, n_experts):
    i = pl.program_id(0)
    tm = x_ref.shape[0]

    @pl.when(i == 0)
    def _():
        carry_ref[...] = jnp.zeros_like(carry_ref)

    h = _rms(x_ref[...], g_ref[...])
    h_hi = h.astype(BF16)
    h_lo = (h - h_hi.astype(F32)).astype(BF16)
    logits = (jnp.dot(h_hi, whi_ref[...], preferred_element_type=F32)
              + jnp.dot(h_hi, wlo_ref[...], preferred_element_type=F32)
              + jnp.dot(h_lo, whi_ref[...], preferred_element_type=F32)) + b_ref[...]
    lane = lax.broadcasted_iota(jnp.int32, (tm, LANES), 1).astype(F32)
    work = jnp.where(lane < n_experts, logits, -jnp.inf)
    vals, idxs = [], []
    for _ in range(TOP_K):
        m = jnp.max(work, axis=1, keepdims=True)
        idx = jnp.min(jnp.where(work == m, lane, float(LANES)), axis=1, keepdims=True)
        vals.append(m)
        idxs.append(idx)
        work = jnp.where(lane == idx, -jnp.inf, work)
    exps = [jnp.exp(v - vals[0]) for v in vals]
    denom = exps[0] + exps[1] + exps[2] + exps[3]
    onehots = [(lane == idx).astype(F32) for idx in idxs]
    chosen = onehots[0] + onehots[1] + onehots[2] + onehots[3]
    row = lax.broadcasted_iota(jnp.int32, (tm, tm), 0)
    col = lax.broadcasted_iota(jnp.int32, (tm, tm), 1)
    before = jnp.dot((col < row).astype(BF16), chosen.astype(BF16), preferred_element_type=F32) + carry_ref[...]
    out_i = jnp.zeros((tm, LANES), F32)
    out_g = jnp.zeros((tm, LANES), F32)
    for k in range(TOP_K):
        rank = jnp.sum(onehots[k] * before, axis=1, keepdims=True)
        out_i = jnp.where(lane == k, idxs[k], out_i)
        out_i = jnp.where(lane == TOP_K + k, rank, out_i)
        out_g = jnp.where(lane == k, exps[k] / denom, out_g)
    oi_ref[...] = out_i.astype(jnp.int32)
    og_ref[...] = out_g
    carry_ref[...] = carry_ref[...] + jnp.sum(chosen, axis=0, keepdims=True)
    cnt_ref[...] = carry_ref[...]


def _router(x, g, w_router, b_router, *, tm=512):
    T, D = x.shape
    E = w_router.shape[1]
    tm = _tile(T, tm)
    w_pad = jnp.zeros((D, LANES), F32).at[:, :E].set(w_router)
    w_hi = w_pad.astype(BF16)
    w_lo = (w_pad - w_hi.astype(F32)).astype(BF16)
    b_pad = jnp.zeros((1, LANES), F32).at[0, :E].set(b_router)
    full = lambda i: (0, 0)
    return pl.pallas_call(
        functools.partial(_router_kernel, n_experts=E),
        grid=(T // tm,),
        in_specs=[pl.BlockSpec((tm, D), lambda i: (i, 0)), pl.BlockSpec((1, D), full),
                  pl.BlockSpec((D, LANES), full), pl.BlockSpec((D, LANES), full), pl.BlockSpec((1, LANES), full)],
        out_specs=[pl.BlockSpec((tm, LANES), lambda i: (i, 0)), pl.BlockSpec((tm, LANES), lambda i: (i, 0)),
                   pl.BlockSpec((1, LANES), full)],
        out_shape=[jax.ShapeDtypeStruct((T, LANES), jnp.int32), jax.ShapeDtypeStruct((T, LANES), F32),
                   jax.ShapeDtypeStruct((1, LANES), F32)],
        scratch_shapes=[pltpu.VMEM((1, LANES), F32)],
        compiler_params=_params("arbitrary"),
    )(x, g.reshape(1, D), w_hi, w_lo, b_pad)


def _row_copy(src_hbm, row, buf, slot, r, sem):
    return pltpu.make_async_copy(src_hbm.at[pl.ds(row, 1)], buf.at[slot, pl.ds(r, 1)], sem.at[slot])


def _start_rows(idx_ref, src_hbm, buf, slot, sem, n_rows):
    def body(r, _):
        _row_copy(src_hbm, idx_ref[0, 0, r], buf, slot, r, sem).start()
        return 0
    lax.fori_loop(0, n_rows, body, 0, unroll=8)


def _wait_rows(src_hbm, buf, slot, sem, n_rows):
    pltpu.make_async_copy(src_hbm.at[pl.ds(0, n_rows)], buf.at[slot], sem.at[slot]).wait()


def _gather_pipeline(idx_ref, nxt_ref, src_hbm, buf, sem, n_rows):
    i = pl.program_id(0)
    n = pl.num_programs(0)
    slot = i % 2

    @pl.when(i == 0)
    def _():
        _start_rows(idx_ref, src_hbm, buf, 0, sem, n_rows)

    @pl.when(i + 1 < n)
    def _():
        _start_rows(nxt_ref, src_hbm, buf, 1 - slot, sem, n_rows)

    _wait_rows(src_hbm, buf, slot, sem, n_rows)
    return slot


def _pack_bf16_pairs(h):
    half = h.shape[1] // 2
    bits = pltpu.bitcast(h.astype(BF16).astype(F32), jnp.uint32)
    return (bits[:, :half] >> 16) | bits[:, half:]


def _unpack_bf16_pairs(u):
    lo = pltpu.bitcast(u << 16, F32).astype(BF16)
    hi = pltpu.bitcast(u & jnp.uint32(0xFFFF0000), F32).astype(BF16)
    return lo, hi


def _scatter_norm_kernel(pe_ref, pd_ref, dest_ref, x_ref, g_ref, xs_hbm, buf, zbuf, sem, zsem, *, tm, n_experts):
    i = pl.program_id(0)
    n = pl.num_programs(0)
    tb = x_ref.shape[0]
    slot = i % 2

    def row_copy(s, t, row):
        return pltpu.make_async_copy(buf.at[s, pl.ds(t, 1)], xs_hbm.at[pl.ds(row, 1)], sem.at[s])

    def wait_slot(s):
        for _ in range(TOP_K):
            pltpu.make_async_copy(buf.at[s], xs_hbm.at[pl.ds(0, tb)], sem.at[s]).wait()

    def zero_group(first_row):
        return pltpu.make_async_copy(zbuf, xs_hbm.at[pl.ds(pl.multiple_of(first_row, tm), tm)], zsem)

    @pl.when(i == 0)
    def _():
        zbuf[...] = jnp.zeros_like(zbuf)
        n_groups = xs_hbm.shape[0] // tm
        used = pe_ref[n_experts - 1] // tm
        for e in range(n_experts):
            @pl.when(pd_ref[e] > 0)
            def _(e=e):
                zero_group(pe_ref[e] - tm).start()
        lax.fori_loop(used, n_groups, lambda gi, c: (zero_group(gi * tm).start(), c)[1], 0)
        for e in range(n_experts):
            @pl.when(pd_ref[e] > 0)
            def _(e=e):
                zero_group(pe_ref[e] - tm).wait()
        lax.fori_loop(used, n_groups, lambda gi, c: (zero_group(gi * tm).wait(), c)[1], 0)

    @pl.when(i >= 2)
    def _():
        wait_slot(slot)

    buf[slot] = _pack_bf16_pairs(_rms(x_ref[...], g_ref[...]))

    def body(t, _):
        for k in range(TOP_K):
            row_copy(slot, t, dest_ref[0, 0, t * TOP_K + k]).start()
        return 0
    lax.fori_loop(0, tb, body, 0, unroll=4)

    @pl.when(i == n - 1)
    def _():
        @pl.when(n >= 2)
        def _():
            wait_slot(1 - slot)
        wait_slot(slot)


def _scatter_norm(x, g, dest, pad_ends, padded, n_rows, *, tm, tb=128):
    T, D = x.shape
    E = pad_ends.shape[0]
    tb = _tile(T, tb)
    nb = T // tb
    grid_spec = pltpu.PrefetchScalarGridSpec(
        num_scalar_prefetch=2, grid=(nb,),
        in_specs=[pl.BlockSpec((1, 1, TOP_K * tb), lambda i, pe, pd: (i, 0, 0), memory_space=pltpu.SMEM),
                  pl.BlockSpec((tb, D), lambda i, pe, pd: (i, 0)),
                  pl.BlockSpec((1, D), lambda i, pe, pd: (0, 0))],
        out_specs=pl.BlockSpec(memory_space=pl.ANY),
        scratch_shapes=[pltpu.VMEM((2, tb, D // 2), jnp.uint32), pltpu.VMEM((tm, D // 2), jnp.uint32),
                        pltpu.SemaphoreType.DMA((2,)), pltpu.SemaphoreType.DMA(())])
    return pl.pallas_call(
        functools.partial(_scatter_norm_kernel, tm=tm, n_experts=E), grid_spec=grid_spec,
        out_shape=jax.ShapeDtypeStruct((n_rows, D // 2), jnp.uint32),
        compiler_params=_params("arbitrary"),
    )(pad_ends, padded, dest.reshape(nb, 1, TOP_K * tb), x, g.reshape(1, D))


def _combine_kernel(idx0_ref, idx1_ref, idx2_ref, x_ref, gate_ref, g_ref, rows_hbm, o_ref, buf0, buf1, buf2, sem, *,
                    final_norm):
    i = pl.program_id(0)
    n = pl.num_programs(0)
    tb = x_ref.shape[0]
    n_rows = TOP_K * tb
    bufs = (buf0, buf1, buf2)

    def issue(idx_ref, k):
        for r in range(n_rows):
            pltpu.make_async_copy(rows_hbm.at[pl.ds(idx_ref[0, 0, r], 1)], bufs[k].at[pl.ds(r, 1)], sem.at[k]).start()

    def wait(k):
        pltpu.make_async_copy(rows_hbm.at[pl.ds(0, n_rows)], bufs[k], sem.at[k]).wait()

    @pl.when(i == 0)
    def _():
        issue(idx0_ref, 0)
        issue(idx1_ref, 1)

    for k in range(3):
        @pl.when(i % 3 == k)
        def _(k=k):
            wait(k)
            issue(idx2_ref, (k + 2) % 3)
            y = x_ref[...]
            gates = gate_ref[...]
            for j in range(TOP_K):
                y = y + gates[:, j:j + 1] * bufs[k][pl.ds(j * tb, tb), :]
            o_ref[...] = _rms(y, g_ref[...]) if final_norm else y

            @pl.when(i == n - 1)
            def _():
                wait((k + 1) % 3)
                wait((k + 2) % 3)


def _combine(x, rows, dest, gates_slab, g, final_norm, *, tb=64):
    T, D = x.shape
    tb = _tile(T, tb)
    nb = T // tb
    idx3 = dest.reshape(nb, tb, TOP_K).transpose(0, 2, 1).reshape(nb, 1, TOP_K * tb)
    return pl.pallas_call(
        functools.partial(_combine_kernel, final_norm=final_norm),
        grid=(nb,),
        in_specs=[pl.BlockSpec((1, 1, TOP_K * tb), lambda i, a=ahead: (jnp.minimum(i + a, nb - 1), 0, 0),
                               memory_space=pltpu.SMEM) for ahead in range(3)] + [
                  pl.BlockSpec((tb, D), lambda i: (i, 0)),
                  pl.BlockSpec((tb, LANES), lambda i: (i, 0)),
                  pl.BlockSpec((1, D), lambda i: (0, 0)),
                  pl.BlockSpec(memory_space=pl.ANY)],
        out_specs=pl.BlockSpec((tb, D), lambda i: (i, 0)),
        out_shape=jax.ShapeDtypeStruct((T, D), F32),
        scratch_shapes=[pltpu.VMEM((TOP_K * tb, D), F32)] * 3 + [pltpu.SemaphoreType.DMA((3,))],
        compiler_params=_params("arbitrary"),
    )(idx3, idx3, idx3, x, gates_slab, g.reshape(1, D), rows)


MOE_GROUP_ROWS = 256
MOE_GROUPS_PER_STEP = 5
MOE_BLOCK_SIZES = (5, 4, 1)
MOE_BLOCK_GROUPS = max(MOE_BLOCK_SIZES)
MOE_UP_COLS = 512
MOE_DOWN_COLS = 512


def _swiglu_tile(lo, hi, wb_ref, b):
    half = lo.shape[1]
    pair = 2 * LANES
    row = lax.broadcasted_iota(jnp.int32, (pair, LANES), 0)
    col = lax.broadcasted_iota(jnp.int32, (pair, LANES), 1)
    pick_even = (row == 2 * col).astype(BF16)
    even = lax.broadcasted_iota(jnp.int32, (1, pair), 1) % 2 == 0
    gu = (jnp.dot(lo, wb_ref[:half, :], preferred_element_type=F32)
          + jnp.dot(hi, wb_ref[half:, :], preferred_element_type=F32)) + b
    outs = []
    for c in range(gu.shape[1] // pair):
        blk = gu[:, c * pair:(c + 1) * pair]
        nxt = pltpu.roll(blk, pair - 1, axis=1)
        gate = jnp.minimum(blk, SWIGLU_LIMIT)
        up = jnp.clip(nxt, -SWIGLU_LIMIT, SWIGLU_LIMIT)
        act = (up + 1.0) * gate * jax.nn.sigmoid(SWIGLU_ALPHA * gate)
        act = jnp.where(even, act, 0.0).astype(BF16)
        outs.append(jnp.dot(act, pick_even, preferred_element_type=F32).astype(BF16))
    return outs[0] if len(outs) == 1 else jnp.concatenate(outs, axis=1)


def _experts_kernel(se_ref, sr_ref, sn_ref, cnt_ref, bgu_ref, bd_ref, xs_hbm, wgu_hbm, wd_hbm, rows_hbm,
                    xs_v, act_v, wu_f, wu_b, wd_f, wd_b, out_v, xs_sem, wu_sem, wd_sem, out_sem, *, tm):
    s = pl.program_id(0)
    n_active = cnt_ref[0]
    tnu, tnd = wu_b.shape[1], wd_b.shape[1]
    n_up, n_down = wgu_hbm.shape[2] // tnu, wd_hbm.shape[2] // tnd

    def wu_copy(step, c, slot):
        cols = pl.ds(pl.multiple_of(c * tnu, tnu), tnu)
        return pltpu.make_async_copy(wgu_hbm.at[se_ref[step], :, cols], wu_f.at[slot], wu_sem.at[slot])

    def wd_copy(step, c, slot):
        cols = pl.ds(pl.multiple_of(c * tnd, tnd), tnd)
        return pltpu.make_async_copy(wd_hbm.at[se_ref[step], :, cols], wd_f.at[slot], wd_sem.at[slot])

    def xs_copy(step, g):
        return pltpu.make_async_copy(xs_hbm.at[pl.ds(pl.multiple_of(sr_ref[step] + g * tm, tm), tm)],
                                     xs_v.at[pl.ds(pl.multiple_of(g * tm, tm), tm)], xs_sem)

    def out_copy(slot, row, c):
        cols = pl.ds(pl.multiple_of(c * tnd, tnd), tnd)
        return pltpu.make_async_copy(
            out_v.at[slot], rows_hbm.at[pl.ds(pl.multiple_of(row, tm), tm), cols], out_sem.at[slot])

    def start_rows(step):
        lax.fori_loop(0, sn_ref[step], lambda g, carry: (xs_copy(step, g).start(), carry)[1], 0)

    def for_row_blocks(n_groups, fn, carry):
        done = 0
        for size in MOE_BLOCK_SIZES:
            n_blocks = (n_groups - done) // size
            carry = lax.fori_loop(0, n_blocks, lambda b, cr, d=done, sz=size: fn(d + b * sz, sz, cr), carry)
            done = done + n_blocks * size
        return carry

    @pl.when(s < n_active)
    def _():
        ng = sn_ref[s]
        row0 = sr_ref[s]

        @pl.when(s == 0)
        def _():
            start_rows(0)
            wu_copy(0, 0, 0).start()

        lax.fori_loop(0, ng, lambda g, carry: (xs_copy(s, g).wait(), carry)[1], 0)

        def up_chunk(c, carry):
            slot = c % 2
            wu_copy(s, c, slot).wait()

            @pl.when(c + 1 < n_up)
            def _():
                wu_copy(s, c + 1, 1 - slot).start()

            @pl.when(c + 1 == n_up)
            def _():
                wd_copy(s, 0, 0).start()

            wu_b[...] = wu_f[slot].astype(BF16)
            bias = bgu_ref[c]

            def up_rows(g0, n_g, carry):
                rows = pl.ds(pl.multiple_of(g0 * tm, tm), n_g * tm)
                lo, hi = _unpack_bf16_pairs(xs_v[rows, :])
                tile = _swiglu_tile(lo, hi, wu_b, bias)
                for cc in range(n_up):
                    @pl.when(c == cc)
                    def _(cc=cc):
                        act_v[rows, cc * (tnu // 2):(cc + 1) * (tnu // 2)] = tile
                return carry
            return for_row_blocks(ng, up_rows, carry)
        lax.fori_loop(0, n_up, up_chunk, 0)

        @pl.when(s + 1 < n_active)
        def _():
            start_rows(s + 1)

        def drain(pending):
            for k in range(MOE_BLOCK_GROUPS):
                @pl.when(pending[k] == 1)
                def _(k=k):
                    out_copy(k, 0, 0).wait()

        def down_chunk(c, pending):
            slot = c % 2
            wd_copy(s, c, slot).wait()

            @pl.when(c + 1 < n_down)
            def _():
                wd_copy(s, c + 1, 1 - slot).start()

            @pl.when(jnp.logical_and(c + 1 == n_down, s + 1 < n_active))
            def _():
                wu_copy(s + 1, 0, 0).start()

            wd_b[...] = wd_f[slot].astype(BF16)
            bias = bd_ref[c]

            def down_rows(g0, n_g, pending):
                rows = pl.ds(pl.multiple_of(g0 * tm, tm), n_g * tm)
                o = jnp.dot(act_v[rows, :], wd_b[...], preferred_element_type=F32) + bias
                drain(pending)
                for k in range(n_g):
                    out_v[k] = o[k * tm:(k + 1) * tm]
                    out_copy(k, row0 + (g0 + k) * tm, c).start()
                return tuple(jnp.int32(1 if k < n_g else 0) for k in range(MOE_BLOCK_GROUPS))
            return for_row_blocks(ng, down_rows, pending)
        drain(lax.fori_loop(0, n_down, down_chunk, (jnp.int32(0),) * MOE_BLOCK_GROUPS))

    @pl.when(s == pl.num_programs(0) - 1)
    def _():
        out_v[0] = jnp.zeros(out_v.shape[1:], out_v.dtype)
        first, last = cnt_ref[1], rows_hbm.shape[0] // tm

        def fill(gi, carry, wait):
            for c in range(n_down):
                cp = out_copy(0, gi * tm, c)
                cp.wait() if wait else cp.start()
            return carry
        lax.fori_loop(first, last, functools.partial(fill, wait=False), 0)
        lax.fori_loop(first, last, functools.partial(fill, wait=True), 0)


def _experts(xs, w_gu, b_gu, w_d, b_d, step_expert, step_row, step_groups, counts, *, tm):
    R, half = xs.shape
    E, D, F2 = w_gu.shape
    F = F2 // 2
    tnu, tnd = min(MOE_UP_COLS, F2), min(MOE_DOWN_COLS, D)
    assert D == 2 * half and F2 % tnu == 0 and D % tnd == 0 and tnu % (2 * LANES) == 0 and D // tnd >= 2
    assert MOE_BLOCK_SIZES[-1] == 1 and MOE_BLOCK_GROUPS <= MOE_GROUPS_PER_STEP
    cap = MOE_GROUPS_PER_STEP * tm
    n_steps = step_expert.shape[0]
    expert_of = lambda s, se, sr, sn, cnt: (se[jnp.minimum(s, cnt[0] - 1)], 0, 0, 0)
    grid_spec = pltpu.PrefetchScalarGridSpec(
        num_scalar_prefetch=4, grid=(n_steps,),
        in_specs=[pl.BlockSpec((None, F2 // tnu, 1, tnu), expert_of),
                  pl.BlockSpec((None, D // tnd, 1, tnd), expert_of),
                  pl.BlockSpec(memory_space=pl.ANY), pl.BlockSpec(memory_space=pl.ANY),
                  pl.BlockSpec(memory_space=pl.ANY)],
        out_specs=pl.BlockSpec(memory_space=pl.ANY),
        scratch_shapes=[pltpu.VMEM((cap, half), jnp.uint32), pltpu.VMEM((cap, F), BF16),
                        pltpu.VMEM((2, D, tnu), F32), pltpu.VMEM((D, tnu), BF16),
                        pltpu.VMEM((2, F, tnd), F32), pltpu.VMEM((F, tnd), BF16),
                        pltpu.VMEM((MOE_BLOCK_GROUPS, tm, tnd), F32),
                        pltpu.SemaphoreType.DMA(()), pltpu.SemaphoreType.DMA((2,)),
                        pltpu.SemaphoreType.DMA((2,)), pltpu.SemaphoreType.DMA((MOE_BLOCK_GROUPS,))])
    return pl.pallas_call(
        functools.partial(_experts_kernel, tm=tm), grid_spec=grid_spec,
        out_shape=jax.ShapeDtypeStruct((R, D), F32),
        compiler_params=_params("arbitrary"),
    )(step_expert, step_row, step_groups, counts, b_gu.reshape(E, F2 // tnu, 1, tnu),
      b_d.reshape(E, D // tnd, 1, tnd), xs, w_gu, w_d)


def _moe(x, g_moe, w_router, b_router, w_gate_up, b_gate_up, w_down, b_down, g_final, final_norm):
    T, D = x.shape
    E = w_router.shape[1]
    A = T * TOP_K
    tm = min(MOE_GROUP_ROWS, A)
    slab_i, slab_g, cnt = _router(x, g_moe, w_router, b_router)
    idx = slab_i[:, :TOP_K]
    rank = slab_i[:, TOP_K:2 * TOP_K]
    counts = cnt[0, :E].astype(jnp.int32)
    groups = (counts + tm - 1) // tm
    pad_ends = jnp.cumsum(groups) * tm
    pad_starts = pad_ends - groups * tm
    dest = pad_starts[idx] + rank
    n_groups = (A + tm - 1) // tm + E
    R = n_groups * tm
    per = MOE_GROUPS_PER_STEP
    runs = (groups + per - 1) // per
    run_ends = jnp.cumsum(runs)
    n_steps = (n_groups + per - 1) // per + E
    step = jnp.arange(n_steps, dtype=jnp.int32)
    step_expert = jnp.minimum(jnp.sum(step[:, None] >= run_ends[None, :], axis=1), E - 1).astype(jnp.int32)
    local = step - (run_ends - runs)[step_expert]
    step_row = (pad_starts[step_expert] + local * per * tm).astype(jnp.int32)
    step_groups = jnp.clip(groups[step_expert] - local * per, 0, per).astype(jnp.int32)
    step_counts = jnp.stack([run_ends[-1], pad_ends[-1] // tm]).astype(jnp.int32)

    xs = _scatter_norm(x, g_moe, dest, pad_ends, groups * tm, R, tm=tm)
    rows = _experts(xs, w_gate_up, b_gate_up, w_down, b_down, step_expert, step_row, step_groups, step_counts, tm=tm)
    return _combine(x, rows, dest, slab_g, g_final, final_norm)


def kernel(x, mem, g_mix, w_in, b_in, w_pool, pool_scale, w_fox_o, w_out, g_mem_q, g_mem_kv, w_mem_q, w_mem_kv,
           w_mem_o, g_moe, w_router, b_router, w_gate_up, b_gate_up, w_down, b_down, g_final):
    B, S, D = x.shape
    T = B * S
    n_mem = mem.shape[1]
    depth, G, C, Do = w_pool.shape
    pool_w = G * C
    fox_w = w_fox_o.shape[1]
    H = w_in.shape[2] - pool_w - 3 * fox_w - 2 * D
    dh = fox_w // H
    off_q, off_f = pool_w, pool_w + 3 * fox_w
    off_gate = off_f + H
    xt = x.reshape(T, D)
    mt = mem.reshape(B * n_mem, D)
    for l in range(depth):
        wl, bl = jnp.swapaxes(w_in[l], 0, 1), b_in[l]
        h, lf_t = _norm(xt, g_mix[l], wl[off_f:off_gate].astype(BF16), bl[off_f:off_gate])
        u = _proj(h, wl, bl, w_t=True, col_off=0, n_cols=off_q, out_dtype=F32)
        qkv = _proj(h, wl, bl, w_t=True, col_off=off_q, n_cols=off_f - off_q, out_dtype=BF16)
        gate_base = off_gate // LANES * LANES
        gate0 = off_gate - gate_base
        gates = _proj(h, wl, bl, w_t=True, col_off=gate_base, mode="sigmoid", out_dtype=BF16)
        c_t = _forget_cumsum(lf_t, B, S)
        att = _fox_attention(qkv, c_t.transpose(0, 2, 1), c_t, B, S, H, dh)
        pp = _pool_mixer(u, w_pool[l], pool_scale[l], gates, gate0, S)
        merged = _proj(att, w_fox_o[l], None, pp, (gates, gate0 + D), mode="merge", out_dtype=BF16)
        xt = _proj(merged, w_out[l], None, xt, mode="residual", out_dtype=F32)
        kv = _proj(_norm(mt, g_mem_kv[l]), w_mem_kv[l], out_dtype=BF16)
        qm = _proj(_norm(xt, g_mem_q[l]), w_mem_q[l], out_dtype=BF16)
        om = _mem_attention(qm, kv, B, S, n_mem)
        xt = _proj(om, w_mem_o[l], None, xt, mode="residual", out_dtype=F32)
        xt = _moe(xt, g_moe[l], w_router[l], b_router[l], w_gate_up[l], b_gate_up[l], w_down[l], b_down[l],
                  g_final, final_norm=l == depth - 1)
    return xt.reshape(B, S, D)
```

```python
import functools

import jax
import jax.numpy as jnp
from jax import lax
from jax.experimental import pallas as pl
from jax.experimental.pallas import tpu as pltpu

F32 = jnp.float32
BF16 = jnp.bfloat16

EPS = 1e-5
POOL_WINDOWS = (2, 4, 8, 16)
POOL_HALO = 16
MEM_HEADS = 4
TOP_K = 4
SWIGLU_LIMIT = 7.0
SWIGLU_ALPHA = 1.702
NEG_BIG = -1e30
LOG2E = 1.4426950408889634

LANES = 128
VMEM_LIMIT_BYTES = 56 * 1024 * 1024


def _tile(dim, pref):
    t = pref
    while t >= 8:
        if dim % t == 0:
            return t
        t //= 2
    return dim


def _params(*sem):
    return pltpu.CompilerParams(dimension_semantics=sem, vmem_limit_bytes=VMEM_LIMIT_BYTES)


def _rms(x, g):
    ms = jnp.mean(x * x, axis=-1, keepdims=True)
    return x * lax.rsqrt(ms + EPS) * g


def _split3(x):
    hi = x.astype(BF16)
    r1 = x - hi.astype(F32)
    mid = r1.astype(BF16)
    lo = (r1 - mid.astype(F32)).astype(BF16)
    return hi, mid, lo


def _norm_kernel(x_ref, g_ref, *refs, with_f):
    h = _rms(x_ref[...], g_ref[...]).astype(BF16)
    if with_f:
        wf_ref, bf_ref, h_ref, lf_ref = refs
        f = lax.dot_general(wf_ref[...], h, (((1,), (1,)), ((), ())), preferred_element_type=F32) + bf_ref[...]
        lf_ref[...] = jnp.minimum(f, 0.0) - jnp.log1p(jnp.exp(-jnp.abs(f)))
    else:
        h_ref, = refs
    h_ref[...] = h


def _norm(x, g, wf_t=None, bf_t=None, *, tm=512):
    M, K = x.shape
    tm = _tile(M, tm)
    with_f = wf_t is not None
    in_specs = [pl.BlockSpec((tm, K), lambda i: (i, 0)), pl.BlockSpec((1, K), lambda i: (0, 0))]
    args = [x, g.reshape(1, K)]
    out_shape = [jax.ShapeDtypeStruct((M, K), BF16)]
    out_specs = [pl.BlockSpec((tm, K), lambda i: (i, 0))]
    if with_f:
        H = wf_t.shape[0]
        in_specs += [pl.BlockSpec((H, K), lambda i: (0, 0)), pl.BlockSpec((H, 1), lambda i: (0, 0))]
        args += [wf_t, bf_t.reshape(H, 1)]
        out_shape.append(jax.ShapeDtypeStruct((H, M), F32))
        out_specs.append(pl.BlockSpec((H, tm), lambda i: (0, i)))
    res = pl.pallas_call(
        functools.partial(_norm_kernel, with_f=with_f),
        grid=(M // tm,),
        in_specs=in_specs, out_specs=out_specs, out_shape=out_shape,
        compiler_params=_params("parallel"),
    )(*args)
    return res if with_f else res[0]


def _lane_window(main_ref, tail_ref, shift):
    if shift == 0:
        return main_ref[...].astype(F32)
    g = jnp.concatenate([main_ref[...], tail_ref[...]], axis=1).astype(F32)
    return pltpu.roll(g, g.shape[1] - shift, axis=1)[:, :main_ref.shape[1]]


def _window_specs(first_col, rows, width, row_of, col_block_of):
    base, shift = first_col // LANES * LANES, first_col % LANES
    assert base % width == 0
    specs = [pl.BlockSpec((rows, width), lambda *g: (row_of(*g), col_block_of(*g) + base // width))]
    if shift:
        specs.append(pl.BlockSpec(
            (rows, LANES), lambda *g: (row_of(*g), (base + (col_block_of(*g) + 1) * width) // LANES)))
    return specs, shift


def _proj_kernel(x_ref, w_ref, *refs, mode, has_bias, gate_shift, w_rows_are_outputs):
    refs = list(refs)
    wb_ref = refs.pop()
    o_ref = refs.pop()
    b_ref = refs.pop(0) if has_bias else None

    @pl.when(pl.program_id(1) == 0)
    def _():
        w = w_ref[...]
        wb_ref[...] = (w.T if w_rows_are_outputs else w).astype(BF16)

    acc = jnp.dot(x_ref[...], wb_ref[...], preferred_element_type=F32)
    if has_bias:
        acc = acc + b_ref[...]
    if mode == "sigmoid":
        acc = jax.nn.sigmoid(acc)
    elif mode == "merge":
        pp_ref, g_ref = refs[0], refs[1]
        acc = pp_ref[...] + _lane_window(g_ref, refs[2] if gate_shift else None, gate_shift) * acc
    elif mode == "residual":
        acc = refs[0][...] + acc
    o_ref[...] = acc.astype(o_ref.dtype)


def _proj(x, w, b=None, extra=None, gate=None, *, w_t=False, col_off=0, n_cols=None, mode="plain", out_dtype,
          tm=1024, tn=512):
    M, K = x.shape
    N = w.shape[0 if w_t else 1] - col_off if n_cols is None else n_cols
    tm = _tile(M, tm)
    tn = min(tn, N)
    while col_off % tn:
        tn //= 2
    assert tn % LANES == 0
    if w_t:
        w_spec = pl.BlockSpec((tn, K), lambda j, i, o=col_off // tn: (j + o, 0))
    else:
        w_spec = pl.BlockSpec((K, tn), lambda j, i, o=col_off // tn: (0, j + o))
    in_specs = [pl.BlockSpec((tm, K), lambda j, i: (i, 0)), w_spec]
    args = [x, w]
    if b is not None:
        in_specs.append(pl.BlockSpec((1, tn), lambda j, i, o=col_off // tn: (0, j + o)))
        args.append(b.reshape(1, -1))
    if extra is not None:
        in_specs.append(pl.BlockSpec((tm, tn), lambda j, i: (i, j)))
        args.append(extra)
    gate_shift = 0
    if gate is not None:
        specs, gate_shift = _window_specs(gate[1], tm, tn, lambda j, i: i, lambda j, i: j)
        in_specs += specs
        args += [gate[0]] * len(specs)
    return pl.pallas_call(
        functools.partial(_proj_kernel, mode=mode, has_bias=b is not None, gate_shift=gate_shift,
                          w_rows_are_outputs=w_t),
        grid=(pl.cdiv(N, tn), M // tm),
        in_specs=in_specs,
        out_specs=pl.BlockSpec((tm, tn), lambda j, i: (i, j)),
        out_shape=jax.ShapeDtypeStruct((M, N), out_dtype),
        scratch_shapes=[pltpu.VMEM((K, tn), BF16)],
        compiler_params=_params("parallel", "arbitrary"),
    )(*args)


def _cumsum_kernel(lf_ref, c_ref):
    S = lf_ref.shape[1]
    row = lax.broadcasted_iota(jnp.int32, (S, S), 0)
    col = lax.broadcasted_iota(jnp.int32, (S, S), 1)
    upper = (row <= col).astype(BF16)
    c = jnp.zeros(lf_ref.shape, F32)
    for part in _split3(lf_ref[...]):
        c = c + jnp.dot(part, upper, preferred_element_type=F32)
    c_ref[...] = c


def _forget_cumsum(lf_t, B, S):
    H = lf_t.shape[0]
    return pl.pallas_call(
        _cumsum_kernel,
        grid=(B,),
        in_specs=[pl.BlockSpec((H, S), lambda b: (0, b))],
        out_specs=pl.BlockSpec((None, H, S), lambda b: (b, 0, 0)),
        out_shape=jax.ShapeDtypeStruct((B, H, S), F32),
        compiler_params=_params("parallel"),
    )(lf_t)


def _pool_kernel(u_ref, halo_ref, w_ref, sc_ref, *refs, seq_len, gate_shift):
    g0_ref, g0_tail_ref = (refs[0], refs[1]) if gate_shift else (refs[0], None)
    o_ref, ext_ref, pooled_ref = refs[-3:]
    g = pl.program_id(0)
    i = pl.program_id(1)
    tp = u_ref.shape[0]
    pos0 = (i * tp) % seq_len
    u = u_ref[...]
    ext_ref[pl.ds(POOL_HALO, tp), :] = u
    ext_ref[pl.ds(0, POOL_HALO), :] = jnp.where(pos0 == 0, 0.0, halo_ref[...])
    pos = pos0 + lax.broadcasted_iota(jnp.int32, (tp, 1), 0)
    for gi, win in enumerate(POOL_WINDOWS):
        @pl.when(g == gi)
        def _(win=win):
            acc = u
            for k in range(1, win):
                acc = acc + ext_ref[pl.ds(POOL_HALO - k, tp), :]
            cnt = jnp.minimum(pos + 1, win).astype(F32)
            pooled_ref[...] = (acc / cnt - u).astype(BF16)
    y = jnp.dot(pooled_ref[...], w_ref[...].astype(BF16), preferred_element_type=F32)
    o_ref[...] = _lane_window(g0_ref, g0_tail_ref, gate_shift) * (y * sc_ref[...])


def _pool_mixer(u, w_pool, scale, gates, gate0_col, seq_len, *, tp=512):
    T = u.shape[0]
    G, C, Do = w_pool.shape
    tp = _tile(seq_len, tp)
    assert tp % POOL_HALO == 0
    hb = tp // POOL_HALO
    gate_specs, gate_shift = _window_specs(gate0_col, tp, Do, lambda g, i: i, lambda g, i: g)
    return pl.pallas_call(
        functools.partial(_pool_kernel, seq_len=seq_len, gate_shift=gate_shift),
        grid=(G, T // tp),
        in_specs=[pl.BlockSpec((tp, C), lambda g, i: (i, g)),
                  pl.BlockSpec((POOL_HALO, C), lambda g, i: (jnp.maximum(i * hb - 1, 0), g)),
                  pl.BlockSpec((None, C, Do), lambda g, i: (g, 0, 0)),
                  pl.BlockSpec((1, Do), lambda g, i: (0, g))] + gate_specs,
        out_specs=pl.BlockSpec((tp, Do), lambda g, i: (i, g)),
        out_shape=jax.ShapeDtypeStruct((T, G * Do), F32),
        scratch_shapes=[pltpu.VMEM((tp + POOL_HALO, C), F32), pltpu.VMEM((tp, C), BF16)],
        compiler_params=_params("parallel", "parallel"),
    )(u, u, w_pool, scale.reshape(1, G * Do), *([gates] * len(gate_specs)))


def _fox_kernel(q_ref, k_ref, v_ref, cq_ref, ck_ref, o_ref, *, scale, dh):
    hg = pl.program_id(1)
    qi = pl.program_id(2)
    tq = q_ref.shape[0]
    n_heads = q_ref.shape[1] // dh
    lane = lax.broadcasted_iota(jnp.int32, cq_ref.shape, 1)
    cq_all = cq_ref[...]
    q = [q_ref[:, n * dh:(n + 1) * dh] for n in range(n_heads)]
    cq2 = [LOG2E * jnp.sum(jnp.where(lane == hg * n_heads + n, cq_all, 0.0), axis=1, keepdims=True)
           for n in range(n_heads)]

    def scores(n, j):
        rows = pl.ds(pl.multiple_of(j * tq, tq), tq)
        s = lax.dot_general(q[n], k_ref[rows, n * dh:(n + 1) * dh], (((1,), (1,)), ((), ())),
                            preferred_element_type=F32) * (scale * LOG2E)
        return s - LOG2E * ck_ref[n:n + 1, rows]

    def update(n, j, s, carry):
        m, l, acc = carry
        m_new = jnp.maximum(m, jnp.max(s, axis=1, keepdims=True) + cq2[n])
        alpha = jnp.exp2(m - m_new)
        p = jnp.exp2(s - (m_new - cq2[n]))
        v = v_ref[pl.ds(pl.multiple_of(j * tq, tq), tq), n * dh:(n + 1) * dh]
        acc = alpha * acc + jnp.dot(p.astype(BF16), v, preferred_element_type=F32)
        return m_new, alpha * l + jnp.sum(p, axis=1, keepdims=True), acc

    init = (jnp.full((tq, 1), NEG_BIG, F32), jnp.zeros((tq, 1), F32), jnp.zeros((tq, dh), F32))
    carries = lax.fori_loop(
        0, qi, lambda j, cs: tuple(update(n, j, scores(n, j), cs[n]) for n in range(n_heads)), (init,) * n_heads)
    row = lax.broadcasted_iota(jnp.int32, (tq, tq), 0)
    col = lax.broadcasted_iota(jnp.int32, (tq, tq), 1)
    for n in range(n_heads):
        s = jnp.where(col <= row, scores(n, qi), NEG_BIG)
        m, l, acc = update(n, qi, s, carries[n])
        o_ref[:, n * dh:(n + 1) * dh] = (acc / l).astype(o_ref.dtype)


def _fox_attention(qkv, c, c_t, B, S, H, dh, *, tq=512, heads_per_step=2):
    T = B * S
    tq = _tile(S, tq)
    nq = S // tq
    hp = heads_per_step if H % heads_per_step == 0 else 1
    ng = H // hp
    w = hp * dh
    return pl.pallas_call(
        functools.partial(_fox_kernel, scale=dh ** -0.5, dh=dh),
        grid=(B, ng, nq),
        in_specs=[pl.BlockSpec((tq, w), lambda b, h, i: (b * nq + i, h)),
                  pl.BlockSpec((S, w), lambda b, h, i: (b, ng + h)),
                  pl.BlockSpec((S, w), lambda b, h, i: (b, 2 * ng + h)),
                  pl.BlockSpec((None, tq, H), lambda b, h, i: (b, i, 0)),
                  pl.BlockSpec((None, None, hp, S), lambda b, h, i: (b, h, 0, 0))],
        out_specs=pl.BlockSpec((tq, w), lambda b, h, i: (b * nq + i, h)),
        out_shape=jax.ShapeDtypeStruct((T, H * dh), BF16),
        compiler_params=_params("parallel", "parallel", "parallel"),
    )(qkv, qkv, qkv, c, c_t.reshape(B, ng, hp, S))


def _mem_attn_kernel(q_ref, kv_ref, o_ref, *, heads):
    width = q_ref.shape[1]
    dh = width // heads
    scale = dh ** -0.5
    for hd in range(heads):
        q = q_ref[:, hd * dh:(hd + 1) * dh]
        k = kv_ref[:, hd * dh:(hd + 1) * dh]
        v = kv_ref[:, width + hd * dh:width + (hd + 1) * dh]
        s = lax.dot_general(q, k, (((1,), (1,)), ((), ())), preferred_element_type=F32) * scale
        p = jnp.exp(s - jnp.max(s, axis=1, keepdims=True))
        p = p / jnp.sum(p, axis=1, keepdims=True)
        o_ref[:, hd * dh:(hd + 1) * dh] = jnp.dot(p.astype(BF16), v, preferred_element_type=F32).astype(o_ref.dtype)


def _mem_attention(q, kv, B, S, n_mem, *, tq=512):
    T, width = q.shape
    tq = _tile(S, tq)
    nq = S // tq
    return pl.pallas_call(
        functools.partial(_mem_attn_kernel, heads=MEM_HEADS),
        grid=(B, nq),
        in_specs=[pl.BlockSpec((tq, width), lambda b, i: (b * nq + i, 0)),
                  pl.BlockSpec((n_mem, 2 * width), lambda b, i: (b, 0))],
        out_specs=pl.BlockSpec((tq, width), lambda b, i: (b * nq + i, 0)),
        out_shape=jax.ShapeDtypeStruct((T, width), BF16),
        compiler_params=_params("parallel", "parallel"),
    )(q, kv)


def _router_kernel(x_ref, g_ref, whi_ref, wlo_ref, b_ref, oi_ref, og_ref, cnt_ref, carry_ref, *, n_experts):
    i = pl.program_id(0)
    tm = x_ref.shape[0]

    @pl.when(i == 0)
    def _():
        carry_ref[...] = jnp.zeros_like(carry_ref)

    h = _rms(x_ref[...], g_ref[...])
    h_hi = h.astype(BF16)
    h_lo = (h - h_hi.astype(F32)).astype(BF16)
    logits = (jnp.dot(h_hi, whi_ref[...], preferred_element_type=F32)
              + jnp.dot(h_hi, wlo_ref[...], preferred_element_type=F32)
              + jnp.dot(h_lo, whi_ref[...], preferred_element_type=F32)) + b_ref[...]
    lane = lax.broadcasted_iota(jnp.int32, (tm, LANES), 1).astype(F32)
    work = jnp.where(lane < n_experts, logits, -jnp.inf)
    vals, idxs = [], []
    for _ in range(TOP_K):
        m = jnp.max(work, axis=1, keepdims=True)
        idx = jnp.min(jnp.where(work == m, lane, float(LANES)), axis=1, keepdims=True)
        vals.append(m)
        idxs.append(idx)
        work = jnp.where(lane == idx, -jnp.inf, work)
    exps = [jnp.exp(v - vals[0]) for v in vals]
    denom = exps[0] + exps[1] + exps[2] + exps[3]
    onehots = [(lane == idx).astype(F32) for idx in idxs]
    chosen = onehots[0] + onehots[1] + onehots[2] + onehots[3]
    row = lax.broadcasted_iota(jnp.int32, (tm, tm), 0)
    col = lax.broadcasted_iota(jnp.int32, (tm, tm), 1)
    before = jnp.dot((col < row).astype(BF16), chosen.astype(BF16), preferred_element_type=F32) + carry_ref[...]
    out_i = jnp.zeros((tm, LANES), F32)
    out_g = jnp.zeros((tm, LANES), F32)
    for k in range(TOP_K):
        rank = jnp.sum(onehots[k] * before, axis=1, keepdims=True)
        out_i = jnp.where(lane == k, idxs[k], out_i)
        out_i = jnp.where(lane == TOP_K + k, rank, out_i)
        out_g = jnp.where(lane == k, exps[k] / denom, out_g)
    oi_ref[...] = out_i.astype(jnp.int32)
    og_ref[...] = out_g
    carry_ref[...] = carry_ref[...] + jnp.sum(chosen, axis=0, keepdims=True)
    cnt_ref[...] = carry_ref[...]


def _router(x, g, w_router, b_router, *, tm=512):
    T, D = x.shape
    E = w_router.shape[1]
    tm = _tile(T, tm)
    w_pad = jnp.zeros((D, LANES), F32).at[:, :E].set(w_router)
    w_hi = w_pad.astype(BF16)
    w_lo = (w_pad - w_hi.astype(F32)).astype(BF16)
    b_pad = jnp.zeros((1, LANES), F32).at[0, :E].set(b_router)
    full = lambda i: (0, 0)
    return pl.pallas_call(
        functools.partial(_router_kernel, n_experts=E),
        grid=(T // tm,),
        in_specs=[pl.BlockSpec((tm, D), lambda i: (i, 0)), pl.BlockSpec((1, D), full),
                  pl.BlockSpec((D, LANES), full), pl.BlockSpec((D, LANES), full), pl.BlockSpec((1, LANES), full)],
        out_specs=[pl.BlockSpec((tm, LANES), lambda i: (i, 0)), pl.BlockSpec((tm, LANES), lambda i: (i, 0)),
                   pl.BlockSpec((1, LANES), full)],
        out_shape=[jax.ShapeDtypeStruct((T, LANES), jnp.int32), jax.ShapeDtypeStruct((T, LANES), F32),
                   jax.ShapeDtypeStruct((1, LANES), F32)],
        scratch_shapes=[pltpu.VMEM((1, LANES), F32)],
        compiler_params=_params("arbitrary"),
    )(x, g.reshape(1, D), w_hi, w_lo, b_pad)


def _row_copy(src_hbm, row, buf, slot, r, sem):
    return pltpu.make_async_copy(src_hbm.at[pl.ds(row, 1)], buf.at[slot, pl.ds(r, 1)], sem.at[slot])


def _start_rows(idx_ref, src_hbm, buf, slot, sem, n_rows):
    def body(r, _):
        _row_copy(src_hbm, idx_ref[0, 0, r], buf, slot, r, sem).start()
        return 0
    lax.fori_loop(0, n_rows, body, 0, unroll=8)


def _wait_rows(src_hbm, buf, slot, sem, n_rows):
    pltpu.make_async_copy(src_hbm.at[pl.ds(0, n_rows)], buf.at[slot], sem.at[slot]).wait()


def _gather_pipeline(idx_ref, nxt_ref, src_hbm, buf, sem, n_rows):
    i = pl.program_id(0)
    n = pl.num_programs(0)
    slot = i % 2

    @pl.when(i == 0)
    def _():
        _start_rows(idx_ref, src_hbm, buf, 0, sem, n_rows)

    @pl.when(i + 1 < n)
    def _():
        _start_rows(nxt_ref, src_hbm, buf, 1 - slot, sem, n_rows)

    _wait_rows(src_hbm, buf, slot, sem, n_rows)
    return slot


def _pack_bf16_pairs(h):
    half = h.shape[1] // 2
    bits = pltpu.bitcast(h.astype(BF16).astype(F32), jnp.uint32)
    return (bits[:, :half] >> 16) | bits[:, half:]


def _unpack_bf16_pairs(u):
    lo = pltpu.bitcast(u << 16, F32).astype(BF16)
    hi = pltpu.bitcast(u & jnp.uint32(0xFFFF0000), F32).astype(BF16)
    return lo, hi


def _scatter_norm_kernel(pe_ref, pd_ref, prev_ref, dest_ref, x_ref, g_ref, xs_hbm, buf0, buf1, buf2, zbuf, sem, zsem,
                         *, tm, n_experts, n_steps):
    i = pl.program_id(0)
    tb = x_ref.shape[0]
    bufs = (buf0, buf1, buf2)

    def issue(rows_ref, k):
        for t in range(tb):
            for j in range(TOP_K):
                pltpu.make_async_copy(bufs[k].at[pl.ds(t, 1)], xs_hbm.at[pl.ds(rows_ref[0, 0, t * TOP_K + j], 1)],
                                      sem.at[k]).start()

    def wait_slot(k):
        for _ in range(TOP_K):
            pltpu.make_async_copy(bufs[k], xs_hbm.at[pl.ds(0, tb)], sem.at[k]).wait()

    def finish(k):
        issue(dest_ref, k)
        if n_steps >= 3:
            wait_slot((k + 1) % 3)
        if n_steps >= 2:
            wait_slot((k + 2) % 3)
        wait_slot(k)

    def zero_group(first_row):
        return pltpu.make_async_copy(zbuf, xs_hbm.at[pl.ds(pl.multiple_of(first_row, tm), tm)], zsem)

    @pl.when(i == 0)
    def _():
        zbuf[...] = jnp.zeros_like(zbuf)
        n_groups = xs_hbm.shape[0] // tm
        used = pe_ref[n_experts - 1] // tm
        for e in range(n_experts):
            @pl.when(pd_ref[e] > 0)
            def _(e=e):
                zero_group(pe_ref[e] - tm).start()
        lax.fori_loop(used, n_groups, lambda gi, c: (zero_group(gi * tm).start(), c)[1], 0)
        for e in range(n_experts):
            @pl.when(pd_ref[e] > 0)
            def _(e=e):
                zero_group(pe_ref[e] - tm).wait()
        lax.fori_loop(used, n_groups, lambda gi, c: (zero_group(gi * tm).wait(), c)[1], 0)

    @pl.when(i == 0)
    def _():
        buf0[...] = _pack_bf16_pairs(_rms(x_ref[...], g_ref[...]))
        if n_steps == 1:
            finish(0)

    for k in range(3):
        @pl.when(jnp.logical_and(i % 3 == k, i > 0))
        def _(k=k):
            @pl.when(i >= 3)
            def _():
                wait_slot(k)

            bufs[k][...] = _pack_bf16_pairs(_rms(x_ref[...], g_ref[...]))
            issue(prev_ref, (k + 2) % 3)

            @pl.when(i == n_steps - 1)
            def _():
                finish(k)


def _scatter_norm(x, g, dest, pad_ends, padded, n_rows, *, tm, tb=128):
    T, D = x.shape
    E = pad_ends.shape[0]
    tb = _tile(T, tb)
    nb = T // tb
    grid_spec = pltpu.PrefetchScalarGridSpec(
        num_scalar_prefetch=2, grid=(nb,),
        in_specs=[pl.BlockSpec((1, 1, TOP_K * tb), lambda i, pe, pd: (jnp.maximum(i - 1, 0), 0, 0),
                               memory_space=pltpu.SMEM),
                  pl.BlockSpec((1, 1, TOP_K * tb), lambda i, pe, pd: (i, 0, 0), memory_space=pltpu.SMEM),
                  pl.BlockSpec((tb, D), lambda i, pe, pd: (i, 0)),
                  pl.BlockSpec((1, D), lambda i, pe, pd: (0, 0))],
        out_specs=pl.BlockSpec(memory_space=pl.ANY),
        scratch_shapes=[pltpu.VMEM((tb, D // 2), jnp.uint32)] * 3 + [pltpu.VMEM((tm, D // 2), jnp.uint32),
                        pltpu.SemaphoreType.DMA((3,)), pltpu.SemaphoreType.DMA(())])
    dest3 = dest.reshape(nb, 1, TOP_K * tb)
    return pl.pallas_call(
        functools.partial(_scatter_norm_kernel, tm=tm, n_experts=E, n_steps=nb), grid_spec=grid_spec,
        out_shape=jax.ShapeDtypeStruct((n_rows, D // 2), jnp.uint32),
        compiler_params=_params("arbitrary"),
    )(pad_ends, padded, dest3, dest3, x, g.reshape(1, D))


def _combine_kernel(idx0_ref, idx1_ref, idx2_ref, x_ref, gate_ref, g_ref, rows_hbm, o_ref, buf0, buf1, buf2, sem, *,
                    final_norm):
    i = pl.program_id(0)
    n = pl.num_programs(0)
    tb = x_ref.shape[0]
    n_rows = TOP_K * tb
    bufs = (buf0, buf1, buf2)

    def issue(idx_ref, k):
        for r in range(n_rows):
            pltpu.make_async_copy(rows_hbm.at[pl.ds(idx_ref[0, 0, r], 1)], bufs[k].at[pl.ds(r, 1)], sem.at[k]).start()

    def wait(k):
        pltpu.make_async_copy(rows_hbm.at[pl.ds(0, n_rows)], bufs[k], sem.at[k]).wait()

    @pl.when(i == 0)
    def _():
        issue(idx0_ref, 0)
        issue(idx1_ref, 1)

    for k in range(3):
        @pl.when(i % 3 == k)
        def _(k=k):
            wait(k)
            issue(idx2_ref, (k + 2) % 3)
            y = x_ref[...]
            gates = gate_ref[...]
            for j in range(TOP_K):
                y = y + gates[:, j:j + 1] * bufs[k][pl.ds(j * tb, tb), :]
            o_ref[...] = _rms(y, g_ref[...]) if final_norm else y

            @pl.when(i == n - 1)
            def _():
                wait((k + 1) % 3)
                wait((k + 2) % 3)


def _combine(x, rows, dest, gates_slab, g, final_norm, *, tb=64):
    T, D = x.shape
    tb = _tile(T, tb)
    nb = T // tb
    idx3 = dest.reshape(nb, tb, TOP_K).transpose(0, 2, 1).reshape(nb, 1, TOP_K * tb)
    return pl.pallas_call(
        functools.partial(_combine_kernel, final_norm=final_norm),
        grid=(nb,),
        in_specs=[pl.BlockSpec((1, 1, TOP_K * tb), lambda i, a=ahead: (jnp.minimum(i + a, nb - 1), 0, 0),
                               memory_space=pltpu.SMEM) for ahead in range(3)] + [
                  pl.BlockSpec((tb, D), lambda i: (i, 0)),
                  pl.BlockSpec((tb, LANES), lambda i: (i, 0)),
                  pl.BlockSpec((1, D), lambda i: (0, 0)),
                  pl.BlockSpec(memory_space=pl.ANY)],
        out_specs=pl.BlockSpec((tb, D), lambda i: (i, 0)),
        out_shape=jax.ShapeDtypeStruct((T, D), F32),
        scratch_shapes=[pltpu.VMEM((TOP_K * tb, D), F32)] * 3 + [pltpu.SemaphoreType.DMA((3,))],
        compiler_params=_params("arbitrary"),
    )(idx3, idx3, idx3, x, gates_slab, g.reshape(1, D), rows)


MOE_GROUP_ROWS = 256
MOE_GROUPS_PER_STEP = 5
MOE_BLOCK_SIZES = (5, 4, 1)
MOE_BLOCK_GROUPS = max(MOE_BLOCK_SIZES)
MOE_UP_COLS = 512
MOE_DOWN_COLS = 512


def _swiglu_tile(lo, hi, wb_ref, b):
    half = lo.shape[1]
    pair = 2 * LANES
    row = lax.broadcasted_iota(jnp.int32, (pair, LANES), 0)
    col = lax.broadcasted_iota(jnp.int32, (pair, LANES), 1)
    pick_even = (row == 2 * col).astype(BF16)
    even = lax.broadcasted_iota(jnp.int32, (1, pair), 1) % 2 == 0
    gu = (jnp.dot(lo, wb_ref[:half, :], preferred_element_type=F32)
          + jnp.dot(hi, wb_ref[half:, :], preferred_element_type=F32)) + b
    outs = []
    for c in range(gu.shape[1] // pair):
        blk = gu[:, c * pair:(c + 1) * pair]
        nxt = pltpu.roll(blk, pair - 1, axis=1)
        gate = jnp.minimum(blk, SWIGLU_LIMIT)
        up = jnp.clip(nxt, -SWIGLU_LIMIT, SWIGLU_LIMIT)
        act = (up + 1.0) * gate * jax.nn.sigmoid(SWIGLU_ALPHA * gate)
        act = jnp.where(even, act, 0.0).astype(BF16)
        outs.append(jnp.dot(act, pick_even, preferred_element_type=F32).astype(BF16))
    return outs[0] if len(outs) == 1 else jnp.concatenate(outs, axis=1)


def _experts_kernel(se_ref, sr_ref, sn_ref, cnt_ref, bgu_ref, bd_ref, xs_hbm, wgu_hbm, wd_hbm, rows_hbm,
                    xs_v, act_v, wu_f, wu_b, wd_f, wd_b, out_v, xs_sem, wu_sem, wd_sem, out_sem, *, tm):
    s = pl.program_id(0)
    n_active = cnt_ref[0]
    tnu, tnd = wu_b.shape[1], wd_b.shape[1]
    n_up, n_down = wgu_hbm.shape[2] // tnu, wd_hbm.shape[2] // tnd

    def wu_copy(step, c, slot):
        cols = pl.ds(pl.multiple_of(c * tnu, tnu), tnu)
        return pltpu.make_async_copy(wgu_hbm.at[se_ref[step], :, cols], wu_f.at[slot], wu_sem.at[slot])

    def wd_copy(step, c, slot):
        cols = pl.ds(pl.multiple_of(c * tnd, tnd), tnd)
        return pltpu.make_async_copy(wd_hbm.at[se_ref[step], :, cols], wd_f.at[slot], wd_sem.at[slot])

    def xs_copy(step, g):
        return pltpu.make_async_copy(xs_hbm.at[pl.ds(pl.multiple_of(sr_ref[step] + g * tm, tm), tm)],
                                     xs_v.at[pl.ds(pl.multiple_of(g * tm, tm), tm)], xs_sem)

    def out_copy(slot, row, c):
        cols = pl.ds(pl.multiple_of(c * tnd, tnd), tnd)
        return pltpu.make_async_copy(
            out_v.at[slot], rows_hbm.at[pl.ds(pl.multiple_of(row, tm), tm), cols], out_sem.at[slot])

    def start_rows(step):
        lax.fori_loop(0, sn_ref[step], lambda g, carry: (xs_copy(step, g).start(), carry)[1], 0)

    def for_row_blocks(n_groups, fn, carry):
        done = 0
        for size in MOE_BLOCK_SIZES:
            n_blocks = (n_groups - done) // size
            carry = lax.fori_loop(0, n_blocks, lambda b, cr, d=done, sz=size: fn(d + b * sz, sz, cr), carry)
            done = done + n_blocks * size
        return carry

    @pl.when(s < n_active)
    def _():
        ng = sn_ref[s]
        row0 = sr_ref[s]

        @pl.when(s == 0)
        def _():
            start_rows(0)
            wu_copy(0, 0, 0).start()

        lax.fori_loop(0, ng, lambda g, carry: (xs_copy(s, g).wait(), carry)[1], 0)

        def up_chunk(c, carry):
            slot = c % 2
            wu_copy(s, c, slot).wait()

            @pl.when(c + 1 < n_up)
            def _():
                wu_copy(s, c + 1, 1 - slot).start()

            @pl.when(c + 1 == n_up)
            def _():
                wd_copy(s, 0, 0).start()

            wu_b[...] = wu_f[slot].astype(BF16)
            bias = bgu_ref[c]

            def up_rows(g0, n_g, carry):
                rows = pl.ds(pl.multiple_of(g0 * tm, tm), n_g * tm)
                lo, hi = _unpack_bf16_pairs(xs_v[rows, :])
                tile = _swiglu_tile(lo, hi, wu_b, bias)
                for cc in range(n_up):
                    @pl.when(c == cc)
                    def _(cc=cc):
                        act_v[rows, cc * (tnu // 2):(cc + 1) * (tnu // 2)] = tile
                return carry
            return for_row_blocks(ng, up_rows, carry)
        lax.fori_loop(0, n_up, up_chunk, 0)

        @pl.when(s + 1 < n_active)
        def _():
            start_rows(s + 1)

        def drain(pending):
            for k in range(MOE_BLOCK_GROUPS):
                @pl.when(pending[k] == 1)
                def _(k=k):
                    out_copy(k, 0, 0).wait()

        def down_chunk(c, pending):
            slot = c % 2
            wd_copy(s, c, slot).wait()

            @pl.when(c + 1 < n_down)
            def _():
                wd_copy(s, c + 1, 1 - slot).start()

            @pl.when(jnp.logical_and(c + 1 == n_down, s + 1 < n_active))
            def _():
                wu_copy(s + 1, 0, 0).start()

            wd_b[...] = wd_f[slot].astype(BF16)
            bias = bd_ref[c]

            def down_rows(g0, n_g, pending):
                rows = pl.ds(pl.multiple_of(g0 * tm, tm), n_g * tm)
                o = jnp.dot(act_v[rows, :], wd_b[...], preferred_element_type=F32) + bias
                drain(pending)
                for k in range(n_g):
                    out_v[k] = o[k * tm:(k + 1) * tm]
                    out_copy(k, row0 + (g0 + k) * tm, c).start()
                return tuple(jnp.int32(1 if k < n_g else 0) for k in range(MOE_BLOCK_GROUPS))
            return for_row_blocks(ng, down_rows, pending)
        drain(lax.fori_loop(0, n_down, down_chunk, (jnp.int32(0),) * MOE_BLOCK_GROUPS))

    @pl.when(s == pl.num_programs(0) - 1)
    def _():
        out_v[0] = jnp.zeros(out_v.shape[1:], out_v.dtype)
        first, last = cnt_ref[1], rows_hbm.shape[0] // tm

        def fill(gi, carry, wait):
            for c in range(n_down):
                cp = out_copy(0, gi * tm, c)
                cp.wait() if wait else cp.start()
            return carry
        lax.fori_loop(first, last, functools.partial(fill, wait=False), 0)
        lax.fori_loop(first, last, functools.partial(fill, wait=True), 0)


def _experts(xs, w_gu, b_gu, w_d, b_d, step_expert, step_row, step_groups, counts, *, tm):
    R, half = xs.shape
    E, D, F2 = w_gu.shape
    F = F2 // 2
    tnu, tnd = min(MOE_UP_COLS, F2), min(MOE_DOWN_COLS, D)
    assert D == 2 * half and F2 % tnu == 0 and D % tnd == 0 and tnu % (2 * LANES) == 0 and D // tnd >= 2
    assert MOE_BLOCK_SIZES[-1] == 1 and MOE_BLOCK_GROUPS <= MOE_GROUPS_PER_STEP
    cap = MOE_GROUPS_PER_STEP * tm
    n_steps = step_expert.shape[0]
    expert_of = lambda s, se, sr, sn, cnt: (se[jnp.minimum(s, cnt[0] - 1)], 0, 0, 0)
    grid_spec = pltpu.PrefetchScalarGridSpec(
        num_scalar_prefetch=4, grid=(n_steps,),
        in_specs=[pl.BlockSpec((None, F2 // tnu, 1, tnu), expert_of),
                  pl.BlockSpec((None, D // tnd, 1, tnd), expert_of),
                  pl.BlockSpec(memory_space=pl.ANY), pl.BlockSpec(memory_space=pl.ANY),
                  pl.BlockSpec(memory_space=pl.ANY)],
        out_specs=pl.BlockSpec(memory_space=pl.ANY),
        scratch_shapes=[pltpu.VMEM((cap, half), jnp.uint32), pltpu.VMEM((cap, F), BF16),
                        pltpu.VMEM((2, D, tnu), F32), pltpu.VMEM((D, tnu), BF16),
                        pltpu.VMEM((2, F, tnd), F32), pltpu.VMEM((F, tnd), BF16),
                        pltpu.VMEM((MOE_BLOCK_GROUPS, tm, tnd), F32),
                        pltpu.SemaphoreType.DMA(()), pltpu.SemaphoreType.DMA((2,)),
                        pltpu.SemaphoreType.DMA((2,)), pltpu.SemaphoreType.DMA((MOE_BLOCK_GROUPS,))])
    return pl.pallas_call(
        functools.partial(_experts_kernel, tm=tm), grid_spec=grid_spec,
        out_shape=jax.ShapeDtypeStruct((R, D), F32),
        compiler_params=_params("arbitrary"),
    )(step_expert, step_row, step_groups, counts, b_gu.reshape(E, F2 // tnu, 1, tnu),
      b_d.reshape(E, D // tnd, 1, tnd), xs, w_gu, w_d)


def _moe(x, g_moe, w_router, b_router, w_gate_up, b_gate_up, w_down, b_down, g_final, final_norm):
    T, D = x.shape
    E = w_router.shape[1]
    A = T * TOP_K
    tm = min(MOE_GROUP_ROWS, A)
    slab_i, slab_g, cnt = _router(x, g_moe, w_router, b_router)
    idx = slab_i[:, :TOP_K]
    rank = slab_i[:, TOP_K:2 * TOP_K]
    counts = cnt[0, :E].astype(jnp.int32)
    groups = (counts + tm - 1) // tm
    pad_ends = jnp.cumsum(groups) * tm
    pad_starts = pad_ends - groups * tm
    dest = pad_starts[idx] + rank
    n_groups = (A + tm - 1) // tm + E
    R = n_groups * tm
    per = MOE_GROUPS_PER_STEP
    runs = (groups + per - 1) // per
    run_ends = jnp.cumsum(runs)
    n_steps = (n_groups + per - 1) // per + E
    step = jnp.arange(n_steps, dtype=jnp.int32)
    step_expert = jnp.minimum(jnp.sum(step[:, None] >= run_ends[None, :], axis=1), E - 1).astype(jnp.int32)
    local = step - (run_ends - runs)[step_expert]
    step_row = (pad_starts[step_expert] + local * per * tm).astype(jnp.int32)
    step_groups = jnp.clip(groups[step_expert] - local * per, 0, per).astype(jnp.int32)
    step_counts = jnp.stack([run_ends[-1], pad_ends[-1] // tm]).astype(jnp.int32)

    xs = _scatter_norm(x, g_moe, dest, pad_ends, groups * tm, R, tm=tm)
    rows = _experts(xs, w_gate_up, b_gate_up, w_down, b_down, step_expert, step_row, step_groups, step_counts, tm=tm)
    return _combine(x, rows, dest, slab_g, g_final, final_norm)


def kernel(x, mem, g_mix, w_in, b_in, w_pool, pool_scale, w_fox_o, w_out, g_mem_q, g_mem_kv, w_mem_q, w_mem_kv,
           w_mem_o, g_moe, w_router, b_router, w_gate_up, b_gate_up, w_down, b_down, g_final):
    B, S, D = x.shape
    T = B * S
    n_mem = mem.shape[1]
    depth, G, C, Do = w_pool.shape
    pool_w = G * C
    fox_w = w_fox_o.shape[1]
    H = w_in.shape[2] - pool_w - 3 * fox_w - 2 * D
    dh = fox_w // H
    off_q, off_f = pool_w, pool_w + 3 * fox_w
    off_gate = off_f + H
    xt = x.reshape(T, D)
    mt = mem.reshape(B * n_mem, D)
    for l in range(depth):
        wl, bl = jnp.swapaxes(w_in[l], 0, 1), b_in[l]
        h, lf_t = _norm(xt, g_mix[l], wl[off_f:off_gate].astype(BF16), bl[off_f:off_gate])
        u = _proj(h, wl, bl, w_t=True, col_off=0, n_cols=off_q, out_dtype=F32)
        qkv = _proj(h, wl, bl, w_t=True, col_off=off_q, n_cols=off_f - off_q, out_dtype=BF16)
        gate_base = off_gate // LANES * LANES
        gate0 = off_gate - gate_base
        gates = _proj(h, wl, bl, w_t=True, col_off=gate_base, mode="sigmoid", out_dtype=BF16)
        c_t = _forget_cumsum(lf_t, B, S)
        att = _fox_attention(qkv, c_t.transpose(0, 2, 1), c_t, B, S, H, dh)
        pp = _pool_mixer(u, w_pool[l], pool_scale[l], gates, gate0, S)
        merged = _proj(att, w_fox_o[l], None, pp, (gates, gate0 + D), mode="merge", out_dtype=BF16)
        xt = _proj(merged, w_out[l], None, xt, mode="residual", out_dtype=F32)
        kv = _proj(_norm(mt, g_mem_kv[l]), w_mem_kv[l], out_dtype=BF16)
        qm = _proj(_norm(xt, g_mem_q[l]), w_mem_q[l], out_dtype=BF16)
        om = _mem_attention(qm, kv, B, S, n_mem)
        xt = _proj(om, w_mem_o[l], None, xt, mode="residual", out_dtype=F32)
        xt = _moe(xt, g_moe[l], w_router[l], b_router[l], w_gate_up[l], b_gate_up[l], w_down[l], b_down[l],
                  g_final, final_norm=l == depth - 1)
    return xt.reshape(B, S, D)
```

```python
import functools

import jax
import jax.numpy as jnp
from jax import lax
from jax.experimental import pallas as pl
from jax.experimental.pallas import tpu as pltpu

F32 = jnp.float32
BF16 = jnp.bfloat16

EPS = 1e-5
POOL_WINDOWS = (2, 4, 8, 16)
POOL_HALO = 16
MEM_HEADS = 4
TOP_K = 4
SWIGLU_LIMIT = 7.0
SWIGLU_ALPHA = 1.702
NEG_BIG = -1e30
LOG2E = 1.4426950408889634

LANES = 128
VMEM_LIMIT_BYTES = 56 * 1024 * 1024


def _tile(dim, pref):
    t = pref
    while t >= 8:
        if dim % t == 0:
            return t
        t //= 2
    return dim


def _params(*sem):
    return pltpu.CompilerParams(dimension_semantics=sem, vmem_limit_bytes=VMEM_LIMIT_BYTES)


def _rms(x, g):
    ms = jnp.mean(x * x, axis=-1, keepdims=True)
    return x * lax.rsqrt(ms + EPS) * g


def _split3(x):
    hi = x.astype(BF16)
    r1 = x - hi.astype(F32)
    mid = r1.astype(BF16)
    lo = (r1 - mid.astype(F32)).astype(BF16)
    return hi, mid, lo


def _norm_kernel(x_ref, g_ref, *refs, with_f):
    h = _rms(x_ref[...], g_ref[...]).astype(BF16)
    if with_f:
        wf_ref, bf_ref, h_ref, lf_ref = refs
        f = lax.dot_general(wf_ref[...], h, (((1,), (1,)), ((), ())), preferred_element_type=F32) + bf_ref[...]
        lf_ref[...] = jnp.minimum(f, 0.0) - jnp.log1p(jnp.exp(-jnp.abs(f)))
    else:
        h_ref, = refs
    h_ref[...] = h


def _norm(x, g, wf_t=None, bf_t=None, *, tm=512):
    M, K = x.shape
    tm = _tile(M, tm)
    with_f = wf_t is not None
    in_specs = [pl.BlockSpec((tm, K), lambda i: (i, 0)), pl.BlockSpec((1, K), lambda i: (0, 0))]
    args = [x, g.reshape(1, K)]
    out_shape = [jax.ShapeDtypeStruct((M, K), BF16)]
    out_specs = [pl.BlockSpec((tm, K), lambda i: (i, 0))]
    if with_f:
        H = wf_t.shape[0]
        in_specs += [pl.BlockSpec((H, K), lambda i: (0, 0)), pl.BlockSpec((H, 1), lambda i: (0, 0))]
        args += [wf_t, bf_t.reshape(H, 1)]
        out_shape.append(jax.ShapeDtypeStruct((H, M), F32))
        out_specs.append(pl.BlockSpec((H, tm), lambda i: (0, i)))
    res = pl.pallas_call(
        functools.partial(_norm_kernel, with_f=with_f),
        grid=(M // tm,),
        in_specs=in_specs, out_specs=out_specs, out_shape=out_shape,
        compiler_params=_params("parallel"),
    )(*args)
    return res if with_f else res[0]


def _lane_window(main_ref, tail_ref, shift):
    if shift == 0:
        return main_ref[...].astype(F32)
    g = jnp.concatenate([main_ref[...], tail_ref[...]], axis=1).astype(F32)
    return pltpu.roll(g, g.shape[1] - shift, axis=1)[:, :main_ref.shape[1]]


def _window_specs(first_col, rows, width, row_of, col_block_of):
    base, shift = first_col // LANES * LANES, first_col % LANES
    assert base % width == 0
    specs = [pl.BlockSpec((rows, width), lambda *g: (row_of(*g), col_block_of(*g) + base // width))]
    if shift:
        specs.append(pl.BlockSpec(
            (rows, LANES), lambda *g: (row_of(*g), (base + (col_block_of(*g) + 1) * width) // LANES)))
    return specs, shift


def _proj_kernel(x_ref, w_ref, *refs, mode, has_bias, gate_shift, w_rows_are_outputs):
    refs = list(refs)
    wb_ref = refs.pop()
    o_ref = refs.pop()
    b_ref = refs.pop(0) if has_bias else None

    @pl.when(pl.program_id(1) == 0)
    def _():
        w = w_ref[...]
        wb_ref[...] = (w.T if w_rows_are_outputs else w).astype(BF16)

    acc = jnp.dot(x_ref[...], wb_ref[...], preferred_element_type=F32)
    if has_bias:
        acc = acc + b_ref[...]
    if mode == "sigmoid":
        acc = jax.nn.sigmoid(acc)
    elif mode == "merge":
        pp_ref, g_ref = refs[0], refs[1]
        acc = pp_ref[...] + _lane_window(g_ref, refs[2] if gate_shift else None, gate_shift) * acc
    elif mode == "residual":
        acc = refs[0][...] + acc
    o_ref[...] = acc.astype(o_ref.dtype)


def _proj(x, w, b=None, extra=None, gate=None, *, w_t=False, col_off=0, n_cols=None, mode="plain", out_dtype,
          tm=1024, tn=512):
    M, K = x.shape
    N = w.shape[0 if w_t else 1] - col_off if n_cols is None else n_cols
    tm = _tile(M, tm)
    tn = min(tn, N)
    while col_off % tn:
        tn //= 2
    assert tn % LANES == 0
    if w_t:
        w_spec = pl.BlockSpec((tn, K), lambda j, i, o=col_off // tn: (j + o, 0))
    else:
        w_spec = pl.BlockSpec((K, tn), lambda j, i, o=col_off // tn: (0, j + o))
    in_specs = [pl.BlockSpec((tm, K), lambda j, i: (i, 0)), w_spec]
    args = [x, w]
    if b is not None:
        in_specs.append(pl.BlockSpec((1, tn), lambda j, i, o=col_off // tn: (0, j + o)))
        args.append(b.reshape(1, -1))
    if extra is not None:
        in_specs.append(pl.BlockSpec((tm, tn), lambda j, i: (i, j)))
        args.append(extra)
    gate_shift = 0
    if gate is not None:
        specs, gate_shift = _window_specs(gate[1], tm, tn, lambda j, i: i, lambda j, i: j)
        in_specs += specs
        args += [gate[0]] * len(specs)
    return pl.pallas_call(
        functools.partial(_proj_kernel, mode=mode, has_bias=b is not None, gate_shift=gate_shift,
                          w_rows_are_outputs=w_t),
        grid=(pl.cdiv(N, tn), M // tm),
        in_specs=in_specs,
        out_specs=pl.BlockSpec((tm, tn), lambda j, i: (i, j)),
        out_shape=jax.ShapeDtypeStruct((M, N), out_dtype),
        scratch_shapes=[pltpu.VMEM((K, tn), BF16)],
        compiler_params=_params("parallel", "arbitrary"),
    )(*args)


def _cumsum_kernel(lf_ref, c_ref):
    S = lf_ref.shape[1]
    row = lax.broadcasted_iota(jnp.int32, (S, S), 0)
    col = lax.broadcasted_iota(jnp.int32, (S, S), 1)
    upper = (row <= col).astype(BF16)
    c = jnp.zeros(lf_ref.shape, F32)
    for part in _split3(lf_ref[...]):
        c = c + jnp.dot(part, upper, preferred_element_type=F32)
    c_ref[...] = c


def _forget_cumsum(lf_t, B, S):
    H = lf_t.shape[0]
    return pl.pallas_call(
        _cumsum_kernel,
        grid=(B,),
        in_specs=[pl.BlockSpec((H, S), lambda b: (0, b))],
        out_specs=pl.BlockSpec((None, H, S), lambda b: (b, 0, 0)),
        out_shape=jax.ShapeDtypeStruct((B, H, S), F32),
        compiler_params=_params("parallel"),
    )(lf_t)


def _pool_kernel(u_ref, halo_ref, w_ref, sc_ref, *refs, seq_len, gate_shift):
    g0_ref, g0_tail_ref = (refs[0], refs[1]) if gate_shift else (refs[0], None)
    o_ref, ext_ref, pooled_ref = refs[-3:]
    g = pl.program_id(0)
    i = pl.program_id(1)
    tp = u_ref.shape[0]
    pos0 = (i * tp) % seq_len
    u = u_ref[...]
    ext_ref[pl.ds(POOL_HALO, tp), :] = u
    ext_ref[pl.ds(0, POOL_HALO), :] = jnp.where(pos0 == 0, 0.0, halo_ref[...])
    pos = pos0 + lax.broadcasted_iota(jnp.int32, (tp, 1), 0)
    for gi, win in enumerate(POOL_WINDOWS):
        @pl.when(g == gi)
        def _(win=win):
            acc = u
            for k in range(1, win):
                acc = acc + ext_ref[pl.ds(POOL_HALO - k, tp), :]
            cnt = jnp.minimum(pos + 1, win).astype(F32)
            pooled_ref[...] = (acc / cnt - u).astype(BF16)
    y = jnp.dot(pooled_ref[...], w_ref[...].astype(BF16), preferred_element_type=F32)
    o_ref[...] = _lane_window(g0_ref, g0_tail_ref, gate_shift) * (y * sc_ref[...])


def _pool_mixer(u, w_pool, scale, gates, gate0_col, seq_len, *, tp=512):
    T = u.shape[0]
    G, C, Do = w_pool.shape
    tp = _tile(seq_len, tp)
    assert tp % POOL_HALO == 0
    hb = tp // POOL_HALO
    gate_specs, gate_shift = _window_specs(gate0_col, tp, Do, lambda g, i: i, lambda g, i: g)
    return pl.pallas_call(
        functools.partial(_pool_kernel, seq_len=seq_len, gate_shift=gate_shift),
        grid=(G, T // tp),
        in_specs=[pl.BlockSpec((tp, C), lambda g, i: (i, g)),
                  pl.BlockSpec((POOL_HALO, C), lambda g, i: (jnp.maximum(i * hb - 1, 0), g)),
                  pl.BlockSpec((None, C, Do), lambda g, i: (g, 0, 0)),
                  pl.BlockSpec((1, Do), lambda g, i: (0, g))] + gate_specs,
        out_specs=pl.BlockSpec((tp, Do), lambda g, i: (i, g)),
        out_shape=jax.ShapeDtypeStruct((T, G * Do), F32),
        scratch_shapes=[pltpu.VMEM((tp + POOL_HALO, C), F32), pltpu.VMEM((tp, C), BF16)],
        compiler_params=_params("parallel", "parallel"),
    )(u, u, w_pool, scale.reshape(1, G * Do), *([gates] * len(gate_specs)))


def _fox_kernel(q_ref, k_ref, v_ref, cq_ref, ck_ref, o_ref, *, scale, dh):
    hg = pl.program_id(1)
    qi = pl.program_id(2)
    tq = q_ref.shape[0]
    n_heads = q_ref.shape[1] // dh
    lane = lax.broadcasted_iota(jnp.int32, cq_ref.shape, 1)
    cq_all = cq_ref[...]
    q = [q_ref[:, n * dh:(n + 1) * dh] for n in range(n_heads)]
    cq2 = [LOG2E * jnp.sum(jnp.where(lane == hg * n_heads + n, cq_all, 0.0), axis=1, keepdims=True)
           for n in range(n_heads)]

    def scores(n, j):
        rows = pl.ds(pl.multiple_of(j * tq, tq), tq)
        s = lax.dot_general(q[n], k_ref[rows, n * dh:(n + 1) * dh], (((1,), (1,)), ((), ())),
                            preferred_element_type=F32) * (scale * LOG2E)
        return s - LOG2E * ck_ref[n:n + 1, rows]

    def update(n, j, s, carry):
        m, l, acc = carry
        m_new = jnp.maximum(m, jnp.max(s, axis=1, keepdims=True) + cq2[n])
        alpha = jnp.exp2(m - m_new)
        p = jnp.exp2(s - (m_new - cq2[n]))
        v = v_ref[pl.ds(pl.multiple_of(j * tq, tq), tq), n * dh:(n + 1) * dh]
        acc = alpha * acc + jnp.dot(p.astype(BF16), v, preferred_element_type=F32)
        return m_new, alpha * l + jnp.sum(p, axis=1, keepdims=True), acc

    init = (jnp.full((tq, 1), NEG_BIG, F32), jnp.zeros((tq, 1), F32), jnp.zeros((tq, dh), F32))
    carries = lax.fori_loop(
        0, qi, lambda j, cs: tuple(update(n, j, scores(n, j), cs[n]) for n in range(n_heads)), (init,) * n_heads)
    row = lax.broadcasted_iota(jnp.int32, (tq, tq), 0)
    col = lax.broadcasted_iota(jnp.int32, (tq, tq), 1)
    for n in range(n_heads):
        s = jnp.where(col <= row, scores(n, qi), NEG_BIG)
        m, l, acc = update(n, qi, s, carries[n])
        o_ref[:, n * dh:(n + 1) * dh] = (acc / l).astype(o_ref.dtype)


def _fox_attention(qkv, c, c_t, B, S, H, dh, *, tq=512, heads_per_step=2):
    T = B * S
    tq = _tile(S, tq)
    nq = S // tq
    hp = heads_per_step if H % heads_per_step == 0 else 1
    ng = H // hp
    w = hp * dh
    return pl.pallas_call(
        functools.partial(_fox_kernel, scale=dh ** -0.5, dh=dh),
        grid=(B, ng, nq),
        in_specs=[pl.BlockSpec((tq, w), lambda b, h, i: (b * nq + i, h)),
                  pl.BlockSpec((S, w), lambda b, h, i: (b, ng + h)),
                  pl.BlockSpec((S, w), lambda b, h, i: (b, 2 * ng + h)),
                  pl.BlockSpec((None, tq, H), lambda b, h, i: (b, i, 0)),
                  pl.BlockSpec((None, None, hp, S), lambda b, h, i: (b, h, 0, 0))],
        out_specs=pl.BlockSpec((tq, w), lambda b, h, i: (b * nq + i, h)),
        out_shape=jax.ShapeDtypeStruct((T, H * dh), BF16),
        compiler_params=_params("parallel", "parallel", "parallel"),
    )(qkv, qkv, qkv, c, c_t.reshape(B, ng, hp, S))


def _mem_attn_kernel(q_ref, kv_ref, o_ref, *, heads):
    width = q_ref.shape[1]
    dh = width // heads
    scale = dh ** -0.5
    for hd in range(heads):
        q = q_ref[:, hd * dh:(hd + 1) * dh]
        k = kv_ref[:, hd * dh:(hd + 1) * dh]
        v = kv_ref[:, width + hd * dh:width + (hd + 1) * dh]
        s = lax.dot_general(q, k, (((1,), (1,)), ((), ())), preferred_element_type=F32) * scale
        p = jnp.exp(s - jnp.max(s, axis=1, keepdims=True))
        p = p / jnp.sum(p, axis=1, keepdims=True)
        o_ref[:, hd * dh:(hd + 1) * dh] = jnp.dot(p.astype(BF16), v, preferred_element_type=F32).astype(o_ref.dtype)


def _mem_attention(q, kv, B, S, n_mem, *, tq=512):
    T, width = q.shape
    tq = _tile(S, tq)
    nq = S // tq
    return pl.pallas_call(
        functools.partial(_mem_attn_kernel, heads=MEM_HEADS),
        grid=(B, nq),
        in_specs=[pl.BlockSpec((tq, width), lambda b, i: (b * nq + i, 0)),
                  pl.BlockSpec((n_mem, 2 * width), lambda b, i: (b, 0))],
        out_specs=pl.BlockSpec((tq, width), lambda b, i: (b * nq + i, 0)),
        out_shape=jax.ShapeDtypeStruct((T, width), BF16),
        compiler_params=_params("parallel", "parallel"),
    )(q, kv)


def _router_kernel(x_ref, g_ref, whi_ref, wlo_ref, b_ref, oi_ref, og_ref, cnt_ref, carry_ref, *, n_experts):
    i = pl.program_id(0)
    tm = x_ref.shape[0]

    @pl.when(i == 0)
    def _():
        carry_ref[...] = jnp.zeros_like(carry_ref)

    h = _rms(x_ref[...], g_ref[...])
    h_hi = h.astype(BF16)
    h_lo = (h - h_hi.astype(F32)).astype(BF16)
    logits = (jnp.dot(h_hi, whi_ref[...], preferred_element_type=F32)
              + jnp.dot(h_hi, wlo_ref[...], preferred_element_type=F32)
              + jnp.dot(h_lo, whi_ref[...], preferred_element_type=F32)) + b_ref[...]
    lane = lax.broadcasted_iota(jnp.int32, (tm, LANES), 1).astype(F32)
    work = jnp.where(lane < n_experts, logits, -jnp.inf)
    vals, idxs = [], []
    for _ in range(TOP_K):
        m = jnp.max(work, axis=1, keepdims=True)
        idx = jnp.min(jnp.where(work == m, lane, float(LANES)), axis=1, keepdims=True)
        vals.append(m)
        idxs.append(idx)
        work = jnp.where(lane == idx, -jnp.inf, work)
    exps = [jnp.exp(v - vals[0]) for v in vals]
    denom = exps[0] + exps[1] + exps[2] + exps[3]
    onehots = [(lane == idx).astype(F32) for idx in idxs]
    chosen = onehots[0] + onehots[1] + onehots[2] + onehots[3]
    row = lax.broadcasted_iota(jnp.int32, (tm, tm), 0)
    col = lax.broadcasted_iota(jnp.int32, (tm, tm), 1)
    before = jnp.dot((col < row).astype(BF16), chosen.astype(BF16), preferred_element_type=F32) + carry_ref[...]
    out_i = jnp.zeros((tm, LANES), F32)
    out_g = jnp.zeros((tm, LANES), F32)
    for k in range(TOP_K):
        rank = jnp.sum(onehots[k] * before, axis=1, keepdims=True)
        out_i = jnp.where(lane == k, idxs[k], out_i)
        out_i = jnp.where(lane == TOP_K + k, rank, out_i)
        out_g = jnp.where(lane == k, exps[k] / denom, out_g)
    oi_ref[...] = out_i.astype(jnp.int32)
    og_ref[...] = out_g
    carry_ref[...] = carry_ref[...] + jnp.sum(chosen, axis=0, keepdims=True)
    cnt_ref[...] = carry_ref[...]


def _router(x, g, w_router, b_router, *, tm=512):
    T, D = x.shape
    E = w_router.shape[1]
    tm = _tile(T, tm)
    w_pad = jnp.zeros((D, LANES), F32).at[:, :E].set(w_router)
    w_hi = w_pad.astype(BF16)
    w_lo = (w_pad - w_hi.astype(F32)).astype(BF16)
    b_pad = jnp.zeros((1, LANES), F32).at[0, :E].set(b_router)
    full = lambda i: (0, 0)
    return pl.pallas_call(
        functools.partial(_router_kernel, n_experts=E),
        grid=(T // tm,),
        in_specs=[pl.BlockSpec((tm, D), lambda i: (i, 0)), pl.BlockSpec((1, D), full),
                  pl.BlockSpec((D, LANES), full), pl.BlockSpec((D, LANES), full), pl.BlockSpec((1, LANES), full)],
        out_specs=[pl.BlockSpec((tm, LANES), lambda i: (i, 0)), pl.BlockSpec((tm, LANES), lambda i: (i, 0)),
                   pl.BlockSpec((1, LANES), full)],
        out_shape=[jax.ShapeDtypeStruct((T, LANES), jnp.int32), jax.ShapeDtypeStruct((T, LANES), F32),
                   jax.ShapeDtypeStruct((1, LANES), F32)],
        scratch_shapes=[pltpu.VMEM((1, LANES), F32)],
        compiler_params=_params("arbitrary"),
    )(x, g.reshape(1, D), w_hi, w_lo, b_pad)


def _pack_bf16_pairs(h):
    half = h.shape[1] // 2
    bits = pltpu.bitcast(h.astype(BF16).astype(F32), jnp.uint32)
    return (bits[:, :half] >> 16) | bits[:, half:]


def _unpack_bf16_pairs(u):
    lo = pltpu.bitcast(u << 16, F32).astype(BF16)
    hi = pltpu.bitcast(u & jnp.uint32(0xFFFF0000), F32).astype(BF16)
    return lo, hi


def _scatter_norm_kernel(pe_ref, pd_ref, prev_ref, dest_ref, x_ref, g_ref, xs_hbm, buf0, buf1, buf2, zbuf, sem, zsem,
                         *, tm, n_experts, n_steps):
    i = pl.program_id(0)
    tb = x_ref.shape[0]
    bufs = (buf0, buf1, buf2)

    def issue(rows_ref, k):
        for t in range(tb):
            for j in range(TOP_K):
                pltpu.make_async_copy(bufs[k].at[pl.ds(t, 1)], xs_hbm.at[pl.ds(rows_ref[0, 0, t * TOP_K + j], 1)],
                                      sem.at[k]).start()

    def wait_slot(k):
        for _ in range(TOP_K):
            pltpu.make_async_copy(bufs[k], xs_hbm.at[pl.ds(0, tb)], sem.at[k]).wait()

    def finish(k):
        issue(dest_ref, k)
        if n_steps >= 3:
            wait_slot((k + 1) % 3)
        if n_steps >= 2:
            wait_slot((k + 2) % 3)
        wait_slot(k)

    def zero_group(first_row):
        return pltpu.make_async_copy(zbuf, xs_hbm.at[pl.ds(pl.multiple_of(first_row, tm), tm)], zsem)

    @pl.when(i == 0)
    def _():
        zbuf[...] = jnp.zeros_like(zbuf)
        n_groups = xs_hbm.shape[0] // tm
        used = pe_ref[n_experts - 1] // tm
        for e in range(n_experts):
            @pl.when(pd_ref[e] > 0)
            def _(e=e):
                zero_group(pe_ref[e] - tm).start()
        lax.fori_loop(used, n_groups, lambda gi, c: (zero_group(gi * tm).start(), c)[1], 0)
        for e in range(n_experts):
            @pl.when(pd_ref[e] > 0)
            def _(e=e):
                zero_group(pe_ref[e] - tm).wait()
        lax.fori_loop(used, n_groups, lambda gi, c: (zero_group(gi * tm).wait(), c)[1], 0)

    @pl.when(i == 0)
    def _():
        buf0[...] = _pack_bf16_pairs(_rms(x_ref[...], g_ref[...]))
        if n_steps == 1:
            finish(0)

    for k in range(3):
        @pl.when(jnp.logical_and(i % 3 == k, i > 0))
        def _(k=k):
            @pl.when(i >= 3)
            def _():
                wait_slot(k)

            bufs[k][...] = _pack_bf16_pairs(_rms(x_ref[...], g_ref[...]))
            issue(prev_ref, (k + 2) % 3)

            @pl.when(i == n_steps - 1)
            def _():
                finish(k)


def _scatter_norm(x, g, dest, pad_ends, padded, n_rows, *, tm, tb=128):
    T, D = x.shape
    E = pad_ends.shape[0]
    tb = _tile(T, tb)
    nb = T // tb
    grid_spec = pltpu.PrefetchScalarGridSpec(
        num_scalar_prefetch=2, grid=(nb,),
        in_specs=[pl.BlockSpec((1, 1, TOP_K * tb), lambda i, pe, pd: (jnp.maximum(i - 1, 0), 0, 0),
                               memory_space=pltpu.SMEM),
                  pl.BlockSpec((1, 1, TOP_K * tb), lambda i, pe, pd: (i, 0, 0), memory_space=pltpu.SMEM),
                  pl.BlockSpec((tb, D), lambda i, pe, pd: (i, 0)),
                  pl.BlockSpec((1, D), lambda i, pe, pd: (0, 0))],
        out_specs=pl.BlockSpec(memory_space=pl.ANY),
        scratch_shapes=[pltpu.VMEM((tb, D // 2), jnp.uint32)] * 3 + [pltpu.VMEM((tm, D // 2), jnp.uint32),
                        pltpu.SemaphoreType.DMA((3,)), pltpu.SemaphoreType.DMA(())])
    dest3 = dest.reshape(nb, 1, TOP_K * tb)
    return pl.pallas_call(
        functools.partial(_scatter_norm_kernel, tm=tm, n_experts=E, n_steps=nb), grid_spec=grid_spec,
        out_shape=jax.ShapeDtypeStruct((n_rows, D // 2), jnp.uint32),
        compiler_params=_params("arbitrary"),
    )(pad_ends, padded, dest3, dest3, x, g.reshape(1, D))


def _combine_kernel(idx0_ref, idx1_ref, idx2_ref, x_ref, gate_ref, g_ref, rows_hbm, o_ref, buf0, buf1, buf2, sem, *,
                    final_norm):
    i = pl.program_id(0)
    n = pl.num_programs(0)
    tb = x_ref.shape[0]
    n_rows = TOP_K * tb
    bufs = (buf0, buf1, buf2)

    def issue(idx_ref, k):
        for r in range(n_rows):
            pltpu.make_async_copy(rows_hbm.at[pl.ds(idx_ref[0, 0, r], 1)], bufs[k].at[pl.ds(r, 1)], sem.at[k]).start()

    def wait(k):
        pltpu.make_async_copy(rows_hbm.at[pl.ds(0, n_rows)], bufs[k], sem.at[k]).wait()

    @pl.when(i == 0)
    def _():
        issue(idx0_ref, 0)
        issue(idx1_ref, 1)

    for k in range(3):
        @pl.when(i % 3 == k)
        def _(k=k):
            wait(k)
            issue(idx2_ref, (k + 2) % 3)
            y = x_ref[...]
            gates = gate_ref[...]
            for j in range(TOP_K):
                y = y + gates[:, j:j + 1] * bufs[k][pl.ds(j * tb, tb), :]
            o_ref[...] = _rms(y, g_ref[...]) if final_norm else y

            @pl.when(i == n - 1)
            def _():
                wait((k + 1) % 3)
                wait((k + 2) % 3)


def _combine(x, rows, dest, gates_slab, g, final_norm, *, tb=64):
    T, D = x.shape
    tb = _tile(T, tb)
    nb = T // tb
    idx3 = dest.reshape(nb, tb, TOP_K).transpose(0, 2, 1).reshape(nb, 1, TOP_K * tb)
    return pl.pallas_call(
        functools.partial(_combine_kernel, final_norm=final_norm),
        grid=(nb,),
        in_specs=[pl.BlockSpec((1, 1, TOP_K * tb), lambda i, a=ahead: (jnp.minimum(i + a, nb - 1), 0, 0),
                               memory_space=pltpu.SMEM) for ahead in range(3)] + [
                  pl.BlockSpec((tb, D), lambda i: (i, 0)),
                  pl.BlockSpec((tb, LANES), lambda i: (i, 0)),
                  pl.BlockSpec((1, D), lambda i: (0, 0)),
                  pl.BlockSpec(memory_space=pl.ANY)],
        out_specs=pl.BlockSpec((tb, D), lambda i: (i, 0)),
        out_shape=jax.ShapeDtypeStruct((T, D), F32),
        scratch_shapes=[pltpu.VMEM((TOP_K * tb, D), F32)] * 3 + [pltpu.SemaphoreType.DMA((3,))],
        compiler_params=_params("arbitrary"),
    )(idx3, idx3, idx3, x, gates_slab, g.reshape(1, D), rows)


MOE_GROUP_ROWS = 256
MOE_GROUPS_PER_STEP = 5
MOE_BLOCK_SIZES = (5, 4, 1)
MOE_BLOCK_GROUPS = max(MOE_BLOCK_SIZES)
MOE_UP_COLS = 512
MOE_DOWN_COLS = 512


def _swiglu_tile(lo, hi, wb_ref, b):
    half = lo.shape[1]
    pair = 2 * LANES
    row = lax.broadcasted_iota(jnp.int32, (pair, LANES), 0)
    col = lax.broadcasted_iota(jnp.int32, (pair, LANES), 1)
    pick_even = (row == 2 * col).astype(BF16)
    even = lax.broadcasted_iota(jnp.int32, (1, pair), 1) % 2 == 0
    gu = (jnp.dot(lo, wb_ref[:half, :], preferred_element_type=F32)
          + jnp.dot(hi, wb_ref[half:, :], preferred_element_type=F32)) + b
    outs = []
    for c in range(gu.shape[1] // pair):
        blk = gu[:, c * pair:(c + 1) * pair]
        nxt = pltpu.roll(blk, pair - 1, axis=1)
        gate = jnp.minimum(blk, SWIGLU_LIMIT)
        up = jnp.clip(nxt, -SWIGLU_LIMIT, SWIGLU_LIMIT)
        act = (up + 1.0) * gate * jax.nn.sigmoid(SWIGLU_ALPHA * gate)
        act = jnp.where(even, act, 0.0).astype(BF16)
        outs.append(jnp.dot(act, pick_even, preferred_element_type=F32).astype(BF16))
    return outs[0] if len(outs) == 1 else jnp.concatenate(outs, axis=1)


def _experts_kernel(se_ref, sr_ref, sn_ref, cnt_ref, bgu_ref, bd_ref, xs_hbm, wgu_hbm, wd_hbm, rows_hbm,
                    xs_v, act_v, wu_f, wu_b, wd_f, wd_b, out_v, xs_sem, wu_sem, wd_sem, out_sem, *, tm):
    s = pl.program_id(0)
    n_active = cnt_ref[0]
    tnu, tnd = wu_b.shape[1], wd_b.shape[1]
    n_up, n_down = wgu_hbm.shape[2] // tnu, wd_hbm.shape[2] // tnd

    def wu_copy(step, c, slot):
        cols = pl.ds(pl.multiple_of(c * tnu, tnu), tnu)
        return pltpu.make_async_copy(wgu_hbm.at[se_ref[step], :, cols], wu_f.at[slot], wu_sem.at[slot])

    def wd_copy(step, c, slot):
        cols = pl.ds(pl.multiple_of(c * tnd, tnd), tnd)
        return pltpu.make_async_copy(wd_hbm.at[se_ref[step], :, cols], wd_f.at[slot], wd_sem.at[slot])

    def xs_copy(step, g):
        return pltpu.make_async_copy(xs_hbm.at[pl.ds(pl.multiple_of(sr_ref[step] + g * tm, tm), tm)],
                                     xs_v.at[pl.ds(pl.multiple_of(g * tm, tm), tm)], xs_sem)

    def out_copy(slot, row, c):
        cols = pl.ds(pl.multiple_of(c * tnd, tnd), tnd)
        return pltpu.make_async_copy(
            out_v.at[slot], rows_hbm.at[pl.ds(pl.multiple_of(row, tm), tm), cols], out_sem.at[slot])

    def start_rows(step):
        lax.fori_loop(0, sn_ref[step], lambda g, carry: (xs_copy(step, g).start(), carry)[1], 0)

    def for_row_blocks(n_groups, fn, carry):
        done = 0
        for size in MOE_BLOCK_SIZES:
            n_blocks = (n_groups - done) // size
            carry = lax.fori_loop(0, n_blocks, lambda b, cr, d=done, sz=size: fn(d + b * sz, sz, cr), carry)
            done = done + n_blocks * size
        return carry

    @pl.when(s < n_active)
    def _():
        ng = sn_ref[s]
        row0 = sr_ref[s]

        @pl.when(s == 0)
        def _():
            start_rows(0)
            wu_copy(0, 0, 0).start()

        lax.fori_loop(0, ng, lambda g, carry: (xs_copy(s, g).wait(), carry)[1], 0)

        def up_chunk(c, carry):
            slot = c % 2
            wu_copy(s, c, slot).wait()

            @pl.when(c + 1 < n_up)
            def _():
                wu_copy(s, c + 1, 1 - slot).start()

            @pl.when(c + 1 == n_up)
            def _():
                wd_copy(s, 0, 0).start()

            wu_b[...] = wu_f[slot].astype(BF16)
            bias = bgu_ref[c]

            def up_rows(g0, n_g, carry):
                rows = pl.ds(pl.multiple_of(g0 * tm, tm), n_g * tm)
                lo, hi = _unpack_bf16_pairs(xs_v[rows, :])
                tile = _swiglu_tile(lo, hi, wu_b, bias)
                for cc in range(n_up):
                    @pl.when(c == cc)
                    def _(cc=cc):
                        act_v[rows, cc * (tnu // 2):(cc + 1) * (tnu // 2)] = tile
                return carry
            return for_row_blocks(ng, up_rows, carry)
        lax.fori_loop(0, n_up, up_chunk, 0)

        @pl.when(s + 1 < n_active)
        def _():
            start_rows(s + 1)

        def drain(pending):
            for k in range(MOE_BLOCK_GROUPS):
                @pl.when(pending[k] == 1)
                def _(k=k):
                    out_copy(k, 0, 0).wait()

        def down_chunk(c, pending):
            slot = c % 2
            wd_copy(s, c, slot).wait()

            @pl.when(c + 1 < n_down)
            def _():
                wd_copy(s, c + 1, 1 - slot).start()

            @pl.when(jnp.logical_and(c + 1 == n_down, s + 1 < n_active))
            def _():
                wu_copy(s + 1, 0, 0).start()

            wd_b[...] = wd_f[slot].astype(BF16)
            bias = bd_ref[c]

            def down_rows(g0, n_g, pending):
                rows = pl.ds(pl.multiple_of(g0 * tm, tm), n_g * tm)
                o = jnp.dot(act_v[rows, :], wd_b[...], preferred_element_type=F32) + bias
                drain(pending)
                for k in range(n_g):
                    out_v[k] = o[k * tm:(k + 1) * tm]
                    out_copy(k, row0 + (g0 + k) * tm, c).start()
                return tuple(jnp.int32(1 if k < n_g else 0) for k in range(MOE_BLOCK_GROUPS))
            return for_row_blocks(ng, down_rows, pending)
        drain(lax.fori_loop(0, n_down, down_chunk, (jnp.int32(0),) * MOE_BLOCK_GROUPS))

    @pl.when(s == pl.num_programs(0) - 1)
    def _():
        out_v[0] = jnp.zeros(out_v.shape[1:], out_v.dtype)
        first, last = cnt_ref[1], rows_hbm.shape[0] // tm

        def fill(gi, carry, wait):
            for c in range(n_down):
                cp = out_copy(0, gi * tm, c)
                cp.wait() if wait else cp.start()
            return carry
        lax.fori_loop(first, last, functools.partial(fill, wait=False), 0)
        lax.fori_loop(first, last, functools.partial(fill, wait=True), 0)


def _experts(xs, w_gu, b_gu, w_d, b_d, step_expert, step_row, step_groups, counts, *, tm):
    R, half = xs.shape
    E, D, F2 = w_gu.shape
    F = F2 // 2
    tnu, tnd = min(MOE_UP_COLS, F2), min(MOE_DOWN_COLS, D)
    assert D == 2 * half and F2 % tnu == 0 and D % tnd == 0 and tnu % (2 * LANES) == 0 and D // tnd >= 2
    assert MOE_BLOCK_SIZES[-1] == 1 and MOE_BLOCK_GROUPS <= MOE_GROUPS_PER_STEP
    cap = MOE_GROUPS_PER_STEP * tm
    n_steps = step_expert.shape[0]
    expert_of = lambda s, se, sr, sn, cnt: (se[jnp.minimum(s, cnt[0] - 1)], 0, 0, 0)
    grid_spec = pltpu.PrefetchScalarGridSpec(
        num_scalar_prefetch=4, grid=(n_steps,),
        in_specs=[pl.BlockSpec((None, F2 // tnu, 1, tnu), expert_of),
                  pl.BlockSpec((None, D // tnd, 1, tnd), expert_of),
                  pl.BlockSpec(memory_space=pl.ANY), pl.BlockSpec(memory_space=pl.ANY),
                  pl.BlockSpec(memory_space=pl.ANY)],
        out_specs=pl.BlockSpec(memory_space=pl.ANY),
        scratch_shapes=[pltpu.VMEM((cap, half), jnp.uint32), pltpu.VMEM((cap, F), BF16),
                        pltpu.VMEM((2, D, tnu), F32), pltpu.VMEM((D, tnu), BF16),
                        pltpu.VMEM((2, F, tnd), F32), pltpu.VMEM((F, tnd), BF16),
                        pltpu.VMEM((MOE_BLOCK_GROUPS, tm, tnd), F32),
                        pltpu.SemaphoreType.DMA(()), pltpu.SemaphoreType.DMA((2,)),
                        pltpu.SemaphoreType.DMA((2,)), pltpu.SemaphoreType.DMA((MOE_BLOCK_GROUPS,))])
    return pl.pallas_call(
        functools.partial(_experts_kernel, tm=tm), grid_spec=grid_spec,
        out_shape=jax.ShapeDtypeStruct((R, D), F32),
        compiler_params=_params("arbitrary"),
    )(step_expert, step_row, step_groups, counts, b_gu.reshape(E, F2 // tnu, 1, tnu),
      b_d.reshape(E, D // tnd, 1, tnd), xs, w_gu, w_d)


def _moe(x, g_moe, w_router, b_router, w_gate_up, b_gate_up, w_down, b_down, g_final, final_norm):
    T, D = x.shape
    E = w_router.shape[1]
    A = T * TOP_K
    tm = min(MOE_GROUP_ROWS, A)
    slab_i, slab_g, cnt = _router(x, g_moe, w_router, b_router)
    idx = slab_i[:, :TOP_K]
    rank = slab_i[:, TOP_K:2 * TOP_K]
    counts = cnt[0, :E].astype(jnp.int32)
    groups = (counts + tm - 1) // tm
    pad_ends = jnp.cumsum(groups) * tm
    pad_starts = pad_ends - groups * tm
    dest = pad_starts[idx] + rank
    n_groups = (A + tm - 1) // tm + E
    R = n_groups * tm
    per = MOE_GROUPS_PER_STEP
    runs = (groups + per - 1) // per
    run_ends = jnp.cumsum(runs)
    n_steps = (n_groups + per - 1) // per + E
    step = jnp.arange(n_steps, dtype=jnp.int32)
    step_expert = jnp.minimum(jnp.sum(step[:, None] >= run_ends[None, :], axis=1), E - 1).astype(jnp.int32)
    local = step - (run_ends - runs)[step_expert]
    step_row = (pad_starts[step_expert] + local * per * tm).astype(jnp.int32)
    step_groups = jnp.clip(groups[step_expert] - local * per, 0, per).astype(jnp.int32)
    step_counts = jnp.stack([run_ends[-1], pad_ends[-1] // tm]).astype(jnp.int32)

    xs = _scatter_norm(x, g_moe, dest, pad_ends, groups * tm, R, tm=tm)
    rows = _experts(xs, w_gate_up, b_gate_up, w_down, b_down, step_expert, step_row, step_groups, step_counts, tm=tm)
    return _combine(x, rows, dest, slab_g, g_final, final_norm)


def kernel(x, mem, g_mix, w_in, b_in, w_pool, pool_scale, w_fox_o, w_out, g_mem_q, g_mem_kv, w_mem_q, w_mem_kv,
           w_mem_o, g_moe, w_router, b_router, w_gate_up, b_gate_up, w_down, b_down, g_final):
    B, S, D = x.shape
    T = B * S
    n_mem = mem.shape[1]
    depth, G, C, Do = w_pool.shape
    pool_w = G * C
    fox_w = w_fox_o.shape[1]
    H = w_in.shape[2] - pool_w - 3 * fox_w - 2 * D
    dh = fox_w // H
    off_q, off_f = pool_w, pool_w + 3 * fox_w
    off_gate = off_f + H
    xt = x.reshape(T, D)
    mt = mem.reshape(B * n_mem, D)
    for l in range(depth):
        wl, bl = jnp.swapaxes(w_in[l], 0, 1), b_in[l]
        h, lf_t = _norm(xt, g_mix[l], wl[off_f:off_gate].astype(BF16), bl[off_f:off_gate])
        u = _proj(h, wl, bl, w_t=True, col_off=0, n_cols=off_q, out_dtype=F32)
        qkv = _proj(h, wl, bl, w_t=True, col_off=off_q, n_cols=off_f - off_q, out_dtype=BF16)
        gate_base = off_gate // LANES * LANES
        gate0 = off_gate - gate_base
        gates = _proj(h, wl, bl, w_t=True, col_off=gate_base, mode="sigmoid", out_dtype=BF16)
        c_t = _forget_cumsum(lf_t, B, S)
        att = _fox_attention(qkv, c_t.transpose(0, 2, 1), c_t, B, S, H, dh)
        pp = _pool_mixer(u, w_pool[l], pool_scale[l], gates, gate0, S)
        merged = _proj(att, w_fox_o[l], None, pp, (gates, gate0 + D), mode="merge", out_dtype=BF16)
        xt = _proj(merged, w_out[l], None, xt, mode="residual", out_dtype=F32)
        kv = _proj(_norm(mt, g_mem_kv[l]), w_mem_kv[l], out_dtype=BF16)
        qm = _proj(_norm(xt, g_mem_q[l]), w_mem_q[l], out_dtype=BF16)
        om = _mem_attention(qm, kv, B, S, n_mem)
        xt = _proj(om, w_mem_o[l], None, xt, mode="residual", out_dtype=F32)
        xt = _moe(xt, g_moe[l], w_router[l], b_router[l], w_gate_up[l], b_gate_up[l], w_down[l], b_down[l],
                  g_final, final_norm=l == depth - 1)
    return xt.reshape(B, S, D)
```

```python
import functools

import jax
import jax.numpy as jnp
from jax import lax
from jax.experimental import pallas as pl
from jax.experimental.pallas import tpu as pltpu

F32 = jnp.float32
BF16 = jnp.bfloat16

EPS = 1e-5
POOL_WINDOWS = (2, 4, 8, 16)
POOL_HALO = 16
MEM_HEADS = 4
TOP_K = 4
SWIGLU_LIMIT = 7.0
SWIGLU_ALPHA = 1.702
NEG_BIG = -1e30
LOG2E = 1.4426950408889634

LANES = 128
VMEM_LIMIT_BYTES = 56 * 1024 * 1024


def _tile(dim, pref):
    t = pref
    while t >= 8:
        if dim % t == 0:
            return t
        t //= 2
    return dim


def _params(*sem):
    return pltpu.CompilerParams(dimension_semantics=sem, vmem_limit_bytes=VMEM_LIMIT_BYTES)


def _rms(x, g):
    ms = jnp.mean(x * x, axis=-1, keepdims=True)
    return x * lax.rsqrt(ms + EPS) * g


def _split3(x):
    hi = x.astype(BF16)
    r1 = x - hi.astype(F32)
    mid = r1.astype(BF16)
    lo = (r1 - mid.astype(F32)).astype(BF16)
    return hi, mid, lo


def _norm_kernel(x_ref, g_ref, *refs, with_f):
    h = _rms(x_ref[...], g_ref[...]).astype(BF16)
    if with_f:
        wf_ref, bf_ref, h_ref, lf_ref = refs
        f = lax.dot_general(wf_ref[...], h, (((1,), (1,)), ((), ())), preferred_element_type=F32) + bf_ref[...]
        lf_ref[...] = jnp.minimum(f, 0.0) - jnp.log1p(jnp.exp(-jnp.abs(f)))
    else:
        h_ref, = refs
    h_ref[...] = h


def _norm(x, g, wf_t=None, bf_t=None, *, tm=512):
    M, K = x.shape
    tm = _tile(M, tm)
    with_f = wf_t is not None
    in_specs = [pl.BlockSpec((tm, K), lambda i: (i, 0)), pl.BlockSpec((1, K), lambda i: (0, 0))]
    args = [x, g.reshape(1, K)]
    out_shape = [jax.ShapeDtypeStruct((M, K), BF16)]
    out_specs = [pl.BlockSpec((tm, K), lambda i: (i, 0))]
    if with_f:
        H = wf_t.shape[0]
        in_specs += [pl.BlockSpec((H, K), lambda i: (0, 0)), pl.BlockSpec((H, 1), lambda i: (0, 0))]
        args += [wf_t, bf_t.reshape(H, 1)]
        out_shape.append(jax.ShapeDtypeStruct((H, M), F32))
        out_specs.append(pl.BlockSpec((H, tm), lambda i: (0, i)))
    res = pl.pallas_call(
        functools.partial(_norm_kernel, with_f=with_f),
        grid=(M // tm,),
        in_specs=in_specs, out_specs=out_specs, out_shape=out_shape,
        compiler_params=_params("parallel"),
    )(*args)
    return res if with_f else res[0]


def _lane_window(main_ref, tail_ref, shift):
    if shift == 0:
        return main_ref[...].astype(F32)
    g = jnp.concatenate([main_ref[...], tail_ref[...]], axis=1).astype(F32)
    return pltpu.roll(g, g.shape[1] - shift, axis=1)[:, :main_ref.shape[1]]


def _window_specs(first_col, rows, width, row_of, col_block_of):
    base, shift = first_col // LANES * LANES, first_col % LANES
    assert base % width == 0
    specs = [pl.BlockSpec((rows, width), lambda *g: (row_of(*g), col_block_of(*g) + base // width))]
    if shift:
        specs.append(pl.BlockSpec(
            (rows, LANES), lambda *g: (row_of(*g), (base + (col_block_of(*g) + 1) * width) // LANES)))
    return specs, shift


def _proj_kernel(x_ref, w_ref, *refs, mode, has_bias, gate_shift, w_rows_are_outputs):
    refs = list(refs)
    wb_ref = refs.pop()
    o_ref = refs.pop()
    b_ref = refs.pop(0) if has_bias else None

    @pl.when(pl.program_id(1) == 0)
    def _():
        w = w_ref[...]
        wb_ref[...] = (w.T if w_rows_are_outputs else w).astype(BF16)

    acc = jnp.dot(x_ref[...], wb_ref[...], preferred_element_type=F32)
    if has_bias:
        acc = acc + b_ref[...]
    if mode == "sigmoid":
        acc = jax.nn.sigmoid(acc)
    elif mode == "merge":
        pp_ref, g_ref = refs[0], refs[1]
        acc = pp_ref[...] + _lane_window(g_ref, refs[2] if gate_shift else None, gate_shift) * acc
    elif mode == "residual":
        acc = refs[0][...] + acc
    o_ref[...] = acc.astype(o_ref.dtype)


def _proj(x, w, b=None, extra=None, gate=None, *, w_t=False, col_off=0, n_cols=None, mode="plain", out_dtype,
          tm=1024, tn=512):
    M, K = x.shape
    N = w.shape[0 if w_t else 1] - col_off if n_cols is None else n_cols
    tm = _tile(M, tm)
    tn = min(tn, N)
    while col_off % tn:
        tn //= 2
    assert tn % LANES == 0
    if w_t:
        w_spec = pl.BlockSpec((tn, K), lambda j, i, o=col_off // tn: (j + o, 0))
    else:
        w_spec = pl.BlockSpec((K, tn), lambda j, i, o=col_off // tn: (0, j + o))
    in_specs = [pl.BlockSpec((tm, K), lambda j, i: (i, 0)), w_spec]
    args = [x, w]
    if b is not None:
        in_specs.append(pl.BlockSpec((1, tn), lambda j, i, o=col_off // tn: (0, j + o)))
        args.append(b.reshape(1, -1))
    if extra is not None:
        in_specs.append(pl.BlockSpec((tm, tn), lambda j, i: (i, j)))
        args.append(extra)
    gate_shift = 0
    if gate is not None:
        specs, gate_shift = _window_specs(gate[1], tm, tn, lambda j, i: i, lambda j, i: j)
        in_specs += specs
        args += [gate[0]] * len(specs)
    return pl.pallas_call(
        functools.partial(_proj_kernel, mode=mode, has_bias=b is not None, gate_shift=gate_shift,
                          w_rows_are_outputs=w_t),
        grid=(pl.cdiv(N, tn), M // tm),
        in_specs=in_specs,
        out_specs=pl.BlockSpec((tm, tn), lambda j, i: (i, j)),
        out_shape=jax.ShapeDtypeStruct((M, N), out_dtype),
        scratch_shapes=[pltpu.VMEM((K, tn), BF16)],
        compiler_params=_params("parallel", "arbitrary"),
    )(*args)


def _cumsum_kernel(lf_ref, c_ref):
    S = lf_ref.shape[1]
    row = lax.broadcasted_iota(jnp.int32, (S, S), 0)
    col = lax.broadcasted_iota(jnp.int32, (S, S), 1)
    upper = (row <= col).astype(BF16)
    c = jnp.zeros(lf_ref.shape, F32)
    for part in _split3(lf_ref[...]):
        c = c + jnp.dot(part, upper, preferred_element_type=F32)
    c_ref[...] = c


def _forget_cumsum(lf_t, B, S):
    H = lf_t.shape[0]
    return pl.pallas_call(
        _cumsum_kernel,
        grid=(B,),
        in_specs=[pl.BlockSpec((H, S), lambda b: (0, b))],
        out_specs=pl.BlockSpec((None, H, S), lambda b: (b, 0, 0)),
        out_shape=jax.ShapeDtypeStruct((B, H, S), F32),
        compiler_params=_params("parallel"),
    )(lf_t)


def _pool_kernel(u_ref, halo_ref, w_ref, sc_ref, *refs, seq_len, gate_shift):
    g0_ref, g0_tail_ref = (refs[0], refs[1]) if gate_shift else (refs[0], None)
    o_ref, ext_ref, tmp_ref, pooled_ref = refs[-4:]
    g = pl.program_id(0)
    i = pl.program_id(1)
    tp = u_ref.shape[0]
    n_ext = tp + POOL_HALO
    pos0 = (i * tp) % seq_len
    u = u_ref[...]
    ext_ref[pl.ds(POOL_HALO, tp), :] = u
    ext_ref[pl.ds(0, POOL_HALO), :] = jnp.where(pos0 == 0, 0.0, halo_ref[...])
    pos = pos0 + lax.broadcasted_iota(jnp.int32, (tp, 1), 0)
    for gi, win in enumerate(POOL_WINDOWS):
        @pl.when(g == gi)
        def _(win=win):
            assert win & (win - 1) == 0 and win <= POOL_HALO
            bufs, first, shift = (ext_ref, tmp_ref), 0, 1
            while 2 * shift < win:
                src, dst = bufs
                first += shift
                dst[pl.ds(first, n_ext - first), :] = (src[pl.ds(first, n_ext - first), :]
                                                       + src[pl.ds(first - shift, n_ext - first), :])
                bufs, shift = (dst, src), 2 * shift
            src = bufs[0]
            acc = src[pl.ds(POOL_HALO, tp), :] + src[pl.ds(POOL_HALO - shift, tp), :]
            cnt = jnp.minimum(pos + 1, win).astype(F32)
            pooled_ref[...] = (acc / cnt - u).astype(BF16)
    y = jnp.dot(pooled_ref[...], w_ref[...].astype(BF16), preferred_element_type=F32)
    o_ref[...] = _lane_window(g0_ref, g0_tail_ref, gate_shift) * (y * sc_ref[...])


def _pool_mixer(u, w_pool, scale, gates, gate0_col, seq_len, *, tp=512):
    T = u.shape[0]
    G, C, Do = w_pool.shape
    tp = _tile(seq_len, tp)
    assert tp % POOL_HALO == 0
    hb = tp // POOL_HALO
    gate_specs, gate_shift = _window_specs(gate0_col, tp, Do, lambda g, i: i, lambda g, i: g)
    return pl.pallas_call(
        functools.partial(_pool_kernel, seq_len=seq_len, gate_shift=gate_shift),
        grid=(G, T // tp),
        in_specs=[pl.BlockSpec((tp, C), lambda g, i: (i, g)),
                  pl.BlockSpec((POOL_HALO, C), lambda g, i: (jnp.maximum(i * hb - 1, 0), g)),
                  pl.BlockSpec((None, C, Do), lambda g, i: (g, 0, 0)),
                  pl.BlockSpec((1, Do), lambda g, i: (0, g))] + gate_specs,
        out_specs=pl.BlockSpec((tp, Do), lambda g, i: (i, g)),
        out_shape=jax.ShapeDtypeStruct((T, G * Do), F32),
        scratch_shapes=[pltpu.VMEM((tp + POOL_HALO, C), F32), pltpu.VMEM((tp + POOL_HALO, C), F32),
                        pltpu.VMEM((tp, C), BF16)],
        compiler_params=_params("parallel", "parallel"),
    )(u, u, w_pool, scale.reshape(1, G * Do), *([gates] * len(gate_specs)))


def _fox_kernel(q_ref, k_ref, v_ref, cq_ref, ck_ref, o_ref, *, scale, dh):
    hg = pl.program_id(1)
    qi = pl.program_id(2)
    tq = q_ref.shape[0]
    n_heads = q_ref.shape[1] // dh
    lane = lax.broadcasted_iota(jnp.int32, cq_ref.shape, 1)
    cq_all = cq_ref[...]
    q = [q_ref[:, n * dh:(n + 1) * dh] for n in range(n_heads)]
    cq2 = [LOG2E * jnp.sum(jnp.where(lane == hg * n_heads + n, cq_all, 0.0), axis=1, keepdims=True)
           for n in range(n_heads)]

    def scores(n, j):
        rows = pl.ds(pl.multiple_of(j * tq, tq), tq)
        s = lax.dot_general(q[n], k_ref[rows, n * dh:(n + 1) * dh], (((1,), (1,)), ((), ())),
                            preferred_element_type=F32) * (scale * LOG2E)
        return s - LOG2E * ck_ref[n:n + 1, rows]

    def update(n, j, s, carry):
        m, l, acc = carry
        m_new = jnp.maximum(m, jnp.max(s, axis=1, keepdims=True) + cq2[n])
        alpha = jnp.exp2(m - m_new)
        p = jnp.exp2(s - (m_new - cq2[n]))
        v = v_ref[pl.ds(pl.multiple_of(j * tq, tq), tq), n * dh:(n + 1) * dh]
        acc = alpha * acc + jnp.dot(p.astype(BF16), v, preferred_element_type=F32)
        return m_new, alpha * l + jnp.sum(p, axis=1, keepdims=True), acc

    init = (jnp.full((tq, 1), NEG_BIG, F32), jnp.zeros((tq, 1), F32), jnp.zeros((tq, dh), F32))
    carries = lax.fori_loop(
        0, qi, lambda j, cs: tuple(update(n, j, scores(n, j), cs[n]) for n in range(n_heads)), (init,) * n_heads)
    row = lax.broadcasted_iota(jnp.int32, (tq, tq), 0)
    col = lax.broadcasted_iota(jnp.int32, (tq, tq), 1)
    for n in range(n_heads):
        s = jnp.where(col <= row, scores(n, qi), NEG_BIG)
        m, l, acc = update(n, qi, s, carries[n])
        o_ref[:, n * dh:(n + 1) * dh] = (acc / l).astype(o_ref.dtype)


def _fox_attention(qkv, c, c_t, B, S, H, dh, *, tq=512, heads_per_step=2):
    T = B * S
    tq = _tile(S, tq)
    nq = S // tq
    hp = heads_per_step if H % heads_per_step == 0 else 1
    ng = H // hp
    w = hp * dh
    return pl.pallas_call(
        functools.partial(_fox_kernel, scale=dh ** -0.5, dh=dh),
        grid=(B, ng, nq),
        in_specs=[pl.BlockSpec((tq, w), lambda b, h, i: (b * nq + i, h)),
                  pl.BlockSpec((S, w), lambda b, h, i: (b, ng + h)),
                  pl.BlockSpec((S, w), lambda b, h, i: (b, 2 * ng + h)),
                  pl.BlockSpec((None, tq, H), lambda b, h, i: (b, i, 0)),
                  pl.BlockSpec((None, None, hp, S), lambda b, h, i: (b, h, 0, 0))],
        out_specs=pl.BlockSpec((tq, w), lambda b, h, i: (b * nq + i, h)),
        out_shape=jax.ShapeDtypeStruct((T, H * dh), BF16),
        compiler_params=_params("parallel", "parallel", "parallel"),
    )(qkv, qkv, qkv, c, c_t.reshape(B, ng, hp, S))


def _mem_attn_kernel(q_ref, kv_ref, o_ref, *, heads):
    width = q_ref.shape[1]
    dh = width // heads
    scale = dh ** -0.5
    for hd in range(heads):
        q = q_ref[:, hd * dh:(hd + 1) * dh]
        k = kv_ref[:, hd * dh:(hd + 1) * dh]
        v = kv_ref[:, width + hd * dh:width + (hd + 1) * dh]
        s = lax.dot_general(q, k, (((1,), (1,)), ((), ())), preferred_element_type=F32) * scale
        p = jnp.exp(s - jnp.max(s, axis=1, keepdims=True))
        p = p / jnp.sum(p, axis=1, keepdims=True)
        o_ref[:, hd * dh:(hd + 1) * dh] = jnp.dot(p.astype(BF16), v, preferred_element_type=F32).astype(o_ref.dtype)


def _mem_attention(q, kv, B, S, n_mem, *, tq=512):
    T, width = q.shape
    tq = _tile(S, tq)
    nq = S // tq
    return pl.pallas_call(
        functools.partial(_mem_attn_kernel, heads=MEM_HEADS),
        grid=(B, nq),
        in_specs=[pl.BlockSpec((tq, width), lambda b, i: (b * nq + i, 0)),
                  pl.BlockSpec((n_mem, 2 * width), lambda b, i: (b, 0))],
        out_specs=pl.BlockSpec((tq, width), lambda b, i: (b * nq + i, 0)),
        out_shape=jax.ShapeDtypeStruct((T, width), BF16),
        compiler_params=_params("parallel", "parallel"),
    )(q, kv)


def _router_kernel(x_ref, g_ref, whi_ref, wlo_ref, b_ref, oi_ref, og_ref, cnt_ref, carry_ref, *, n_experts):
    i = pl.program_id(0)
    tm = x_ref.shape[0]

    @pl.when(i == 0)
    def _():
        carry_ref[...] = jnp.zeros_like(carry_ref)

    h = _rms(x_ref[...], g_ref[...])
    h_hi = h.astype(BF16)
    h_lo = (h - h_hi.astype(F32)).astype(BF16)
    logits = (jnp.dot(h_hi, whi_ref[...], preferred_element_type=F32)
              + jnp.dot(h_hi, wlo_ref[...], preferred_element_type=F32)
              + jnp.dot(h_lo, whi_ref[...], preferred_element_type=F32)) + b_ref[...]
    lane = lax.broadcasted_iota(jnp.int32, (tm, LANES), 1).astype(F32)
    work = jnp.where(lane < n_experts, logits, -jnp.inf)
    vals, idxs = [], []
    for _ in range(TOP_K):
        m = jnp.max(work, axis=1, keepdims=True)
        idx = jnp.min(jnp.where(work == m, lane, float(LANES)), axis=1, keepdims=True)
        vals.append(m)
        idxs.append(idx)
        work = jnp.where(lane == idx, -jnp.inf, work)
    exps = [jnp.exp(v - vals[0]) for v in vals]
    denom = exps[0] + exps[1] + exps[2] + exps[3]
    onehots = [(lane == idx).astype(F32) for idx in idxs]
    chosen = onehots[0] + onehots[1] + onehots[2] + onehots[3]
    row = lax.broadcasted_iota(jnp.int32, (tm, tm), 0)
    col = lax.broadcasted_iota(jnp.int32, (tm, tm), 1)
    before = jnp.dot((col < row).astype(BF16), chosen.astype(BF16), preferred_element_type=F32) + carry_ref[...]
    out_i = jnp.zeros((tm, LANES), F32)
    out_g = jnp.zeros((tm, LANES), F32)
    for k in range(TOP_K):
        rank = jnp.sum(onehots[k] * before, axis=1, keepdims=True)
        out_i = jnp.where(lane == k, idxs[k], out_i)
        out_i = jnp.where(lane == TOP_K + k, rank, out_i)
        out_g = jnp.where(lane == k, exps[k] / denom, out_g)
    oi_ref[...] = out_i.astype(jnp.int32)
    og_ref[...] = out_g
    carry_ref[...] = carry_ref[...] + jnp.sum(chosen, axis=0, keepdims=True)
    cnt_ref[...] = carry_ref[...]


def _router(x, g, w_router, b_router, *, tm=512):
    T, D = x.shape
    E = w_router.shape[1]
    tm = _tile(T, tm)
    w_pad = jnp.zeros((D, LANES), F32).at[:, :E].set(w_router)
    w_hi = w_pad.astype(BF16)
    w_lo = (w_pad - w_hi.astype(F32)).astype(BF16)
    b_pad = jnp.zeros((1, LANES), F32).at[0, :E].set(b_router)
    full = lambda i: (0, 0)
    return pl.pallas_call(
        functools.partial(_router_kernel, n_experts=E),
        grid=(T // tm,),
        in_specs=[pl.BlockSpec((tm, D), lambda i: (i, 0)), pl.BlockSpec((1, D), full),
                  pl.BlockSpec((D, LANES), full), pl.BlockSpec((D, LANES), full), pl.BlockSpec((1, LANES), full)],
        out_specs=[pl.BlockSpec((tm, LANES), lambda i: (i, 0)), pl.BlockSpec((tm, LANES), lambda i: (i, 0)),
                   pl.BlockSpec((1, LANES), full)],
        out_shape=[jax.ShapeDtypeStruct((T, LANES), jnp.int32), jax.ShapeDtypeStruct((T, LANES), F32),
                   jax.ShapeDtypeStruct((1, LANES), F32)],
        scratch_shapes=[pltpu.VMEM((1, LANES), F32)],
        compiler_params=_params("arbitrary"),
    )(x, g.reshape(1, D), w_hi, w_lo, b_pad)


def _pack_bf16_pairs(h):
    half = h.shape[1] // 2
    bits = pltpu.bitcast(h.astype(BF16).astype(F32), jnp.uint32)
    return (bits[:, :half] >> 16) | bits[:, half:]


def _unpack_bf16_pairs(u):
    lo = pltpu.bitcast(u << 16, F32).astype(BF16)
    hi = pltpu.bitcast(u & jnp.uint32(0xFFFF0000), F32).astype(BF16)
    return lo, hi


def _scatter_norm_kernel(pe_ref, pd_ref, prev_ref, dest_ref, x_ref, g_ref, xs_hbm, buf0, buf1, buf2, zbuf, sem, zsem,
                         *, tm, n_experts, n_steps):
    i = pl.program_id(0)
    tb = x_ref.shape[0]
    bufs = (buf0, buf1, buf2)

    def issue(rows_ref, k):
        for t in range(tb):
            for j in range(TOP_K):
                pltpu.make_async_copy(bufs[k].at[pl.ds(t, 1)], xs_hbm.at[pl.ds(rows_ref[0, 0, t * TOP_K + j], 1)],
                                      sem.at[k]).start()

    def wait_slot(k):
        for _ in range(TOP_K):
            pltpu.make_async_copy(bufs[k], xs_hbm.at[pl.ds(0, tb)], sem.at[k]).wait()

    def finish(k):
        issue(dest_ref, k)
        if n_steps >= 3:
            wait_slot((k + 1) % 3)
        if n_steps >= 2:
            wait_slot((k + 2) % 3)
        wait_slot(k)

    def zero_group(first_row):
        return pltpu.make_async_copy(zbuf, xs_hbm.at[pl.ds(pl.multiple_of(first_row, tm), tm)], zsem)

    @pl.when(i == 0)
    def _():
        zbuf[...] = jnp.zeros_like(zbuf)
        n_groups = xs_hbm.shape[0] // tm
        used = pe_ref[n_experts - 1] // tm
        for e in range(n_experts):
            @pl.when(pd_ref[e] > 0)
            def _(e=e):
                zero_group(pe_ref[e] - tm).start()
        lax.fori_loop(used, n_groups, lambda gi, c: (zero_group(gi * tm).start(), c)[1], 0)
        for e in range(n_experts):
            @pl.when(pd_ref[e] > 0)
            def _(e=e):
                zero_group(pe_ref[e] - tm).wait()
        lax.fori_loop(used, n_groups, lambda gi, c: (zero_group(gi * tm).wait(), c)[1], 0)

    @pl.when(i == 0)
    def _():
        buf0[...] = _pack_bf16_pairs(_rms(x_ref[...], g_ref[...]))
        if n_steps == 1:
            finish(0)

    for k in range(3):
        @pl.when(jnp.logical_and(i % 3 == k, i > 0))
        def _(k=k):
            @pl.when(i >= 3)
            def _():
                wait_slot(k)

            bufs[k][...] = _pack_bf16_pairs(_rms(x_ref[...], g_ref[...]))
            issue(prev_ref, (k + 2) % 3)

            @pl.when(i == n_steps - 1)
            def _():
                finish(k)


def _scatter_norm(x, g, dest, pad_ends, padded, n_rows, *, tm, tb=128):
    T, D = x.shape
    E = pad_ends.shape[0]
    tb = _tile(T, tb)
    nb = T // tb
    grid_spec = pltpu.PrefetchScalarGridSpec(
        num_scalar_prefetch=2, grid=(nb,),
        in_specs=[pl.BlockSpec((1, 1, TOP_K * tb), lambda i, pe, pd: (jnp.maximum(i - 1, 0), 0, 0),
                               memory_space=pltpu.SMEM),
                  pl.BlockSpec((1, 1, TOP_K * tb), lambda i, pe, pd: (i, 0, 0), memory_space=pltpu.SMEM),
                  pl.BlockSpec((tb, D), lambda i, pe, pd: (i, 0)),
                  pl.BlockSpec((1, D), lambda i, pe, pd: (0, 0))],
        out_specs=pl.BlockSpec(memory_space=pl.ANY),
        scratch_shapes=[pltpu.VMEM((tb, D // 2), jnp.uint32)] * 3 + [pltpu.VMEM((tm, D // 2), jnp.uint32),
                        pltpu.SemaphoreType.DMA((3,)), pltpu.SemaphoreType.DMA(())])
    dest3 = dest.reshape(nb, 1, TOP_K * tb)
    return pl.pallas_call(
        functools.partial(_scatter_norm_kernel, tm=tm, n_experts=E, n_steps=nb), grid_spec=grid_spec,
        out_shape=jax.ShapeDtypeStruct((n_rows, D // 2), jnp.uint32),
        compiler_params=_params("arbitrary"),
    )(pad_ends, padded, dest3, dest3, x, g.reshape(1, D))


def _combine_kernel(idx0_ref, idx1_ref, idx2_ref, x_ref, gate_ref, g_ref, rows_hbm, o_ref, buf0, buf1, buf2, sem, *,
                    final_norm):
    i = pl.program_id(0)
    n = pl.num_programs(0)
    tb = x_ref.shape[0]
    n_rows = TOP_K * tb
    bufs = (buf0, buf1, buf2)

    def issue(idx_ref, k):
        for r in range(n_rows):
            pltpu.make_async_copy(rows_hbm.at[pl.ds(idx_ref[0, 0, r], 1)], bufs[k].at[pl.ds(r, 1)], sem.at[k]).start()

    def wait(k):
        pltpu.make_async_copy(rows_hbm.at[pl.ds(0, n_rows)], bufs[k], sem.at[k]).wait()

    @pl.when(i == 0)
    def _():
        issue(idx0_ref, 0)
        issue(idx1_ref, 1)

    for k in range(3):
        @pl.when(i % 3 == k)
        def _(k=k):
            wait(k)
            issue(idx2_ref, (k + 2) % 3)
            y = x_ref[...]
            gates = gate_ref[...]
            for j in range(TOP_K):
                y = y + gates[:, j:j + 1] * bufs[k][pl.ds(j * tb, tb), :]
            o_ref[...] = _rms(y, g_ref[...]) if final_norm else y

            @pl.when(i == n - 1)
            def _():
                wait((k + 1) % 3)
                wait((k + 2) % 3)


def _combine(x, rows, dest, gates_slab, g, final_norm, *, tb=64):
    T, D = x.shape
    tb = _tile(T, tb)
    nb = T // tb
    idx3 = dest.reshape(nb, tb, TOP_K).transpose(0, 2, 1).reshape(nb, 1, TOP_K * tb)
    return pl.pallas_call(
        functools.partial(_combine_kernel, final_norm=final_norm),
        grid=(nb,),
        in_specs=[pl.BlockSpec((1, 1, TOP_K * tb), lambda i, a=ahead: (jnp.minimum(i + a, nb - 1), 0, 0),
                               memory_space=pltpu.SMEM) for ahead in range(3)] + [
                  pl.BlockSpec((tb, D), lambda i: (i, 0)),
                  pl.BlockSpec((tb, LANES), lambda i: (i, 0)),
                  pl.BlockSpec((1, D), lambda i: (0, 0)),
                  pl.BlockSpec(memory_space=pl.ANY)],
        out_specs=pl.BlockSpec((tb, D), lambda i: (i, 0)),
        out_shape=jax.ShapeDtypeStruct((T, D), F32),
        scratch_shapes=[pltpu.VMEM((TOP_K * tb, D), F32)] * 3 + [pltpu.SemaphoreType.DMA((3,))],
        compiler_params=_params("arbitrary"),
    )(idx3, idx3, idx3, x, gates_slab, g.reshape(1, D), rows)


MOE_GROUP_ROWS = 256
MOE_GROUPS_PER_STEP = 5
MOE_BLOCK_SIZES = (5, 4, 1)
MOE_BLOCK_GROUPS = max(MOE_BLOCK_SIZES)
MOE_UP_COLS = 512
MOE_DOWN_COLS = 512


def _swiglu_tile(lo, hi, wb_ref, b):
    half = lo.shape[1]
    pair = 2 * LANES
    row = lax.broadcasted_iota(jnp.int32, (pair, LANES), 0)
    col = lax.broadcasted_iota(jnp.int32, (pair, LANES), 1)
    pick_even = (row == 2 * col).astype(BF16)
    even = lax.broadcasted_iota(jnp.int32, (1, pair), 1) % 2 == 0
    gu = (jnp.dot(lo, wb_ref[:half, :], preferred_element_type=F32)
          + jnp.dot(hi, wb_ref[half:, :], preferred_element_type=F32)) + b
    outs = []
    for c in range(gu.shape[1] // pair):
        blk = gu[:, c * pair:(c + 1) * pair]
        nxt = pltpu.roll(blk, pair - 1, axis=1)
        gate = jnp.minimum(blk, SWIGLU_LIMIT)
        up = jnp.clip(nxt, -SWIGLU_LIMIT, SWIGLU_LIMIT)
        act = (up + 1.0) * gate * jax.nn.sigmoid(SWIGLU_ALPHA * gate)
        act = jnp.where(even, act, 0.0).astype(BF16)
        outs.append(jnp.dot(act, pick_even, preferred_element_type=F32).astype(BF16))
    return outs[0] if len(outs) == 1 else jnp.concatenate(outs, axis=1)


def _experts_kernel(se_ref, sr_ref, sn_ref, cnt_ref, bgu_ref, bd_ref, xs_hbm, wgu_hbm, wd_hbm, rows_hbm,
                    xs_v, act_v, wu_f, wu_b, wd_f, wd_b, out_v, xs_sem, wu_sem, wd_sem, out_sem, *, tm):
    s = pl.program_id(0)
    n_active = cnt_ref[0]
    tnu, tnd = wu_b.shape[1], wd_b.shape[1]
    n_up, n_down = wgu_hbm.shape[2] // tnu, wd_hbm.shape[2] // tnd

    def wu_copy(step, c, slot):
        cols = pl.ds(pl.multiple_of(c * tnu, tnu), tnu)
        return pltpu.make_async_copy(wgu_hbm.at[se_ref[step], :, cols], wu_f.at[slot], wu_sem.at[slot])

    def wd_copy(step, c, slot):
        cols = pl.ds(pl.multiple_of(c * tnd, tnd), tnd)
        return pltpu.make_async_copy(wd_hbm.at[se_ref[step], :, cols], wd_f.at[slot], wd_sem.at[slot])

    def xs_copy(step, g):
        return pltpu.make_async_copy(xs_hbm.at[pl.ds(pl.multiple_of(sr_ref[step] + g * tm, tm), tm)],
                                     xs_v.at[pl.ds(pl.multiple_of(g * tm, tm), tm)], xs_sem)

    def out_copy(slot, row, c):
        cols = pl.ds(pl.multiple_of(c * tnd, tnd), tnd)
        return pltpu.make_async_copy(
            out_v.at[slot], rows_hbm.at[pl.ds(pl.multiple_of(row, tm), tm), cols], out_sem.at[slot])

    def start_rows(step):
        lax.fori_loop(0, sn_ref[step], lambda g, carry: (xs_copy(step, g).start(), carry)[1], 0)

    def for_row_blocks(n_groups, fn, carry):
        done = 0
        for size in MOE_BLOCK_SIZES:
            n_blocks = (n_groups - done) // size
            carry = lax.fori_loop(0, n_blocks, lambda b, cr, d=done, sz=size: fn(d + b * sz, sz, cr), carry)
            done = done + n_blocks * size
        return carry

    @pl.when(s < n_active)
    def _():
        ng = sn_ref[s]
        row0 = sr_ref[s]

        @pl.when(s == 0)
        def _():
            start_rows(0)
            wu_copy(0, 0, 0).start()

        lax.fori_loop(0, ng, lambda g, carry: (xs_copy(s, g).wait(), carry)[1], 0)

        def up_chunk(c, carry):
            slot = c % 2
            wu_copy(s, c, slot).wait()

            @pl.when(c + 1 < n_up)
            def _():
                wu_copy(s, c + 1, 1 - slot).start()

            @pl.when(c + 1 == n_up)
            def _():
                wd_copy(s, 0, 0).start()

            wu_b[...] = wu_f[slot].astype(BF16)
            bias = bgu_ref[c]

            def up_rows(g0, n_g, carry):
                rows = pl.ds(pl.multiple_of(g0 * tm, tm), n_g * tm)
                lo, hi = _unpack_bf16_pairs(xs_v[rows, :])
                tile = _swiglu_tile(lo, hi, wu_b, bias)
                for cc in range(n_up):
                    @pl.when(c == cc)
                    def _(cc=cc):
                        act_v[rows, cc * (tnu // 2):(cc + 1) * (tnu // 2)] = tile
                return carry
            return for_row_blocks(ng, up_rows, carry)
        lax.fori_loop(0, n_up, up_chunk, 0)

        @pl.when(s + 1 < n_active)
        def _():
            start_rows(s + 1)

        def drain(pending):
            for k in range(MOE_BLOCK_GROUPS):
                @pl.when(pending[k] == 1)
                def _(k=k):
                    out_copy(k, 0, 0).wait()

        def down_chunk(c, pending):
            slot = c % 2
            wd_copy(s, c, slot).wait()

            @pl.when(c + 1 < n_down)
            def _():
                wd_copy(s, c + 1, 1 - slot).start()

            @pl.when(jnp.logical_and(c + 1 == n_down, s + 1 < n_active))
            def _():
                wu_copy(s + 1, 0, 0).start()

            wd_b[...] = wd_f[slot].astype(BF16)
            bias = bd_ref[c]

            def down_rows(g0, n_g, pending):
                rows = pl.ds(pl.multiple_of(g0 * tm, tm), n_g * tm)
                o = jnp.dot(act_v[rows, :], wd_b[...], preferred_element_type=F32) + bias
                drain(pending)
                for k in range(n_g):
                    out_v[k] = o[k * tm:(k + 1) * tm]
                    out_copy(k, row0 + (g0 + k) * tm, c).start()
                return tuple(jnp.int32(1 if k < n_g else 0) for k in range(MOE_BLOCK_GROUPS))
            return for_row_blocks(ng, down_rows, pending)
        drain(lax.fori_loop(0, n_down, down_chunk, (jnp.int32(0),) * MOE_BLOCK_GROUPS))

    @pl.when(s == pl.num_programs(0) - 1)
    def _():
        out_v[0] = jnp.zeros(out_v.shape[1:], out_v.dtype)
        first, last = cnt_ref[1], rows_hbm.shape[0] // tm

        def fill(gi, carry, wait):
            for c in range(n_down):
                cp = out_copy(0, gi * tm, c)
                cp.wait() if wait else cp.start()
            return carry
        lax.fori_loop(first, last, functools.partial(fill, wait=False), 0)
        lax.fori_loop(first, last, functools.partial(fill, wait=True), 0)


def _experts(xs, w_gu, b_gu, w_d, b_d, step_expert, step_row, step_groups, counts, *, tm):
    R, half = xs.shape
    E, D, F2 = w_gu.shape
    F = F2 // 2
    tnu, tnd = min(MOE_UP_COLS, F2), min(MOE_DOWN_COLS, D)
    assert D == 2 * half and F2 % tnu == 0 and D % tnd == 0 and tnu % (2 * LANES) == 0 and D // tnd >= 2
    assert MOE_BLOCK_SIZES[-1] == 1 and MOE_BLOCK_GROUPS <= MOE_GROUPS_PER_STEP
    cap = MOE_GROUPS_PER_STEP * tm
    n_steps = step_expert.shape[0]
    expert_of = lambda s, se, sr, sn, cnt: (se[jnp.minimum(s, cnt[0] - 1)], 0, 0, 0)
    grid_spec = pltpu.PrefetchScalarGridSpec(
        num_scalar_prefetch=4, grid=(n_steps,),
        in_specs=[pl.BlockSpec((None, F2 // tnu, 1, tnu), expert_of),
                  pl.BlockSpec((None, D // tnd, 1, tnd), expert_of),
                  pl.BlockSpec(memory_space=pl.ANY), pl.BlockSpec(memory_space=pl.ANY),
                  pl.BlockSpec(memory_space=pl.ANY)],
        out_specs=pl.BlockSpec(memory_space=pl.ANY),
        scratch_shapes=[pltpu.VMEM((cap, half), jnp.uint32), pltpu.VMEM((cap, F), BF16),
                        pltpu.VMEM((2, D, tnu), F32), pltpu.VMEM((D, tnu), BF16),
                        pltpu.VMEM((2, F, tnd), F32), pltpu.VMEM((F, tnd), BF16),
                        pltpu.VMEM((MOE_BLOCK_GROUPS, tm, tnd), F32),
                        pltpu.SemaphoreType.DMA(()), pltpu.SemaphoreType.DMA((2,)),
                        pltpu.SemaphoreType.DMA((2,)), pltpu.SemaphoreType.DMA((MOE_BLOCK_GROUPS,))])
    return pl.pallas_call(
        functools.partial(_experts_kernel, tm=tm), grid_spec=grid_spec,
        out_shape=jax.ShapeDtypeStruct((R, D), F32),
        compiler_params=_params("arbitrary"),
    )(step_expert, step_row, step_groups, counts, b_gu.reshape(E, F2 // tnu, 1, tnu),
      b_d.reshape(E, D // tnd, 1, tnd), xs, w_gu, w_d)


def _moe(x, g_moe, w_router, b_router, w_gate_up, b_gate_up, w_down, b_down, g_final, final_norm):
    T, D = x.shape
    E = w_router.shape[1]
    A = T * TOP_K
    tm = min(MOE_GROUP_ROWS, A)
    slab_i, slab_g, cnt = _router(x, g_moe, w_router, b_router)
    idx = slab_i[:, :TOP_K]
    rank = slab_i[:, TOP_K:2 * TOP_K]
    counts = cnt[0, :E].astype(jnp.int32)
    groups = (counts + tm - 1) // tm
    pad_ends = jnp.cumsum(groups) * tm
    pad_starts = pad_ends - groups * tm
    dest = pad_starts[idx] + rank
    n_groups = (A + tm - 1) // tm + E
    R = n_groups * tm
    per = MOE_GROUPS_PER_STEP
    runs = (groups + per - 1) // per
    run_ends = jnp.cumsum(runs)
    n_steps = (n_groups + per - 1) // per + E
    step = jnp.arange(n_steps, dtype=jnp.int32)
    step_expert = jnp.minimum(jnp.sum(step[:, None] >= run_ends[None, :], axis=1), E - 1).astype(jnp.int32)
    local = step - (run_ends - runs)[step_expert]
    step_row = (pad_starts[step_expert] + local * per * tm).astype(jnp.int32)
    step_groups = jnp.clip(groups[step_expert] - local * per, 0, per).astype(jnp.int32)
    step_counts = jnp.stack([run_ends[-1], pad_ends[-1] // tm]).astype(jnp.int32)

    xs = _scatter_norm(x, g_moe, dest, pad_ends, groups * tm, R, tm=tm)
    rows = _experts(xs, w_gate_up, b_gate_up, w_down, b_down, step_expert, step_row, step_groups, step_counts, tm=tm)
    return _combine(x, rows, dest, slab_g, g_final, final_norm)


def kernel(x, mem, g_mix, w_in, b_in, w_pool, pool_scale, w_fox_o, w_out, g_mem_q, g_mem_kv, w_mem_q, w_mem_kv,
           w_mem_o, g_moe, w_router, b_router, w_gate_up, b_gate_up, w_down, b_down, g_final):
    B, S, D = x.shape
    T = B * S
    n_mem = mem.shape[1]
    depth, G, C, Do = w_pool.shape
    pool_w = G * C
    fox_w = w_fox_o.shape[1]
    H = w_in.shape[2] - pool_w - 3 * fox_w - 2 * D
    dh = fox_w // H
    off_q, off_f = pool_w, pool_w + 3 * fox_w
    off_gate = off_f + H
    xt = x.reshape(T, D)
    mt = mem.reshape(B * n_mem, D)
    for l in range(depth):
        wl, bl = jnp.swapaxes(w_in[l], 0, 1), b_in[l]
        h, lf_t = _norm(xt, g_mix[l], wl[off_f:off_gate].astype(BF16), bl[off_f:off_gate])
        u = _proj(h, wl, bl, w_t=True, col_off=0, n_cols=off_q, out_dtype=F32)
        qkv = _proj(h, wl, bl, w_t=True, col_off=off_q, n_cols=off_f - off_q, out_dtype=BF16)
        gate_base = off_gate // LANES * LANES
        gate0 = off_gate - gate_base
        gates = _proj(h, wl, bl, w_t=True, col_off=gate_base, mode="sigmoid", out_dtype=BF16)
        c_t = _forget_cumsum(lf_t, B, S)
        att = _fox_attention(qkv, c_t.transpose(0, 2, 1), c_t, B, S, H, dh)
        pp = _pool_mixer(u, w_pool[l], pool_scale[l], gates, gate0, S)
        merged = _proj(att, w_fox_o[l], None, pp, (gates, gate0 + D), mode="merge", out_dtype=BF16)
        xt = _proj(merged, w_out[l], None, xt, mode="residual", out_dtype=F32)
        kv = _proj(_norm(mt, g_mem_kv[l]), w_mem_kv[l], out_dtype=BF16)
        qm = _proj(_norm(xt, g_mem_q[l]), w_mem_q[l], out_dtype=BF16)
        om = _mem_attention(qm, kv, B, S, n_mem)
        xt = _proj(om, w_mem_o[l], None, xt, mode="residual", out_dtype=F32)
        xt = _moe(xt, g_moe[l], w_router[l], b_router[l], w_gate_up[l], b_gate_up[l], w_down[l], b_down[l],
                  g_final, final_norm=l == depth - 1)
    return xt.reshape(B, S, D)
```
